```python
import jax
import jax.numpy as jnp
from jax import lax
import numpy as np

D_MODEL = 1024
BATCH = 16
SEQ = 4096
DEPTH = 2

CTX_LEN = 256
GRID_W = 64
N_MOD = 6
NORM_EPS = 1e-6
F32 = jnp.float32

SSD_HEADS = 8
SSD_HEAD_DIM = 64
SSD_D_INNER = SSD_HEADS * SSD_HEAD_DIM
SSD_GROUPS = 2
SSD_STATE = 64
SSD_XBC = SSD_D_INNER + 2 * SSD_GROUPS * SSD_STATE
SSD_CONV_K = 3
SSD_CHUNK = 128

GLA_HEADS = 4
GLA_KEY_DIM = 64
GLA_VAL_DIM = 64
GLA_QK_W = GLA_HEADS * GLA_KEY_DIM
GLA_V_W = GLA_HEADS * GLA_VAL_DIM
GLA_GATE_RANK = 16
GLA_TAU = 16.0
GLA_CHUNK = 64

MLA_HEADS = 4
MLA_Q_RANK = 192
MLA_KV_RANK = 128
MLA_NOPE = 64
MLA_ROPE = 32
MLA_V = 64
MLA_QK = MLA_NOPE + MLA_ROPE
Q_BLOCK = 128
ROPE_BASE = 10000.0

MIX_WIDTH = SSD_D_INNER + GLA_V_W + MLA_HEADS * MLA_V

N_EXPERTS = 32
TOP_K = 4
EXPERT_FF = 1024
SWIGLU_ALPHA = 1.702
SWIGLU_LIMIT = 7.0

IN_SIZES = (SSD_D_INNER, SSD_XBC, 2 * SSD_HEADS, GLA_QK_W, GLA_QK_W, GLA_V_W, GLA_V_W,
            2 * GLA_GATE_RANK, MLA_Q_RANK, MLA_KV_RANK, MLA_ROPE)
IN_WIDTH = sum(IN_SIZES)

kernel_name = "hybrid_ssd_gla_mla_moe_dit_block"


def _rms(t):
    t = t.astype(F32)
    return t * lax.rsqrt(jnp.mean(t * t, axis=-1, keepdims=True) + NORM_EPS)


def rms_norm(t, w):
    return (_rms(t) * w.astype(F32)).astype(t.dtype)


def modulate(t, shift, scale):
    return t * (1.0 + scale) + shift


def split_cols(t, sizes):
    offsets, acc = [], 0
    for s in sizes[:-1]:
        acc += s
        offsets.append(acc)
    return jnp.split(t, offsets, axis=-1)


def flip(t):
    return jnp.flip(t, axis=1)


def depthwise_conv(t, w, b):
    pad = w.shape[0] // 2
    y = lax.conv_general_dilated(t, w[:, None, :].astype(t.dtype), (1,), [(pad, pad)],
                                 dimension_numbers=('NWC', 'WIO', 'NWC'),
                                 feature_group_count=t.shape[-1])
    return y + b.astype(t.dtype)


def axial_rope_tables(length):
    rows = length // GRID_W
    row = jnp.repeat(jnp.arange(rows, dtype=F32), GRID_W)
    col = jnp.tile(jnp.arange(GRID_W, dtype=F32), rows)
    half = MLA_ROPE // 2
    inv_freq = 1.0 / (ROPE_BASE ** (jnp.arange(0, half, 2, dtype=F32) / half))
    ang = jnp.stack([row[:, None] * inv_freq, col[:, None] * inv_freq], axis=1)
    return jnp.cos(ang), jnp.sin(ang)


def apply_axial_rope(t, cos, sin):
    shp = t.shape
    t32 = t.astype(F32).reshape(shp[:-1] + (2, 2, MLA_ROPE // 4))
    t1, t2 = t32[..., 0, :], t32[..., 1, :]
    cs, sn = cos[None, :, None], sin[None, :, None]
    out = jnp.stack([t1 * cs - t2 * sn, t2 * cs + t1 * sn], axis=-2)
    return out.reshape(shp).astype(t.dtype)


def chunk_recurrence(decay, inc, h0):
    def step(h, di):
        d, i = di
        return d * h + i, h
    h_final, h_enter = lax.scan(step, h0, (jnp.moveaxis(decay, 1, 0), jnp.moveaxis(inc, 1, 0)))
    return h_final, jnp.moveaxis(h_enter, 0, 1)


def ssd_chunked(xs, dt, a_coef, bm, cm, h0, with_output):
    b, L, H, P = xs.shape
    N = bm.shape[-1]
    Q = SSD_CHUNK
    nc = L // Q
    xc = xs.reshape(b, nc, Q, H, P)
    dtc = dt.reshape(b, nc, Q, H)
    bc = bm.reshape(b, nc, Q, H, N)
    cc = cm.reshape(b, nc, Q, H, N)
    a_cum = jnp.cumsum(dtc * a_coef, axis=2)
    a_last = a_cum[:, :, -1]
    states = jnp.einsum('bcshn,bcsh,bcshp->bchpn', bc, jnp.exp(a_last[:, :, None] - a_cum) * dtc, xc)
    h_final, h_enter = chunk_recurrence(jnp.exp(a_last)[..., None, None], states, h0)
    if not with_output:
        return None, h_final
    tril = jnp.tril(jnp.ones((Q, Q), bool))
    seg = a_cum[:, :, :, None, :] - a_cum[:, :, None, :, :]
    decay = jnp.exp(jnp.where(tril[:, :, None], seg, -jnp.inf))
    scores = jnp.einsum('bclhn,bcshn->bclsh', cc, bc) * decay * dtc[:, :, None]
    y = (jnp.einsum('bclsh,bcshp->bclhp', scores, xc)
         + jnp.einsum('bclhn,bchpn->bclhp', cc, h_enter) * jnp.exp(a_cum)[..., None])
    return y.reshape(b, L, H, P), h_final


def gla_chunked(q, k, v, g, s0, with_output):
    b, L, H, K = q.shape
    V = v.shape[-1]
    Q = GLA_CHUNK
    nc = L // Q
    qc = q.reshape(b, nc, Q, H, K)
    kc = k.reshape(b, nc, Q, H, K)
    vc = v.reshape(b, nc, Q, H, V)
    g_cum = jnp.cumsum(g.reshape(b, nc, Q, H, K), axis=2)
    g_last = g_cum[:, :, -1]
    inc = jnp.einsum('bcshk,bcshv->bchkv', kc * jnp.exp(g_last[:, :, None] - g_cum), vc)
    s_final, s_enter = chunk_recurrence(jnp.exp(g_last)[..., None], inc, s0)
    if not with_output:
        return None, s_final
    q_dec = qc * jnp.exp(g_cum)
    k_inv = kc * jnp.exp(-g_cum)
    tril = jnp.tril(jnp.ones((Q, Q), bool))
    att = jnp.where(tril[:, :, None], jnp.einsum('bclhk,bcshk->bclsh', q_dec, k_inv), 0.0)
    o = (jnp.einsum('bclsh,bcshv->bclhv', att, vc)
         + jnp.einsum('bclhk,bchkv->bclhv', q_dec, s_enter))
    return o.reshape(b, L, H, V), s_final


def ssd_mixer(cols_ctx, cols_lat, conv_w, conv_b, a_log, dt_bias, d_skip, norm_w, need_ctx_out):
    a_coef = -jnp.exp(a_log.astype(F32))
    dt_bias = dt_bias.astype(F32)
    d_skip = d_skip.astype(F32)[:, None]
    rep = SSD_HEADS // SSD_GROUPS

    def prep(z, xbc, dt_raw):
        b, L, _ = xbc.shape
        xbc = jax.nn.silu(depthwise_conv(xbc, conv_w, conv_b)).astype(F32)
        xs, bm, cm = jnp.split(xbc, [SSD_D_INNER, SSD_D_INNER + SSD_GROUPS * SSD_STATE], axis=-1)
        bm = jnp.repeat(bm.reshape(b, L, SSD_GROUPS, SSD_STATE), rep, axis=2)
        cm = jnp.repeat(cm.reshape(b, L, SSD_GROUPS, SSD_STATE), rep, axis=2)
        dt = jax.nn.softplus(dt_raw.astype(F32).reshape(b, L, 2, SSD_HEADS) + dt_bias)
        return z, xs.reshape(b, L, SSD_HEADS, SSD_HEAD_DIM), bm, cm, dt

    def bidir(xs, bm, cm, dt, h_fwd, h_bwd, with_output):
        y_f, hf = ssd_chunked(xs, dt[:, :, 0], a_coef[0], bm, cm, h_fwd, with_output)
        y_b, hb = ssd_chunked(flip(xs), flip(dt[:, :, 1]), a_coef[1], flip(bm), flip(cm), h_bwd, with_output)
        y = y_f + flip(y_b) + d_skip * xs if with_output else None
        return y, hf, hb

    def finish(y, z):
        b, L = y.shape[:2]
        gtd = y.reshape(b, L, SSD_D_INNER) * jax.nn.silu(z.astype(F32))
        gtd = _rms(gtd.reshape(b, L, SSD_GROUPS, SSD_D_INNER // SSD_GROUPS)).reshape(b, L, SSD_D_INNER)
        return gtd * norm_w.astype(F32)

    z_c, x_c, b_c, c_c, dt_c = prep(*cols_ctx)
    z_l, x_l, b_l, c_l, dt_l = prep(*cols_lat)
    h0 = jnp.zeros((x_c.shape[0], SSD_HEADS, SSD_HEAD_DIM, SSD_STATE), F32)
    y_c, h_f, h_b = bidir(x_c, b_c, c_c, dt_c, h0, h0, need_ctx_out)
    y_l, _, _ = bidir(x_l, b_l, c_l, dt_l, h_f, h_b, True)
    out_c = finish(y_c, z_c) if need_ctx_out else None
    return out_c, finish(y_l, z_l)


def gla_mixer(cols_ctx, cols_lat, g_up_w, g_up_b, norm_w, need_ctx_out):
    g_up_w = g_up_w.astype(F32)
    g_up_b = g_up_b.astype(F32)

    def prep(q, k, v, r, g_lr):
        b, L, _ = q.shape
        heads = lambda t, d: t.astype(F32).reshape(b, L, GLA_HEADS, d)
        g = jnp.einsum('blxr,xrk->blxk', g_lr.astype(F32).reshape(b, L, 2, GLA_GATE_RANK), g_up_w) + g_up_b
        g = (jax.nn.log_sigmoid(g) / GLA_TAU).reshape(b, L, 2, GLA_HEADS, GLA_KEY_DIM)
        return (heads(q, GLA_KEY_DIM) * (GLA_KEY_DIM ** -0.5), heads(k, GLA_KEY_DIM),
                heads(v, GLA_VAL_DIM), r, g)

    def bidir(q, k, v, g, s_f, s_b, with_output):
        o_f, sf = gla_chunked(q, k, v, g[:, :, 0], s_f, with_output)
        o_b, sb = gla_chunked(flip(q), flip(k), flip(v), flip(g[:, :, 1]), s_b, with_output)
        o = o_f + flip(o_b) if with_output else None
        return o, sf, sb

    def finish(o, r):
        b, L = o.shape[:2]
        return (_rms(o) * norm_w.astype(F32)).reshape(b, L, GLA_V_W) * jax.nn.silu(r.astype(F32))

    q_c, k_c, v_c, r_c, g_c = prep(*cols_ctx)
    q_l, k_l, v_l, r_l, g_l = prep(*cols_lat)
    s0 = jnp.zeros((q_c.shape[0], GLA_HEADS, GLA_KEY_DIM, GLA_VAL_DIM), F32)
    o_c, s_f, s_b = bidir(q_c, k_c, v_c, g_c, s0, s0, need_ctx_out)
    o_l, _, _ = bidir(q_l, k_l, v_l, g_l, s_f, s_b, True)
    out_c = finish(o_c, r_c) if need_ctx_out else None
    return out_c, finish(o_l, r_l)


def attend(q, k, v):
    s = jnp.einsum('bqhd,bkhd->bhqk', q, k, preferred_element_type=F32) * (MLA_QK ** -0.5)
    p = jax.nn.softmax(s, axis=-1)
    return jnp.einsum('bhqk,bkhv->bqhv', p.astype(v.dtype), v)


def blocked_attention(q, k, v):
    b, L, H, dq = q.shape
    nb = L // Q_BLOCK
    qb = q.reshape(b, nb, Q_BLOCK, H, dq).swapaxes(0, 1)
    ob = lax.map(lambda qi: attend(qi, k, v), qb)
    return ob.swapaxes(0, 1).reshape(b, L, H, v.shape[-1])


def mla_mixer(cols_ctx, cols_lat, q_norm_w, w_uq, kv_norm_w, w_ukv, cos, sin, need_ctx_out):
    def queries(cq):
        b, L, _ = cq.shape
        return (rms_norm(cq, q_norm_w) @ w_uq).reshape(b, L, MLA_HEADS, MLA_QK)

    def keys_values(ckv, kpe):
        b, L, _ = ckv.shape
        kv = (rms_norm(ckv, kv_norm_w) @ w_ukv).reshape(b, L, MLA_HEADS, MLA_NOPE + MLA_V)
        k_nope, v = jnp.split(kv, [MLA_NOPE], axis=-1)
        k = jnp.concatenate([k_nope, jnp.broadcast_to(kpe[:, :, None, :], (b, L, MLA_HEADS, MLA_ROPE))], axis=-1)
        return k, v

    cq_c, ckv_c, kpe_c = cols_ctx
    cq_l, ckv_l, kpe_l = cols_lat
    k_c, v_c = keys_values(ckv_c, kpe_c)
    k_l, v_l = keys_values(ckv_l, apply_axial_rope(kpe_l[:, :, None, :], cos, sin)[:, :, 0])
    q_l = queries(cq_l)
    q_l = jnp.concatenate([q_l[..., :MLA_NOPE], apply_axial_rope(q_l[..., MLA_NOPE:], cos, sin)], axis=-1)
    k_all = jnp.concatenate([k_c, k_l], axis=1)
    v_all = jnp.concatenate([v_c, v_l], axis=1)
    b, L = q_l.shape[:2]
    out_l = blocked_attention(q_l, k_all, v_all).reshape(b, L, MLA_HEADS * MLA_V).astype(F32)
    out_c = None
    if need_ctx_out:
        lc = cq_c.shape[1]
        out_c = attend(queries(cq_c), k_c, v_c).reshape(b, lc, MLA_HEADS * MLA_V).astype(F32)
    return out_c, out_l


def moe_ffn(t, router_w, router_b, w1, b1, w2, b2):
    logits = jnp.dot(t, router_w, preferred_element_type=F32) + router_b.astype(F32)
    top_val, top_idx = lax.top_k(logits, TOP_K)
    top_w = jax.nn.softmax(top_val, axis=-1)
    gate = jnp.einsum('tk,tke->te', top_w, jax.nn.one_hot(top_idx, N_EXPERTS, dtype=F32))
    out = jnp.zeros(t.shape, F32)
    for e in range(N_EXPERTS):
        hdn = t @ w1[e] + b1[e]
        glu = jnp.minimum(hdn[:, 0::2], SWIGLU_LIMIT).astype(F32)
        lin = jnp.clip(hdn[:, 1::2], -SWIGLU_LIMIT, SWIGLU_LIMIT).astype(F32)
        act = (glu * jax.nn.sigmoid(SWIGLU_ALPHA * glu) * (lin + 1.0)).astype(t.dtype)
        out = out + gate[:, e:e + 1] * (act @ w2[e] + b2[e]).astype(F32)
    return out.astype(t.dtype)


def setup_inputs(seed: int = 0) -> dict:
    key = jax.random.key(seed)
    ks = jax.random.split(key, 32)
    nrm = lambda i, shape, std: jax.random.normal(ks[i], shape, F32) * std
    gain = lambda i, shape: 1.0 + 0.05 * jax.random.normal(ks[i], shape, F32)
    dt0 = jnp.exp(jax.random.uniform(ks[9], (DEPTH, 2, SSD_HEADS), F32, np.log(1e-3), np.log(1e-1)))
    return {
        "x": nrm(0, (BATCH, SEQ, D_MODEL), 1.0),
        "c": nrm(1, (BATCH, D_MODEL), 1.0),
        "ctx": nrm(2, (BATCH, CTX_LEN, D_MODEL), 1.0),
        "c_ctx": nrm(3, (D_MODEL,), 1.0),
        "w_mod": nrm(4, (DEPTH, D_MODEL, N_MOD * D_MODEL), 0.5 * D_MODEL ** -0.5),
        "b_mod": nrm(5, (DEPTH, N_MOD * D_MODEL), 0.02),
        "norm1_w": gain(6, (DEPTH, D_MODEL)),
        "w_in": nrm(7, (DEPTH, D_MODEL, IN_WIDTH), D_MODEL ** -0.5),
        "ssd_conv_w": nrm(8, (DEPTH, SSD_CONV_K, SSD_XBC), SSD_CONV_K ** -0.5),
        "ssd_conv_b": nrm(10, (DEPTH, SSD_XBC), 0.02),
        "ssd_a_log": jnp.log(jax.random.uniform(ks[11], (DEPTH, 2, SSD_HEADS), F32, 1.0, 16.0)),
        "ssd_dt_bias": dt0 + jnp.log(-jnp.expm1(-dt0)),
        "ssd_d": gain(12, (DEPTH, SSD_HEADS)),
        "ssd_norm_w": gain(13, (DEPTH, SSD_D_INNER)),
        "gla_g_up_w": nrm(14, (DEPTH, 2, GLA_GATE_RANK, GLA_QK_W), GLA_GATE_RANK ** -0.5),
        "gla_g_up_b": nrm(15, (DEPTH, 2, GLA_QK_W), 0.1),
        "gla_norm_w": gain(16, (DEPTH, GLA_VAL_DIM)),
        "mla_q_norm_w": gain(17, (DEPTH, MLA_Q_RANK)),
        "mla_w_uq": nrm(18, (DEPTH, MLA_Q_RANK, MLA_HEADS * MLA_QK), MLA_Q_RANK ** -0.5),
        "mla_kv_norm_w": gain(19, (DEPTH, MLA_KV_RANK)),
        "mla_w_ukv": nrm(20, (DEPTH, MLA_KV_RANK, MLA_HEADS * (MLA_NOPE + MLA_V)), MLA_KV_RANK ** -0.5),
        "w_out": nrm(21, (DEPTH, MIX_WIDTH, D_MODEL), MIX_WIDTH ** -0.5),
        "norm2_w": gain(22, (DEPTH, D_MODEL)),
        "router_w": nrm(23, (DEPTH, D_MODEL, N_EXPERTS), D_MODEL ** -0.5),
        "router_b": nrm(24, (DEPTH, N_EXPERTS), 0.01),
        "expert_w1": nrm(25, (DEPTH, N_EXPERTS, D_MODEL, 2 * EXPERT_FF), D_MODEL ** -0.5),
        "expert_b1": nrm(26, (DEPTH, N_EXPERTS, 2 * EXPERT_FF), 0.01),
        "expert_w2": nrm(27, (DEPTH, N_EXPERTS, EXPERT_FF, D_MODEL), EXPERT_FF ** -0.5),
        "expert_b2": nrm(28, (DEPTH, N_EXPERTS, D_MODEL), 0.01),
        "final_norm_w": gain(29, (D_MODEL,)),
    }


def reference(x, c, ctx, c_ctx, w_mod, b_mod, norm1_w, w_in, ssd_conv_w, ssd_conv_b, ssd_a_log,
              ssd_dt_bias, ssd_d, ssd_norm_w, gla_g_up_w, gla_g_up_b, gla_norm_w, mla_q_norm_w,
              mla_w_uq, mla_kv_norm_w, mla_w_ukv, w_out, norm2_w, router_w, router_b, expert_w1,
              expert_b1, expert_w2, expert_b2, final_norm_w):
    seq_len = x.shape[1]
    cos, sin = axial_rope_tables(seq_len)
    silu_c = jax.nn.silu(c)
    silu_cc = jax.nn.silu(c_ctx)
    h_lat, h_ctx = x, ctx
    for l in range(DEPTH):
        need_ctx_out = l < DEPTH - 1
        mod_l = (silu_c @ w_mod[l] + b_mod[l]).reshape(-1, N_MOD, 1, D_MODEL)
        mod_c = (silu_cc @ w_mod[l] + b_mod[l]).reshape(N_MOD, D_MODEL)
        act_dtype = h_lat.dtype

        u_l = modulate(rms_norm(h_lat, norm1_w[l]), mod_l[:, 0], mod_l[:, 1])
        u_c = modulate(rms_norm(h_ctx, norm1_w[l]), mod_c[0], mod_c[1])
        cols_l = split_cols(u_l @ w_in[l], IN_SIZES)
        cols_c = split_cols(u_c @ w_in[l], IN_SIZES)
        ssd_c, ssd_l = ssd_mixer(cols_c[0:3], cols_l[0:3], ssd_conv_w[l], ssd_conv_b[l], ssd_a_log[l],
                                 ssd_dt_bias[l], ssd_d[l], ssd_norm_w[l], need_ctx_out)
        gla_c, gla_l = gla_mixer(cols_c[3:8], cols_l[3:8], gla_g_up_w[l], gla_g_up_b[l], gla_norm_w[l],
                                 need_ctx_out)
        mla_c, mla_l = mla_mixer(cols_c[8:11], cols_l[8:11], mla_q_norm_w[l], mla_w_uq[l], mla_kv_norm_w[l],
                                 mla_w_ukv[l], cos, sin, need_ctx_out)
        mix_l = jnp.concatenate([ssd_l, gla_l, mla_l], axis=-1).astype(act_dtype) @ w_out[l]
        h_lat = h_lat + mod_l[:, 2] * mix_l
        if need_ctx_out:
            mix_c = jnp.concatenate([ssd_c, gla_c, mla_c], axis=-1).astype(act_dtype) @ w_out[l]
            h_ctx = h_ctx + mod_c[2] * mix_c

        v_l = modulate(rms_norm(h_lat, norm2_w[l]), mod_l[:, 3], mod_l[:, 4])
        b, L, d = v_l.shape
        if need_ctx_out:
            v_c = modulate(rms_norm(h_ctx, norm2_w[l]), mod_c[3], mod_c[4])
            lc = v_c.shape[1]
            f = moe_ffn(jnp.concatenate([v_c.reshape(-1, d), v_l.reshape(-1, d)], axis=0), router_w[l],
                        router_b[l], expert_w1[l], expert_b1[l], expert_w2[l], expert_b2[l])
            h_ctx = h_ctx + mod_c[5] * f[:b * lc].reshape(b, lc, d)
            h_lat = h_lat + mod_l[:, 5] * f[b * lc:].reshape(b, L, d)
        else:
            f = moe_ffn(v_l.reshape(-1, d), router_w[l], router_b[l], expert_w1[l], expert_b1[l],
                        expert_w2[l], expert_b2[l])
            h_lat = h_lat + mod_l[:, 5] * f.reshape(b, L, d)
    return rms_norm(h_lat, final_norm_w)
```

```python
import functools

import jax
import jax.numpy as jnp
from jax import lax
from jax.experimental import pallas as pl
from jax.experimental.pallas import tpu as pltpu

F32 = jnp.float32
BF16 = jnp.bfloat16
I32 = jnp.int32

D_MODEL = 1024
N_MOD = 6
NORM_EPS = 1e-6
SSD_HEADS, SSD_GROUPS, SSD_STATE = 8, 2, 64
SSD_D_INNER, SSD_XBC = 512, 768
GLA_HEADS, GLA_W, GLA_GATE_RANK, GLA_TAU = 4, 256, 16, 16.0
MLA_HEADS, MLA_Q_RANK, MLA_KV_RANK = 4, 192, 128
MLA_NOPE, MLA_ROPE, MLA_V, MLA_QK = 64, 32, 64, 96
GRID_W, ROPE_BASE = 64, 10000.0
N_EXPERTS, TOP_K, EXPERT_FF = 32, 4, 1024
SWIGLU_ALPHA, SWIGLU_LIMIT = 1.702, 7.0

LANES = 128
SUBLANES = 8
ROW_TILE = 256
CHUNK = 128
MOE_SUPER = 4096
MOE_ROWS = 128
MOE_PLANE = MOE_ROWS + SUBLANES
VMEM_LIMIT = 56 * 1024 * 1024
NEG = -1e30

IN_Z, IN_XBC, IN_DT, IN_GLA, IN_GLR, IN_MLA = 0, 512, 1280, 1408, 2432, 2560
IN_PAD_WIDTH = 3200
MLA_IN_W = 640


def _dot(a, b):
    return jnp.dot(a, b, preferred_element_type=F32)


def _dot_nt(a, b):
    return lax.dot_general(a, b, (((1,), (1,)), ((), ())), preferred_element_type=F32)


def _dot_tn(a, b):
    return lax.dot_general(a, b, (((0,), (0,)), ((), ())), preferred_element_type=F32)


def _split(x):
    hi = x.astype(BF16)
    lo = (x - hi.astype(F32)).astype(BF16)
    return hi, lo


def _dot_sel_rhs(x, m):
    hi, lo = _split(x)
    return _dot(hi, m) + _dot(lo, m)


def _dot_sel_lhs(m, x):
    hi, lo = _split(x)
    return _dot(m, hi) + _dot(m, lo)


def _dot3(a, b):
    ah, al = _split(a)
    bh, bl = _split(b)
    return _dot(ah, bh) + _dot(ah, bl) + _dot(al, bh)


def _softplus(x):
    return jnp.maximum(x, 0.0) + jnp.log(1.0 + jnp.exp(-jnp.abs(x)))


def _silu(x):
    return x * jax.nn.sigmoid(x)


def _rms(x, n=None):
    ms = jnp.sum(x * x, axis=-1, keepdims=True) * (1.0 / (n or x.shape[-1]))
    return x * lax.rsqrt(ms + NORM_EPS)


def _params(*sem):
    return pltpu.CompilerParams(dimension_semantics=sem, vmem_limit_bytes=VMEM_LIMIT)


def _mod_kernel(c_ref, w_ref, b_ref, o_ref):
    cv = c_ref[...]
    o_ref[0] = _dot3(_silu(cv), w_ref[0]) + b_ref[0]


def _modulation(cvec, w_mod, b_mod):
    depth, _, width = w_mod.shape
    rows = cvec.shape[0]
    tn = 1536
    return pl.pallas_call(
        _mod_kernel,
        grid=(depth, width // tn),
        in_specs=[pl.BlockSpec((rows, D_MODEL), lambda l, n: (0, 0)),
                  pl.BlockSpec((1, D_MODEL, tn), lambda l, n: (l, 0, n)),
                  pl.BlockSpec((1, 1, tn), lambda l, n: (l, 0, n))],
        out_specs=pl.BlockSpec((1, rows, tn), lambda l, n: (l, 0, n)),
        out_shape=jax.ShapeDtypeStruct((depth, rows, width), F32),
        compiler_params=_params("parallel", "parallel"),
        name="modulation",
    )(cvec, w_mod, b_mod.reshape(depth, 1, width))


def _from_token_vreg(f8_ref, rows):
    return jnp.concatenate([f8_ref[pl.ds(j, rows, stride=SUBLANES), :] for j in range(SUBLANES)], axis=1)


def _inproj_kernel(has_moe, *refs):
    if has_moe:
        h_ref, f8_ref, pmod_ref, mod_ref, nw_ref, w_ref = refs[:6]
        outs = refs[6:]
    else:
        h_ref, mod_ref, nw_ref, w_ref = refs[:4]
        outs = refs[4:]
    x = h_ref[0]
    if has_moe:
        x = x + pmod_ref[0, 0, 5:6, :] * _from_token_vreg(f8_ref, ROW_TILE)
        outs[6][0] = x
    mod = mod_ref[0, 0]
    u = (_rms(x) * nw_ref[...]) * (1.0 + mod[1:2, :]) + mod[0:1, :]
    ub = u.astype(BF16)
    z_ref, xbc_ref, dt_ref, gla_ref, glr_ref, mla_ref = outs[:6]
    z_ref[0] = _dot(ub, w_ref[:, IN_Z:IN_XBC]).astype(BF16)
    xbc_ref[0] = _dot(ub, w_ref[:, IN_XBC:IN_DT]).astype(BF16)
    dt_ref[0] = _dot(ub, w_ref[:, IN_DT:IN_GLA])
    gla_ref[0] = _dot(ub, w_ref[:, IN_GLA:IN_GLR]).astype(BF16)
    glr_ref[0] = _dot(ub, w_ref[:, IN_GLR:IN_MLA])
    mla_ref[0] = _dot(ub, w_ref[:, IN_MLA:IN_PAD_WIDTH]).astype(BF16)


def _in_projection(h, f8, prev_mod, mod, norm_w, w_pad, n_ctx_tiles):
    b, lt, _ = h.shape
    nt = lt // ROW_TILE
    has_moe = f8 is not None
    row = lambda w: pl.BlockSpec((1, ROW_TILE, w), lambda i, t: (i, t, 0))
    mod_spec = pl.BlockSpec((1, 1, N_MOD, D_MODEL), lambda i, t: (i, jnp.where(t >= n_ctx_tiles, 1, 0), 0, 0))
    in_specs = [row(D_MODEL)]
    args = [h]
    if has_moe:
        in_specs += [pl.BlockSpec((ROW_TILE * SUBLANES, LANES), lambda i, t: (i * nt + t, 0)), mod_spec]
        args += [f8, prev_mod]
    in_specs += [mod_spec, pl.BlockSpec((1, D_MODEL), lambda i, t: (0, 0)),
                 pl.BlockSpec((D_MODEL, IN_PAD_WIDTH), lambda i, t: (0, 0))]
    args += [mod, norm_w.reshape(1, D_MODEL), w_pad]
    widths = [(SSD_D_INNER, BF16), (SSD_XBC, BF16), (LANES, F32), (4 * GLA_W, BF16), (LANES, F32), (MLA_IN_W, BF16)]
    if has_moe:
        widths.append((D_MODEL, F32))
    return pl.pallas_call(
        functools.partial(_inproj_kernel, has_moe),
        grid=(b, nt),
        in_specs=in_specs,
        out_specs=[row(w) for w, _ in widths],
        out_shape=[jax.ShapeDtypeStruct((b, lt, w), dt) for w, dt in widths],
        compiler_params=_params("parallel", "parallel"),
        name="in_projection",
    )(*args)


def _conv_kernel(n_ctx_tiles, prev_ref, cur_ref, next_ref, w_ref, b_ref, o_ref):
    t = pl.program_id(1)
    nt = pl.num_programs(1)
    cur = cur_ref[0].astype(F32)
    has_prev = jnp.logical_and(t != 0, t != n_ctx_tiles)
    has_next = jnp.logical_and(t != n_ctx_tiles - 1, t != nt - 1)
    prev_row = jnp.where(has_prev, prev_ref[0][SUBLANES - 1:SUBLANES, :].astype(F32), 0.0)
    next_row = jnp.where(has_next, next_ref[0][0:1, :].astype(F32), 0.0)
    row = lax.broadcasted_iota(I32, cur.shape, 0)
    before = jnp.where(row == 0, prev_row, pltpu.roll(cur, 1, 0))
    after = jnp.where(row == ROW_TILE - 1, next_row, pltpu.roll(cur, ROW_TILE - 1, 0))
    y = w_ref[0:1, :] * before + w_ref[1:2, :] * cur + w_ref[2:3, :] * after + b_ref[...]
    o_ref[0] = _silu(y).astype(BF16)


def _ssd_conv(xbc, conv_w, conv_b, n_ctx_tiles):
    b, lt, w = xbc.shape
    nt = lt // ROW_TILE
    per = ROW_TILE // SUBLANES
    last = lt // SUBLANES - 1
    return pl.pallas_call(
        functools.partial(_conv_kernel, n_ctx_tiles),
        grid=(b, nt),
        in_specs=[pl.BlockSpec((1, SUBLANES, w), lambda i, t: (i, jnp.maximum(t * per - 1, 0), 0)),
                  pl.BlockSpec((1, ROW_TILE, w), lambda i, t: (i, t, 0)),
                  pl.BlockSpec((1, SUBLANES, w), lambda i, t: (i, jnp.minimum((t + 1) * per, last), 0)),
                  pl.BlockSpec((3, w), lambda i, t: (0, 0)),
                  pl.BlockSpec((1, w), lambda i, t: (0, 0))],
        out_specs=pl.BlockSpec((1, ROW_TILE, w), lambda i, t: (i, t, 0)),
        out_shape=jax.ShapeDtypeStruct((b, lt, w), BF16),
        compiler_params=_params("parallel", "parallel"),
        name="ssd_conv",
    )(xbc, xbc, xbc, conv_w, conv_b.reshape(1, w))


def _scan_chunk(reverse, n_ctx_chunks, n_chunks, j):
    if not reverse:
        return j
    return jnp.where(j < n_ctx_chunks, n_ctx_chunks - 1 - j, n_chunks - 1 - (j - n_ctx_chunks))


def _ssd_kernel(reverse, *refs):
    if reverse:
        (xbc_ref, dtc_ref, dtr_ref, acr_ref, acc_ref, dbr_ref, dbc_ref,
         yf_ref, z_ref, dsk_ref, nw_ref, out_ref, st_ref) = refs
    else:
        xbc_ref, dtc_ref, dtr_ref, acr_ref, acc_ref, dbr_ref, dbc_ref, out_ref, st_ref = refs
    q = CHUNK
    d = 1 if reverse else 0

    @pl.when(pl.program_id(1) == 0)
    def _():
        st_ref[...] = jnp.zeros_like(st_ref)

    xbc = xbc_ref[0]
    xs = xbc[:, :SSD_D_INNER]
    bm = xbc[:, SSD_D_INNER:SSD_D_INNER + LANES]
    cm = xbc[:, SSD_D_INNER + LANES:]
    dt_c = _softplus(dtc_ref[0][:, SSD_HEADS * d:SSD_HEADS * (d + 1)] + dbr_ref[0])
    dt_r = _softplus(dtr_ref[0, 0] + dbc_ref[0])
    a_c = dt_c * acr_ref[0]
    a_r = dt_r * acc_ref[0]
    row = lax.broadcasted_iota(I32, (q, q), 0)
    col = lax.broadcasted_iota(I32, (q, q), 1)
    if not reverse:
        lmat = jnp.where(row >= col, 1.0, 0.0).astype(BF16)
        umat = jnp.where(row <= col, 1.0, 0.0).astype(BF16)
        valid = row >= col
    else:
        lmat = jnp.where(row > col, 1.0, 0.0).astype(BF16)
        umat = jnp.where(row < col, 1.0, 0.0).astype(BF16)
        valid = col >= row
    p_c = _dot_sel_lhs(lmat, a_c)
    p_r = _dot_sel_rhs(a_r, umat)
    if not reverse:
        tot = p_c[q - 1:q, :]
        inter_c = jnp.exp(p_c)
        w_c = jnp.exp(tot - p_c) * dt_c
    else:
        tot = p_c[q - 1:q, :] + a_c[q - 1:q, :]
        inter_c = jnp.exp(tot - p_c)
        w_c = jnp.exp(p_c) * dt_c
    hrow = lax.broadcasted_iota(I32, (SSD_HEADS, SSD_D_INNER), 0)
    hlane = lax.broadcasted_iota(I32, (SSD_HEADS, SSD_D_INNER), 1)
    expand = jnp.where(jnp.right_shift(hlane, 6) == hrow, 1.0, 0.0).astype(BF16)
    inter_f = _dot_sel_rhs(inter_c, expand)
    w_f = _dot_sel_rhs(w_c, expand)
    dec_f = _dot_sel_rhs(jnp.broadcast_to(jnp.exp(tot), (SUBLANES, SSD_HEADS)), expand)[0:1]
    xs_f = xs.astype(F32)
    xw = (xs_f * w_f).astype(BF16)
    half = SSD_D_INNER // SSD_GROUPS
    lane_h = jnp.right_shift(lax.broadcasted_iota(I32, (q, half), 1), 6)
    lane_g = jnp.right_shift(lax.broadcasted_iota(I32, (q, LANES), 1), 6)
    for g in range(SSD_GROUPS):
        sl = slice(g * half, (g + 1) * half)
        cg = jnp.where(lane_g == g, cm, jnp.zeros_like(cm))
        bg = jnp.where(lane_g == g, bm, jnp.zeros_like(bm))
        cb = _dot_nt(cg, bm)
        state = st_ref[g]
        y = _dot(cg, state.astype(BF16)) * inter_f[:, sl]
        xs_g = xs[:, sl]
        for hh in range(SSD_HEADS // SSD_GROUPS):
            h = g * (SSD_HEADS // SSD_GROUPS) + hh
            if not reverse:
                seg = p_c[:, h:h + 1] - p_r[h:h + 1, :]
            else:
                seg = p_r[h:h + 1, :] - p_c[:, h:h + 1]
            decay = jnp.exp(jnp.where(valid, seg, NEG))
            scores = (cb * decay * dt_r[h:h + 1, :]).astype(BF16)
            y = y + _dot(scores, jnp.where(lane_h == hh, xs_g, jnp.zeros_like(xs_g)))
        st_ref[g] = state * dec_f[:, sl] + _dot_tn(bg, xw[:, sl])
        if reverse:
            ytot = y + yf_ref[0][:, sl].astype(F32) + dsk_ref[:, sl] * xs_f[:, sl]
            gated = ytot * _silu(z_ref[0][:, sl].astype(F32))
            out_ref[0, :, sl] = (_rms(gated) * nw_ref[:, sl]).astype(BF16)
        else:
            out_ref[0, :, sl] = y.astype(BF16)


def _ssd_scan(reverse, xbc_act, dtc, dtr, a_coef, dt_bias, n_ctx_chunks, extra=None):
    b, lt, _ = xbc_act.shape
    nc = lt // CHUNK
    d = 1 if reverse else 0
    cidx = functools.partial(_scan_chunk, reverse, n_ctx_chunks, nc)
    row = lambda w: pl.BlockSpec((1, CHUNK, w), lambda i, j: (i, cidx(j), 0))
    small_r = pl.BlockSpec((1, 1, SSD_HEADS), lambda i, j: (d, 0, 0))
    small_c = pl.BlockSpec((1, SSD_HEADS, 1), lambda i, j: (d, 0, 0))
    in_specs = [row(SSD_XBC), row(LANES),
                pl.BlockSpec((1, 1, SSD_HEADS, CHUNK), lambda i, j: (i, d, 0, cidx(j))),
                small_r, small_c, small_r, small_c]
    args = [xbc_act, dtc, dtr, a_coef.reshape(2, 1, SSD_HEADS), a_coef.reshape(2, SSD_HEADS, 1),
            dt_bias.reshape(2, 1, SSD_HEADS), dt_bias.reshape(2, SSD_HEADS, 1)]
    if reverse:
        yf, z, dsk, nw = extra
        vec = pl.BlockSpec((1, SSD_D_INNER), lambda i, j: (0, 0))
        in_specs += [row(SSD_D_INNER), row(SSD_D_INNER), vec, vec]
        args += [yf, z, dsk, nw]
    return pl.pallas_call(
        functools.partial(_ssd_kernel, reverse),
        grid=(b, nc),
        in_specs=in_specs,
        out_specs=row(SSD_D_INNER),
        out_shape=jax.ShapeDtypeStruct((b, lt, SSD_D_INNER), BF16),
        scratch_shapes=[pltpu.VMEM((SSD_GROUPS, LANES, SSD_D_INNER // SSD_GROUPS), F32)],
        compiler_params=_params("parallel", "arbitrary"),
        name="ssd_scan_bwd" if reverse else "ssd_scan_fwd",
    )(*args)


def _gla_kernel(reverse, *refs):
    if reverse:
        gla_ref, glr_ref, wg_ref, bg_ref, of_ref, nw_ref, out_ref, st_ref = refs
    else:
        gla_ref, glr_ref, wg_ref, bg_ref, out_ref, st_ref = refs
    q = CHUNK
    w = GLA_W

    @pl.when(pl.program_id(1) == 0)
    def _():
        st_ref[...] = jnp.zeros_like(st_ref)

    x = gla_ref[0]
    qq = x[:, :w].astype(F32) * (64.0 ** -0.5)
    kk = x[:, w:2 * w].astype(F32)
    vv = x[:, 2 * w:3 * w]
    gpre = _dot3(glr_ref[0], wg_ref[0]) + bg_ref[0]
    g = (jnp.minimum(gpre, 0.0) - jnp.log(1.0 + jnp.exp(-jnp.abs(gpre)))) * (1.0 / GLA_TAU)
    row = lax.broadcasted_iota(I32, (q, q), 0)
    col = lax.broadcasted_iota(I32, (q, q), 1)
    if not reverse:
        gc = _dot_sel_lhs(jnp.where(row >= col, 1.0, 0.0).astype(BF16), g)
        valid = row >= col
        tot = gc[q - 1:q, :]
        q_dec = qq * jnp.exp(gc)
        k_inv = kk * jnp.exp(-gc)
        k_st = kk * jnp.exp(tot - gc)
    else:
        ge = _dot_sel_lhs(jnp.where(row > col, 1.0, 0.0).astype(BF16), g)
        valid = col >= row
        tot = ge[q - 1:q, :] + g[q - 1:q, :]
        q_dec = qq * jnp.exp(tot - ge)
        k_inv = kk * jnp.exp(ge - tot)
        k_st = kk * jnp.exp(ge)
    lane_h = jnp.right_shift(lax.broadcasted_iota(I32, (q, w), 1), 6)
    state = st_ref[...]
    qd_b = q_dec.astype(BF16)
    ki_b = k_inv.astype(BF16)
    o = _dot_nt(qd_b, state.astype(BF16))
    for h in range(GLA_HEADS):
        att = _dot_nt(jnp.where(lane_h == h, qd_b, jnp.zeros_like(qd_b)), ki_b)
        att = jnp.where(valid, att, 0.0).astype(BF16)
        o = o + _dot(att, jnp.where(lane_h == h, vv, jnp.zeros_like(vv)))
    r2 = jnp.right_shift(lax.broadcasted_iota(I32, (w, w), 0), 6)
    c2 = jnp.right_shift(lax.broadcasted_iota(I32, (w, w), 1), 6)
    same_head = r2 == c2
    st_ref[...] = jnp.where(same_head, state * jnp.exp(tot) + _dot_tn(vv, k_st.astype(BF16)), 0.0)
    if reverse:
        ot = o + of_ref[0].astype(F32)
        ms = _dot_sel_rhs(ot * ot, jnp.where(same_head, 1.0, 0.0).astype(BF16)) * (1.0 / 64.0)
        rr = x[:, 3 * w:].astype(F32)
        out_ref[0] = (ot * lax.rsqrt(ms + NORM_EPS) * nw_ref[...] * _silu(rr)).astype(BF16)
    else:
        out_ref[0] = o.astype(BF16)


def _gla_scan(reverse, gla_in, glr, wg_pad, g_up_b, n_ctx_chunks, extra=None):
    b, lt, _ = gla_in.shape
    nc = lt // CHUNK
    d = 1 if reverse else 0
    cidx = functools.partial(_scan_chunk, reverse, n_ctx_chunks, nc)
    row = lambda w: pl.BlockSpec((1, CHUNK, w), lambda i, j: (i, cidx(j), 0))
    in_specs = [row(4 * GLA_W), row(LANES),
                pl.BlockSpec((1, LANES, GLA_W), lambda i, j: (d, 0, 0)),
                pl.BlockSpec((1, 1, GLA_W), lambda i, j: (d, 0, 0))]
    args = [gla_in, glr, wg_pad, g_up_b.reshape(2, 1, GLA_W)]
    if reverse:
        of, nw = extra
        in_specs += [row(GLA_W), pl.BlockSpec((1, GLA_W), lambda i, j: (0, 0))]
        args += [of, nw]
    return pl.pallas_call(
        functools.partial(_gla_kernel, reverse),
        grid=(b, nc),
        in_specs=in_specs,
        out_specs=row(GLA_W),
        out_shape=jax.ShapeDtypeStruct((b, lt, GLA_W), BF16),
        scratch_shapes=[pltpu.VMEM((GLA_W, GLA_W), F32)],
        compiler_params=_params("parallel", "arbitrary"),
        name="gla_scan_bwd" if reverse else "gla_scan_fwd",
    )(*args)


def _mla_prep_kernel(m_ref, qnw_ref, wq_ref, wqs_ref, kvnw_ref, wk_ref, wv_ref, cos_ref, sin_ref,
                     q_ref, k_ref, v_ref):
    m = m_ref[0]
    cq = m[:, :2 * LANES].astype(F32)
    ckv = m[:, 2 * LANES:3 * LANES].astype(F32)
    ka = m[:, 3 * LANES:4 * LANES].astype(F32)
    kb = m[:, 4 * LANES:].astype(F32)
    cos = cos_ref[...]
    sin = sin_ref[...]
    cqn = (_rms(cq, MLA_Q_RANK) * qnw_ref[...]).astype(BF16)
    qm = _dot(cqn, wq_ref[...])
    qs = _dot(cqn, wqs_ref[...])
    ckn = (_rms(ckv) * kvnw_ref[...]).astype(BF16)
    kn = _dot(ckn, wk_ref[...])
    k_rot = ka * cos + kb * sin
    v_ref[0] = _dot(ckn, wv_ref[...]).astype(BF16)
    for h in range(MLA_HEADS):
        sl = slice(h * LANES, (h + 1) * LANES)
        q_ref[0, h] = ((qm[:, sl] * cos + qs[:, sl] * sin) * (MLA_QK ** -0.5)).astype(BF16)
        k_ref[0, h] = (kn[:, sl] + k_rot).astype(BF16)


def _mla_prep(mla_in, qnw, wq, wqs, kvnw, wk, wv, cos_t, sin_t):
    b, lt, _ = mla_in.shape
    nt = lt // ROW_TILE
    full = lambda a: pl.BlockSpec(a.shape, lambda i, t: (0,) * a.ndim)
    tab = pl.BlockSpec((ROW_TILE, LANES), lambda i, t: (t, 0))
    head = pl.BlockSpec((1, MLA_HEADS, ROW_TILE, LANES), lambda i, t: (i, 0, t, 0))
    return pl.pallas_call(
        _mla_prep_kernel,
        grid=(b, nt),
        in_specs=[pl.BlockSpec((1, ROW_TILE, MLA_IN_W), lambda i, t: (i, t, 0)),
                  full(qnw), full(wq), full(wqs), full(kvnw), full(wk), full(wv), tab, tab],
        out_specs=[head, head, pl.BlockSpec((1, ROW_TILE, MLA_HEADS * MLA_V), lambda i, t: (i, t, 0))],
        out_shape=[jax.ShapeDtypeStruct((b, MLA_HEADS, lt, LANES), BF16),
                   jax.ShapeDtypeStruct((b, MLA_HEADS, lt, LANES), BF16),
                   jax.ShapeDtypeStruct((b, lt, MLA_HEADS * MLA_V), BF16)],
        compiler_params=_params("parallel", "parallel"),
        name="mla_prep",
    )(mla_in, qnw, wq, wqs, kvnw, wk, wv, cos_t, sin_t)


def _attn_kernel(n_ctx, q_ref, k_ref, v_ref, o_ref):
    def attend(n_keys):
        lane_h = jnp.right_shift(lax.broadcasted_iota(I32, (n_keys, MLA_HEADS * MLA_V), 1), 6)
        vv = v_ref[0, :n_keys, :]
        acc = jnp.zeros((ROW_TILE, MLA_HEADS * MLA_V), F32)
        for h in range(MLA_HEADS):
            s = _dot_nt(q_ref[0, h], k_ref[0, h, :n_keys, :])
            p = jnp.exp(s - jnp.max(s, axis=-1, keepdims=True))
            denom = jnp.sum(p, axis=-1, keepdims=True)
            pv = _dot(p.astype(BF16), jnp.where(lane_h == h, vv, jnp.zeros_like(vv)))
            acc = acc + pv * (1.0 / denom)
        o_ref[0] = acc.astype(BF16)

    is_ctx = pl.program_id(1) < n_ctx // ROW_TILE
    pl.when(is_ctx)(lambda: attend(n_ctx))
    pl.when(jnp.logical_not(is_ctx))(lambda: attend(k_ref.shape[2]))


def _attention(q, k, v, n_ctx):
    b, _, lt, _ = q.shape
    nt = lt // ROW_TILE
    return pl.pallas_call(
        functools.partial(_attn_kernel, n_ctx),
        grid=(b, nt),
        in_specs=[pl.BlockSpec((1, MLA_HEADS, ROW_TILE, LANES), lambda i, t: (i, 0, t, 0)),
                  pl.BlockSpec((1, MLA_HEADS, lt, LANES), lambda i, t: (i, 0, 0, 0)),
                  pl.BlockSpec((1, lt, MLA_HEADS * MLA_V), lambda i, t: (i, 0, 0))],
        out_specs=pl.BlockSpec((1, ROW_TILE, MLA_HEADS * MLA_V), lambda i, t: (i, t, 0)),
        out_shape=jax.ShapeDtypeStruct((b, lt, MLA_HEADS * MLA_V), BF16),
        compiler_params=_params("parallel", "arbitrary"),
        name="mla_attention",
    )(q, k, v)


def _outproj_kernel(ssd_ref, gla_ref, mla_ref, h_ref, mod_ref, w_ref, nw_ref, rw_ref, rb_ref,
                    h_out_ref, v8_ref, idx_ref, gate_ref):
    mod = mod_ref[0, 0]
    mix = (_dot(ssd_ref[0], w_ref[:SSD_D_INNER, :])
           + _dot(gla_ref[0], w_ref[SSD_D_INNER:SSD_D_INNER + GLA_W, :])
           + _dot(mla_ref[0], w_ref[SSD_D_INNER + GLA_W:, :]))
    hm = h_ref[0] + mod[2:3, :] * mix
    h_out_ref[0] = hm
    v = (_rms(hm) * nw_ref[...]) * (1.0 + mod[4:5, :]) + mod[3:4, :]
    for j in range(SUBLANES):
        v8_ref[pl.ds(j, ROW_TILE, stride=SUBLANES), :] = v[:, j * LANES:(j + 1) * LANES]
    logits = _dot(v.astype(BF16), rw_ref[...]) + rb_ref[...]
    lane = lax.broadcasted_iota(I32, logits.shape, 1)
    idxs, vals = [], []
    for _ in range(TOP_K):
        top = jnp.max(logits, axis=-1, keepdims=True)
        pick = jnp.min(jnp.where(logits == top, lane, LANES), axis=-1, keepdims=True)
        idxs.append(pick)
        vals.append(top)
        logits = jnp.where(lane == pick, 2.0 * NEG, logits)
    exps = [jnp.exp(t - vals[0]) for t in vals]
    inv = 1.0 / (exps[0] + exps[1] + exps[2] + exps[3])
    idx_out = jnp.zeros(lane.shape, I32)
    gate_out = jnp.zeros(lane.shape, F32)
    for k in range(TOP_K):
        idx_out = jnp.where(lane == k, idxs[k], idx_out)
        gate_out = jnp.where(lane == k, exps[k] * inv, gate_out)
    idx_ref[0] = idx_out
    gate_ref[0] = gate_out


def _out_projection(ssd_o, gla_o, mla_o, h, mod, w_out, norm_w, rw_pad, rb_pad, n_ctx_tiles):
    b, lt, _ = h.shape
    nt = lt // ROW_TILE
    row = lambda w: pl.BlockSpec((1, ROW_TILE, w), lambda i, t: (i, t, 0))
    full = lambda a: pl.BlockSpec(a.shape, lambda i, t: (0,) * a.ndim)
    mod_spec = pl.BlockSpec((1, 1, N_MOD, D_MODEL), lambda i, t: (i, jnp.where(t >= n_ctx_tiles, 1, 0), 0, 0))
    nw = norm_w.reshape(1, D_MODEL)
    return pl.pallas_call(
        _outproj_kernel,
        grid=(b, nt),
        in_specs=[row(SSD_D_INNER), row(GLA_W), row(MLA_HEADS * MLA_V), row(D_MODEL), mod_spec,
                  full(w_out), full(nw), full(rw_pad), full(rb_pad)],
        out_specs=[row(D_MODEL),
                   pl.BlockSpec((ROW_TILE * SUBLANES, LANES), lambda i, t: (i * nt + t, 0)),
                   row(LANES), row(LANES)],
        out_shape=[jax.ShapeDtypeStruct((b, lt, D_MODEL), F32),
                   jax.ShapeDtypeStruct((b * lt * SUBLANES, LANES), F32),
                   jax.ShapeDtypeStruct((b, lt, LANES), I32),
                   jax.ShapeDtypeStruct((b, lt, LANES), F32)],
        compiler_params=_params("parallel", "parallel"),
        name="out_projection",
    )(ssd_o, gla_o, mla_o, h, mod, w_out, nw, rw_pad, rb_pad)


def _moe_kernel(ts, off_ref, cnt_ref, tok_ref, gate_ref, x8_ref, w1g_ref, w1l_ref, w2_ref,
                b1g_ref, b1l_ref, b2_ref, f8_ref, xg_ref, yp_ref):
    s = pl.program_id(0)
    e = pl.program_id(1)
    r = MOE_ROWS
    unroll = SUBLANES

    @pl.when(e == 0)
    def _():
        f8_ref[...] = jnp.zeros_like(f8_ref)

    base0 = off_ref[s * N_EXPERTS + e]

    def tile(t, carry):
        base = base0 + t * r

        def gather(i, c):
            for u in range(unroll):
                rr = i * unroll + u
                tok = tok_ref[0, 0, base + rr]
                src = pl.multiple_of(tok * SUBLANES, SUBLANES)
                xg_ref[pl.ds(rr, SUBLANES, stride=MOE_PLANE), :] = x8_ref[pl.ds(src, SUBLANES), :]
            return c

        lax.fori_loop(0, r // unroll, gather, 0)
        x = jnp.concatenate([xg_ref[j * MOE_PLANE:j * MOE_PLANE + r, :] for j in range(SUBLANES)],
                            axis=1).astype(BF16)
        glu = jnp.minimum(_dot(x, w1g_ref[0]) + b1g_ref[0], SWIGLU_LIMIT)
        lin = jnp.clip(_dot(x, w1l_ref[0]) + b1l_ref[0], -SWIGLU_LIMIT, SWIGLU_LIMIT)
        act = (glu * jax.nn.sigmoid(SWIGLU_ALPHA * glu) * (lin + 1.0)).astype(BF16)
        y = _dot(act, w2_ref[0]) + b2_ref[0]
        for j in range(SUBLANES):
            yp_ref[j * MOE_PLANE:j * MOE_PLANE + r, :] = y[:, j * LANES:(j + 1) * LANES]

        def scatter(i, c):
            new = []
            for u in range(unroll):
                rr = i * unroll + u
                tok = tok_ref[0, 0, base + rr]
                dst = pl.multiple_of(tok * SUBLANES, SUBLANES)
                gate = gate_ref[0, 0, base + rr]
                new.append((dst, f8_ref[pl.ds(dst, SUBLANES), :]
                            + gate * yp_ref[pl.ds(rr, SUBLANES, stride=MOE_PLANE), :]))
            for dst, val in reversed(new):
                f8_ref[pl.ds(dst, SUBLANES), :] = val
            return c

        lax.fori_loop(0, r // unroll, scatter, 0)
        return carry

    lax.fori_loop(0, cnt_ref[s * N_EXPERTS + e], tile, 0)


def _moe(x8, plan, w1g, w1l, w2, b1g, b1l, b2, ts):
    off, cnt, tok, gate = plan
    n_super = x8.shape[0] // (ts * SUBLANES)
    cap = tok.shape[-1]
    wspec = pl.BlockSpec((1, D_MODEL, EXPERT_FF), lambda s, e, *_: (e, 0, 0))
    w2spec = pl.BlockSpec((1, EXPERT_FF, D_MODEL), lambda s, e, *_: (e, 0, 0))
    bspec = pl.BlockSpec((1, 1, EXPERT_FF), lambda s, e, *_: (e, 0, 0))
    smem = pl.BlockSpec((1, 1, cap), lambda s, e, *_: (s, 0, 0), memory_space=pltpu.SMEM)
    win = pl.BlockSpec((ts * SUBLANES, LANES), lambda s, e, *_: (s, 0), pipeline_mode=pl.Buffered(1))
    grid_spec = pltpu.PrefetchScalarGridSpec(
        num_scalar_prefetch=2,
        grid=(n_super, N_EXPERTS),
        in_specs=[smem, smem, win, wspec, wspec, w2spec, bspec, bspec, bspec],
        out_specs=pl.BlockSpec((ts * SUBLANES, LANES), lambda s, e, *_: (s, 0), pipeline_mode=pl.Buffered(1)),
        scratch_shapes=[pltpu.VMEM((SUBLANES * MOE_PLANE, LANES), F32),
                        pltpu.VMEM((SUBLANES * MOE_PLANE, LANES), F32)],
    )
    return pl.pallas_call(
        functools.partial(_moe_kernel, ts),
        grid_spec=grid_spec,
        out_shape=jax.ShapeDtypeStruct(x8.shape, F32),
        compiler_params=_params("arbitrary", "arbitrary"),
        name="moe_experts",
    )(off, cnt, tok, gate, x8, w1g, w1l, w2, b1g, b1l, b2)


def _moe_plan(idx, gate, ts):
    n_tok = idx.shape[0]
    n_super = n_tok // ts
    n_rows = ts * TOP_K
    cap = n_rows + N_EXPERTS * MOE_ROWS
    flat_e = idx.reshape(n_super, n_rows)
    flat_g = gate.reshape(n_super, n_rows)
    order = jnp.argsort(flat_e, axis=1, stable=True)
    sorted_tok = (order // TOP_K).astype(I32)
    sorted_gate = jnp.take_along_axis(flat_g, order, axis=1)
    counts = jnp.sum(flat_e[:, :, None] == jnp.arange(N_EXPERTS, dtype=I32)[None, None, :], axis=1).astype(I32)
    tiles = (counts + MOE_ROWS - 1) // MOE_ROWS
    padded = tiles * MOE_ROWS
    off = jnp.cumsum(padded, axis=1) - padded
    start = jnp.cumsum(counts, axis=1) - counts
    slot = jnp.arange(cap, dtype=I32)
    ends = off + padded
    slot_e = jnp.minimum(jnp.sum(slot[None, :, None] >= ends[:, None, :], axis=2), N_EXPERTS - 1).astype(I32)
    within = slot[None, :] - jnp.take_along_axis(off, slot_e, axis=1)
    valid = within < jnp.take_along_axis(counts, slot_e, axis=1)
    src = jnp.clip(jnp.take_along_axis(start, slot_e, axis=1) + within, 0, n_rows - 1)
    tok = jnp.where(valid, jnp.take_along_axis(sorted_tok, src, axis=1), 0).astype(I32)
    gat = jnp.where(valid, jnp.take_along_axis(sorted_gate, src, axis=1), 0.0).astype(F32)
    return (off.reshape(-1).astype(I32), tiles.reshape(-1).astype(I32),
            tok.reshape(n_super, 1, cap), gat.reshape(n_super, 1, cap))


def _final_kernel(h_ref, f8_ref, mod_ref, w_ref, o_ref):
    x = h_ref[0] + mod_ref[0, 0, 5:6, :] * _from_token_vreg(f8_ref, ROW_TILE)
    o_ref[0] = _rms(x) * w_ref[...]


def _final_norm(h, f8, mod, w, n_ctx_tiles):
    b, lt, _ = h.shape
    nt = lt // ROW_TILE
    nl = nt - n_ctx_tiles
    return pl.pallas_call(
        _final_kernel,
        grid=(b, nl),
        in_specs=[pl.BlockSpec((1, ROW_TILE, D_MODEL), lambda i, t: (i, t + n_ctx_tiles, 0)),
                  pl.BlockSpec((ROW_TILE * SUBLANES, LANES), lambda i, t: (i * nt + t + n_ctx_tiles, 0)),
                  pl.BlockSpec((1, 1, N_MOD, D_MODEL), lambda i, t: (i, 1, 0, 0)),
                  pl.BlockSpec((1, D_MODEL), lambda i, t: (0, 0))],
        out_specs=pl.BlockSpec((1, ROW_TILE, D_MODEL), lambda i, t: (i, t, 0)),
        out_shape=jax.ShapeDtypeStruct((b, nl * ROW_TILE, D_MODEL), F32),
        compiler_params=_params("parallel", "parallel"),
        name="final_norm",
    )(h, f8, mod, w.reshape(1, D_MODEL))


def _rope_partner():
    i = jnp.arange(MLA_ROPE)
    return jnp.where((i % 16) < 8, i + 8, i - 8)


def _pad_cols(w, width):
    return jnp.pad(w, ((0, 0), (0, width - w.shape[1])))


def _in_weight(w_in):
    sizes = (SSD_D_INNER, SSD_XBC, 2 * SSD_HEADS, GLA_W, GLA_W, GLA_W, GLA_W, 2 * GLA_GATE_RANK,
             MLA_Q_RANK, MLA_KV_RANK, MLA_ROPE)
    cols, acc = [], 0
    for s in sizes:
        cols.append(w_in[:, acc:acc + s])
        acc += s
    z, xbc, dt, gq, gk, gv, gr, glr, cq, ckv, kpe = cols
    rope_at = lambda w: jnp.pad(w, ((0, 0), (MLA_NOPE, LANES - MLA_NOPE - MLA_ROPE)))
    parts = [z, xbc, _pad_cols(dt, LANES), gq, gk, gv, gr, _pad_cols(glr, LANES), _pad_cols(cq, 2 * LANES), ckv,
             rope_at(kpe), rope_at(kpe[:, _rope_partner()])]
    return jnp.concatenate(parts, axis=1).astype(BF16)


def _mla_weights(w_uq, w_ukv):
    wq = w_uq.reshape(MLA_Q_RANK, MLA_HEADS, MLA_QK)
    nope, rope = wq[..., :MLA_NOPE], wq[..., MLA_NOPE:]
    zeros = jnp.zeros((MLA_Q_RANK, MLA_HEADS, LANES - MLA_QK), F32)
    main = jnp.concatenate([nope, rope, zeros], axis=-1)
    swap = jnp.concatenate([jnp.zeros_like(nope), rope[..., _rope_partner()], zeros], axis=-1)
    pad_rows = lambda w: jnp.pad(w.reshape(MLA_Q_RANK, MLA_HEADS * LANES), ((0, 2 * LANES - MLA_Q_RANK), (0, 0)))
    wkv = w_ukv.reshape(MLA_KV_RANK, MLA_HEADS, MLA_NOPE + MLA_V)
    wk = jnp.pad(wkv[..., :MLA_NOPE], ((0, 0), (0, 0), (0, LANES - MLA_NOPE))).reshape(MLA_KV_RANK, MLA_HEADS * LANES)
    wv = wkv[..., MLA_NOPE:].reshape(MLA_KV_RANK, MLA_HEADS * MLA_V)
    return pad_rows(main).astype(BF16), pad_rows(swap).astype(BF16), wk.astype(BF16), wv.astype(BF16)


def _rope_tables(n_ctx, n_lat):
    pos = jnp.arange(n_lat, dtype=F32)
    rowp = jnp.floor(pos / GRID_W)
    colp = pos - rowp * GRID_W
    half = MLA_ROPE // 2
    inv_freq = 1.0 / (ROPE_BASE ** (jnp.arange(0, half, 2, dtype=F32) / half))
    ang = jnp.stack([rowp[:, None] * inv_freq, colp[:, None] * inv_freq], axis=1)
    cos, sin = jnp.cos(ang), jnp.sin(ang)
    cos32 = jnp.concatenate([cos, cos], axis=2).reshape(n_lat, MLA_ROPE)
    sin32 = jnp.concatenate([-sin, sin], axis=2).reshape(n_lat, MLA_ROPE)
    cos32 = jnp.concatenate([jnp.ones((n_ctx, MLA_ROPE), F32), cos32], axis=0)
    sin32 = jnp.concatenate([jnp.zeros((n_ctx, MLA_ROPE), F32), sin32], axis=0)
    n = n_ctx + n_lat
    cos_t = jnp.concatenate([jnp.ones((n, MLA_NOPE), F32), cos32, jnp.zeros((n, LANES - MLA_QK), F32)], axis=1)
    sin_t = jnp.concatenate([jnp.zeros((n, MLA_NOPE), F32), sin32, jnp.zeros((n, LANES - MLA_QK), F32)], axis=1)
    return cos_t, sin_t


def kernel(x, c, ctx, c_ctx, w_mod, b_mod, norm1_w, w_in, ssd_conv_w, ssd_conv_b, ssd_a_log, ssd_dt_bias, ssd_d, ssd_norm_w, gla_g_up_w, gla_g_up_b, gla_norm_w, mla_q_norm_w, mla_w_uq, mla_kv_norm_w, mla_w_ukv, w_out, norm2_w, router_w, router_b, expert_w1, expert_b1, expert_w2, expert_b2, final_norm_w):
    b, n_lat, _ = x.shape
    n_ctx = ctx.shape[1]
    lt = n_ctx + n_lat
    depth = w_mod.shape[0]
    assert n_ctx % ROW_TILE == 0 and n_lat % ROW_TILE == 0 and n_lat % GRID_W == 0
    n_ctx_tiles = n_ctx // ROW_TILE
    n_ctx_chunks = n_ctx // CHUNK
    n_tok = b * lt
    ts = MOE_SUPER if n_tok % MOE_SUPER == 0 else n_tok
    assert b <= 16

    h = jnp.concatenate([ctx, x], axis=1)
    cvec = jnp.concatenate([c, c_ctx[None, :], jnp.zeros((24 - b - 1, D_MODEL), F32)], axis=0)
    mod_all = _modulation(cvec, w_mod, b_mod).reshape(depth, 24, N_MOD, D_MODEL)
    mods = [jnp.stack([jnp.broadcast_to(mod_all[l, b], (b, N_MOD, D_MODEL)), mod_all[l, :b]], axis=1)
            for l in range(depth)]
    cos_t, sin_t = _rope_tables(n_ctx, n_lat)

    f8 = None
    for l in range(depth):
        outs = _in_projection(h, f8, mods[l - 1] if l else None, mods[l], norm1_w[l], _in_weight(w_in[l]),
                              n_ctx_tiles)
        z, xbc, dt, gla_in, glr, mla_in = outs[:6]
        if f8 is not None:
            h = outs[6]

        a_coef = -jnp.exp(ssd_a_log[l].astype(F32))
        xbc_act = _ssd_conv(xbc, ssd_conv_w[l], ssd_conv_b[l], n_ctx_tiles)
        dtr = jnp.transpose(dt[:, :, :2 * SSD_HEADS], (0, 2, 1)).reshape(b, 2, SSD_HEADS, lt)
        yf = _ssd_scan(False, xbc_act, dt, dtr, a_coef, ssd_dt_bias[l], n_ctx_chunks)
        dsk = jnp.repeat(ssd_d[l].astype(F32), SSD_D_INNER // SSD_HEADS).reshape(1, SSD_D_INNER)
        ssd_o = _ssd_scan(True, xbc_act, dt, dtr, a_coef, ssd_dt_bias[l], n_ctx_chunks,
                          extra=(yf, z, dsk, ssd_norm_w[l].reshape(1, SSD_D_INNER)))

        wg = jnp.zeros((2, LANES, GLA_W), F32)
        for d in range(2):
            wg = wg.at[d, d * GLA_GATE_RANK:(d + 1) * GLA_GATE_RANK].set(gla_g_up_w[l, d])
        of = _gla_scan(False, gla_in, glr, wg, gla_g_up_b[l], n_ctx_chunks)
        gnw = jnp.tile(gla_norm_w[l].astype(F32), GLA_HEADS).reshape(1, GLA_W)
        gla_o = _gla_scan(True, gla_in, glr, wg, gla_g_up_b[l], n_ctx_chunks, extra=(of, gnw))

        wq, wqs, wk, wv = _mla_weights(mla_w_uq[l], mla_w_ukv[l])
        qnw = jnp.pad(mla_q_norm_w[l], (0, 2 * LANES - MLA_Q_RANK)).reshape(1, 2 * LANES)
        qh, kh, vh = _mla_prep(mla_in, qnw, wq, wqs, mla_kv_norm_w[l].reshape(1, MLA_KV_RANK), wk, wv, cos_t, sin_t)
        mla_o = _attention(qh, kh, vh, n_ctx)

        rw = _pad_cols(router_w[l], LANES).astype(BF16)
        rb = jnp.concatenate([router_b[l].astype(F32), jnp.full((LANES - N_EXPERTS,), NEG, F32)]).reshape(1, LANES)
        h, v8, idx, gate = _out_projection(ssd_o, gla_o, mla_o, h, mods[l], w_out[l].astype(BF16), norm2_w[l],
                                           rw, rb, n_ctx_tiles)

        plan = _moe_plan(idx.reshape(n_tok, LANES)[:, :TOP_K], gate.reshape(n_tok, LANES)[:, :TOP_K], ts)
        w1 = expert_w1[l]
        f8 = _moe(v8, plan, w1[:, :, 0::2].astype(BF16), w1[:, :, 1::2].astype(BF16), expert_w2[l].astype(BF16),
                  expert_b1[l][:, None, 0::2], expert_b1[l][:, None, 1::2], expert_b2[l][:, None, :], ts)

    return _final_norm(h, f8, mods[depth - 1], final_norm_w, n_ctx_tiles)
```

```python
import functools

import jax
import jax.numpy as jnp
from jax import lax
from jax.experimental import pallas as pl
from jax.experimental.pallas import tpu as pltpu

F32 = jnp.float32
BF16 = jnp.bfloat16
I32 = jnp.int32

D_MODEL = 1024
N_MOD = 6
NORM_EPS = 1e-6
SSD_HEADS, SSD_GROUPS, SSD_STATE = 8, 2, 64
SSD_D_INNER, SSD_XBC = 512, 768
GLA_HEADS, GLA_W, GLA_GATE_RANK, GLA_TAU = 4, 256, 16, 16.0
MLA_HEADS, MLA_Q_RANK, MLA_KV_RANK = 4, 192, 128
MLA_NOPE, MLA_ROPE, MLA_V, MLA_QK = 64, 32, 64, 96
GRID_W, ROPE_BASE = 64, 10000.0
N_EXPERTS, TOP_K, EXPERT_FF = 32, 4, 1024
SWIGLU_ALPHA, SWIGLU_LIMIT = 1.702, 7.0

LANES = 128
SUBLANES = 8
ROW_TILE = 256
CHUNK = 128
MOE_SUPER = 4096
MOE_ROWS = 128
MOE_PLANE = MOE_ROWS + SUBLANES
VMEM_LIMIT = 56 * 1024 * 1024
NEG = -1e30

IN_Z, IN_XBC, IN_DT, IN_GLA, IN_GLR, IN_MLA = 0, 512, 1280, 1408, 2432, 2560
IN_PAD_WIDTH = 3200
MLA_IN_W = 640


def _dot(a, b):
    return jnp.dot(a, b, preferred_element_type=F32)


def _dot_nt(a, b):
    return lax.dot_general(a, b, (((1,), (1,)), ((), ())), preferred_element_type=F32)


def _dot_tn(a, b):
    return lax.dot_general(a, b, (((0,), (0,)), ((), ())), preferred_element_type=F32)


def _split(x):
    hi = x.astype(BF16)
    lo = (x - hi.astype(F32)).astype(BF16)
    return hi, lo


def _dot_sel_rhs(x, m):
    hi, lo = _split(x)
    return _dot(hi, m) + _dot(lo, m)


def _dot_sel_lhs(m, x):
    hi, lo = _split(x)
    return _dot(m, hi) + _dot(m, lo)


def _dot3(a, b):
    ah, al = _split(a)
    bh, bl = _split(b)
    return _dot(ah, bh) + _dot(ah, bl) + _dot(al, bh)


def _softplus(x):
    return jnp.maximum(x, 0.0) + jnp.log(1.0 + jnp.exp(-jnp.abs(x)))


def _silu(x):
    return x * jax.nn.sigmoid(x)


def _rms(x, n=None):
    ms = jnp.sum(x * x, axis=-1, keepdims=True) * (1.0 / (n or x.shape[-1]))
    return x * lax.rsqrt(ms + NORM_EPS)


def _params(*sem):
    return pltpu.CompilerParams(dimension_semantics=sem, vmem_limit_bytes=VMEM_LIMIT)


def _mod_kernel(c_ref, w_ref, b_ref, o_ref):
    cv = c_ref[...]
    o_ref[0] = _dot3(_silu(cv), w_ref[0]) + b_ref[0]


def _modulation(cvec, w_mod, b_mod):
    depth, _, width = w_mod.shape
    rows = cvec.shape[0]
    tn = 1536
    return pl.pallas_call(
        _mod_kernel,
        grid=(depth, width // tn),
        in_specs=[pl.BlockSpec((rows, D_MODEL), lambda l, n: (0, 0)),
                  pl.BlockSpec((1, D_MODEL, tn), lambda l, n: (l, 0, n)),
                  pl.BlockSpec((1, 1, tn), lambda l, n: (l, 0, n))],
        out_specs=pl.BlockSpec((1, rows, tn), lambda l, n: (l, 0, n)),
        out_shape=jax.ShapeDtypeStruct((depth, rows, width), F32),
        compiler_params=_params("parallel", "parallel"),
        name="modulation",
    )(cvec, w_mod, b_mod.reshape(depth, 1, width))


def _from_token_vreg(f8_ref, rows):
    return jnp.concatenate([f8_ref[pl.ds(j, rows, stride=SUBLANES), :] for j in range(SUBLANES)], axis=1)


def _inproj_kernel(has_moe, *refs):
    if has_moe:
        h_ref, f8_ref, pmod_ref, mod_ref, nw_ref, w_ref = refs[:6]
        outs = refs[6:]
    else:
        h_ref, mod_ref, nw_ref, w_ref = refs[:4]
        outs = refs[4:]
    x = h_ref[0]
    if has_moe:
        x = x + pmod_ref[0, 0, 5:6, :] * _from_token_vreg(f8_ref, ROW_TILE)
        outs[6][0] = x
    mod = mod_ref[0, 0]
    u = (_rms(x) * nw_ref[...]) * (1.0 + mod[1:2, :]) + mod[0:1, :]
    ub = u.astype(BF16)
    z_ref, xbc_ref, dt_ref, gla_ref, glr_ref, mla_ref = outs[:6]
    z_ref[0] = _dot(ub, w_ref[:, IN_Z:IN_XBC]).astype(BF16)
    xbc_ref[0] = _dot(ub, w_ref[:, IN_XBC:IN_DT]).astype(BF16)
    dt_ref[0] = _dot(ub, w_ref[:, IN_DT:IN_GLA])
    gla_ref[0] = _dot(ub, w_ref[:, IN_GLA:IN_GLR]).astype(BF16)
    glr_ref[0] = _dot(ub, w_ref[:, IN_GLR:IN_MLA])
    mla_ref[0] = _dot(ub, w_ref[:, IN_MLA:IN_PAD_WIDTH]).astype(BF16)


def _in_projection(h, f8, prev_mod, mod, norm_w, w_pad, n_ctx_tiles):
    b, lt, _ = h.shape
    nt = lt // ROW_TILE
    has_moe = f8 is not None
    row = lambda w: pl.BlockSpec((1, ROW_TILE, w), lambda i, t: (i, t, 0))
    mod_spec = pl.BlockSpec((1, 1, N_MOD, D_MODEL), lambda i, t: (i, jnp.where(t >= n_ctx_tiles, 1, 0), 0, 0))
    in_specs = [row(D_MODEL)]
    args = [h]
    if has_moe:
        in_specs += [pl.BlockSpec((ROW_TILE * SUBLANES, LANES), lambda i, t: (i * nt + t, 0)), mod_spec]
        args += [f8, prev_mod]
    in_specs += [mod_spec, pl.BlockSpec((1, D_MODEL), lambda i, t: (0, 0)),
                 pl.BlockSpec((D_MODEL, IN_PAD_WIDTH), lambda i, t: (0, 0))]
    args += [mod, norm_w.reshape(1, D_MODEL), w_pad]
    widths = [(SSD_D_INNER, BF16), (SSD_XBC, BF16), (LANES, F32), (4 * GLA_W, BF16), (LANES, F32), (MLA_IN_W, BF16)]
    if has_moe:
        widths.append((D_MODEL, F32))
    return pl.pallas_call(
        functools.partial(_inproj_kernel, has_moe),
        grid=(b, nt),
        in_specs=in_specs,
        out_specs=[row(w) for w, _ in widths],
        out_shape=[jax.ShapeDtypeStruct((b, lt, w), dt) for w, dt in widths],
        compiler_params=_params("parallel", "parallel"),
        name="in_projection",
    )(*args)


def _conv_kernel(n_ctx_tiles, prev_ref, cur_ref, next_ref, w_ref, b_ref, o_ref):
    t = pl.program_id(1)
    nt = pl.num_programs(1)
    cur = cur_ref[0].astype(F32)
    has_prev = jnp.logical_and(t != 0, t != n_ctx_tiles)
    has_next = jnp.logical_and(t != n_ctx_tiles - 1, t != nt - 1)
    prev_row = jnp.where(has_prev, prev_ref[0][SUBLANES - 1:SUBLANES, :].astype(F32), 0.0)
    next_row = jnp.where(has_next, next_ref[0][0:1, :].astype(F32), 0.0)
    row = lax.broadcasted_iota(I32, cur.shape, 0)
    before = jnp.where(row == 0, prev_row, pltpu.roll(cur, 1, 0))
    after = jnp.where(row == ROW_TILE - 1, next_row, pltpu.roll(cur, ROW_TILE - 1, 0))
    y = w_ref[0:1, :] * before + w_ref[1:2, :] * cur + w_ref[2:3, :] * after + b_ref[...]
    o_ref[0] = _silu(y).astype(BF16)


def _ssd_conv(xbc, conv_w, conv_b, n_ctx_tiles):
    b, lt, w = xbc.shape
    nt = lt // ROW_TILE
    per = ROW_TILE // SUBLANES
    last = lt // SUBLANES - 1
    return pl.pallas_call(
        functools.partial(_conv_kernel, n_ctx_tiles),
        grid=(b, nt),
        in_specs=[pl.BlockSpec((1, SUBLANES, w), lambda i, t: (i, jnp.maximum(t * per - 1, 0), 0)),
                  pl.BlockSpec((1, ROW_TILE, w), lambda i, t: (i, t, 0)),
                  pl.BlockSpec((1, SUBLANES, w), lambda i, t: (i, jnp.minimum((t + 1) * per, last), 0)),
                  pl.BlockSpec((3, w), lambda i, t: (0, 0)),
                  pl.BlockSpec((1, w), lambda i, t: (0, 0))],
        out_specs=pl.BlockSpec((1, ROW_TILE, w), lambda i, t: (i, t, 0)),
        out_shape=jax.ShapeDtypeStruct((b, lt, w), BF16),
        compiler_params=_params("parallel", "parallel"),
        name="ssd_conv",
    )(xbc, xbc, xbc, conv_w, conv_b.reshape(1, w))


def _scan_chunk(reverse, n_ctx_chunks, n_chunks, j):
    if not reverse:
        return j
    return jnp.where(j < n_ctx_chunks, n_ctx_chunks - 1 - j, n_chunks - 1 - (j - n_ctx_chunks))


def _ssd_kernel(reverse, *refs):
    if reverse:
        (xbc_ref, dtc_ref, dtr_ref, acr_ref, acc_ref, dbr_ref, dbc_ref,
         yf_ref, z_ref, dsk_ref, nw_ref, out_ref, st_ref) = refs
    else:
        xbc_ref, dtc_ref, dtr_ref, acr_ref, acc_ref, dbr_ref, dbc_ref, out_ref, st_ref = refs
    q = CHUNK
    d = 1 if reverse else 0

    @pl.when(pl.program_id(1) == 0)
    def _():
        st_ref[...] = jnp.zeros_like(st_ref)

    xbc = xbc_ref[0]
    xs = xbc[:, :SSD_D_INNER]
    bm = xbc[:, SSD_D_INNER:SSD_D_INNER + LANES]
    cm = xbc[:, SSD_D_INNER + LANES:]
    dt_c = _softplus(dtc_ref[0][:, SSD_HEADS * d:SSD_HEADS * (d + 1)] + dbr_ref[0])
    dt_r = _softplus(dtr_ref[0, 0] + dbc_ref[0])
    a_c = dt_c * acr_ref[0]
    a_r = dt_r * acc_ref[0]
    row = lax.broadcasted_iota(I32, (q, q), 0)
    col = lax.broadcasted_iota(I32, (q, q), 1)
    if not reverse:
        lmat = jnp.where(row >= col, 1.0, 0.0).astype(BF16)
        umat = jnp.where(row <= col, 1.0, 0.0).astype(BF16)
        valid = row >= col
    else:
        lmat = jnp.where(row > col, 1.0, 0.0).astype(BF16)
        umat = jnp.where(row < col, 1.0, 0.0).astype(BF16)
        valid = col >= row
    p_c = _dot_sel_lhs(lmat, a_c)
    p_r = _dot_sel_rhs(a_r, umat)
    if not reverse:
        tot = p_c[q - 1:q, :]
        inter_c = jnp.exp(p_c)
        w_c = jnp.exp(tot - p_c) * dt_c
    else:
        tot = p_c[q - 1:q, :] + a_c[q - 1:q, :]
        inter_c = jnp.exp(tot - p_c)
        w_c = jnp.exp(p_c) * dt_c
    hrow = lax.broadcasted_iota(I32, (SSD_HEADS, SSD_D_INNER), 0)
    hlane = lax.broadcasted_iota(I32, (SSD_HEADS, SSD_D_INNER), 1)
    expand = jnp.where(jnp.right_shift(hlane, 6) == hrow, 1.0, 0.0).astype(BF16)
    inter_f = _dot_sel_rhs(inter_c, expand)
    w_f = _dot_sel_rhs(w_c, expand)
    dec_f = _dot_sel_rhs(jnp.broadcast_to(jnp.exp(tot), (SUBLANES, SSD_HEADS)), expand)[0:1]
    xs_f = xs.astype(F32)
    xw = (xs_f * w_f).astype(BF16)
    half = SSD_D_INNER // SSD_GROUPS
    lane_h = jnp.right_shift(lax.broadcasted_iota(I32, (q, half), 1), 6)
    lane_g = jnp.right_shift(lax.broadcasted_iota(I32, (q, LANES), 1), 6)
    for g in range(SSD_GROUPS):
        sl = slice(g * half, (g + 1) * half)
        cg = jnp.where(lane_g == g, cm, jnp.zeros_like(cm))
        bg = jnp.where(lane_g == g, bm, jnp.zeros_like(bm))
        cb = _dot_nt(cg, bm)
        state = st_ref[g]
        y = _dot(cg, state.astype(BF16)) * inter_f[:, sl]
        xs_g = xs[:, sl]
        for hh in range(SSD_HEADS // SSD_GROUPS):
            h = g * (SSD_HEADS // SSD_GROUPS) + hh
            if not reverse:
                seg = p_c[:, h:h + 1] - p_r[h:h + 1, :]
            else:
                seg = p_r[h:h + 1, :] - p_c[:, h:h + 1]
            decay = jnp.exp(jnp.where(valid, seg, NEG))
            scores = (cb * decay * dt_r[h:h + 1, :]).astype(BF16)
            y = y + _dot(scores, jnp.where(lane_h == hh, xs_g, jnp.zeros_like(xs_g)))
        st_ref[g] = state * dec_f[:, sl] + _dot_tn(bg, xw[:, sl])
        if reverse:
            ytot = y + yf_ref[0][:, sl].astype(F32) + dsk_ref[:, sl] * xs_f[:, sl]
            gated = ytot * _silu(z_ref[0][:, sl].astype(F32))
            out_ref[0, :, sl] = (_rms(gated) * nw_ref[:, sl]).astype(BF16)
        else:
            out_ref[0, :, sl] = y.astype(BF16)


def _ssd_scan(reverse, xbc_act, dtc, dtr, a_coef, dt_bias, n_ctx_chunks, extra=None):
    b, lt, _ = xbc_act.shape
    nc = lt // CHUNK
    d = 1 if reverse else 0
    cidx = functools.partial(_scan_chunk, reverse, n_ctx_chunks, nc)
    row = lambda w: pl.BlockSpec((1, CHUNK, w), lambda i, j: (i, cidx(j), 0))
    small_r = pl.BlockSpec((1, 1, SSD_HEADS), lambda i, j: (d, 0, 0))
    small_c = pl.BlockSpec((1, SSD_HEADS, 1), lambda i, j: (d, 0, 0))
    in_specs = [row(SSD_XBC), row(LANES),
                pl.BlockSpec((1, 1, SSD_HEADS, CHUNK), lambda i, j: (i, d, 0, cidx(j))),
                small_r, small_c, small_r, small_c]
    args = [xbc_act, dtc, dtr, a_coef.reshape(2, 1, SSD_HEADS), a_coef.reshape(2, SSD_HEADS, 1),
            dt_bias.reshape(2, 1, SSD_HEADS), dt_bias.reshape(2, SSD_HEADS, 1)]
    if reverse:
        yf, z, dsk, nw = extra
        vec = pl.BlockSpec((1, SSD_D_INNER), lambda i, j: (0, 0))
        in_specs += [row(SSD_D_INNER), row(SSD_D_INNER), vec, vec]
        args += [yf, z, dsk, nw]
    return pl.pallas_call(
        functools.partial(_ssd_kernel, reverse),
        grid=(b, nc),
        in_specs=in_specs,
        out_specs=row(SSD_D_INNER),
        out_shape=jax.ShapeDtypeStruct((b, lt, SSD_D_INNER), BF16),
        scratch_shapes=[pltpu.VMEM((SSD_GROUPS, LANES, SSD_D_INNER // SSD_GROUPS), F32)],
        compiler_params=_params("parallel", "arbitrary"),
        name="ssd_scan_bwd" if reverse else "ssd_scan_fwd",
    )(*args)


def _gla_kernel(reverse, *refs):
    if reverse:
        gla_ref, glr_ref, wg_ref, bg_ref, of_ref, nw_ref, out_ref, st_ref = refs
    else:
        gla_ref, glr_ref, wg_ref, bg_ref, out_ref, st_ref = refs
    q = CHUNK
    w = GLA_W

    @pl.when(pl.program_id(1) == 0)
    def _():
        st_ref[...] = jnp.zeros_like(st_ref)

    x = gla_ref[0]
    qq = x[:, :w].astype(F32) * (64.0 ** -0.5)
    kk = x[:, w:2 * w].astype(F32)
    vv = x[:, 2 * w:3 * w]
    gpre = _dot3(glr_ref[0], wg_ref[0]) + bg_ref[0]
    g = (jnp.minimum(gpre, 0.0) - jnp.log(1.0 + jnp.exp(-jnp.abs(gpre)))) * (1.0 / GLA_TAU)
    row = lax.broadcasted_iota(I32, (q, q), 0)
    col = lax.broadcasted_iota(I32, (q, q), 1)
    if not reverse:
        gc = _dot_sel_lhs(jnp.where(row >= col, 1.0, 0.0).astype(BF16), g)
        valid = row >= col
        tot = gc[q - 1:q, :]
        q_dec = qq * jnp.exp(gc)
        k_inv = kk * jnp.exp(-gc)
        k_st = kk * jnp.exp(tot - gc)
    else:
        ge = _dot_sel_lhs(jnp.where(row > col, 1.0, 0.0).astype(BF16), g)
        valid = col >= row
        tot = ge[q - 1:q, :] + g[q - 1:q, :]
        q_dec = qq * jnp.exp(tot - ge)
        k_inv = kk * jnp.exp(ge - tot)
        k_st = kk * jnp.exp(ge)
    lane_h = jnp.right_shift(lax.broadcasted_iota(I32, (q, w), 1), 6)
    state = st_ref[...]
    qd_b = q_dec.astype(BF16)
    ki_b = k_inv.astype(BF16)
    o = _dot_nt(qd_b, state.astype(BF16))
    for h in range(GLA_HEADS):
        att = _dot_nt(jnp.where(lane_h == h, qd_b, jnp.zeros_like(qd_b)), ki_b)
        att = jnp.where(valid, att, 0.0).astype(BF16)
        o = o + _dot(att, jnp.where(lane_h == h, vv, jnp.zeros_like(vv)))
    r2 = jnp.right_shift(lax.broadcasted_iota(I32, (w, w), 0), 6)
    c2 = jnp.right_shift(lax.broadcasted_iota(I32, (w, w), 1), 6)
    same_head = r2 == c2
    st_ref[...] = jnp.where(same_head, state * jnp.exp(tot) + _dot_tn(vv, k_st.astype(BF16)), 0.0)
    if reverse:
        ot = o + of_ref[0].astype(F32)
        ms = _dot_sel_rhs(ot * ot, jnp.where(same_head, 1.0, 0.0).astype(BF16)) * (1.0 / 64.0)
        rr = x[:, 3 * w:].astype(F32)
        out_ref[0] = (ot * lax.rsqrt(ms + NORM_EPS) * nw_ref[...] * _silu(rr)).astype(BF16)
    else:
        out_ref[0] = o.astype(BF16)


def _gla_scan(reverse, gla_in, glr, wg_pad, g_up_b, n_ctx_chunks, extra=None):
    b, lt, _ = gla_in.shape
    nc = lt // CHUNK
    d = 1 if reverse else 0
    cidx = functools.partial(_scan_chunk, reverse, n_ctx_chunks, nc)
    row = lambda w: pl.BlockSpec((1, CHUNK, w), lambda i, j: (i, cidx(j), 0))
    in_specs = [row(4 * GLA_W), row(LANES),
                pl.BlockSpec((1, LANES, GLA_W), lambda i, j: (d, 0, 0)),
                pl.BlockSpec((1, 1, GLA_W), lambda i, j: (d, 0, 0))]
    args = [gla_in, glr, wg_pad, g_up_b.reshape(2, 1, GLA_W)]
    if reverse:
        of, nw = extra
        in_specs += [row(GLA_W), pl.BlockSpec((1, GLA_W), lambda i, j: (0, 0))]
        args += [of, nw]
    return pl.pallas_call(
        functools.partial(_gla_kernel, reverse),
        grid=(b, nc),
        in_specs=in_specs,
        out_specs=row(GLA_W),
        out_shape=jax.ShapeDtypeStruct((b, lt, GLA_W), BF16),
        scratch_shapes=[pltpu.VMEM((GLA_W, GLA_W), F32)],
        compiler_params=_params("parallel", "arbitrary"),
        name="gla_scan_bwd" if reverse else "gla_scan_fwd",
    )(*args)


def _mla_prep_kernel(m_ref, qnw_ref, wq_ref, wqs_ref, kvnw_ref, wk_ref, wv_ref, cos_ref, sin_ref,
                     q_ref, k_ref, v_ref):
    m = m_ref[0]
    cq = m[:, :2 * LANES].astype(F32)
    ckv = m[:, 2 * LANES:3 * LANES].astype(F32)
    ka = m[:, 3 * LANES:4 * LANES].astype(F32)
    kb = m[:, 4 * LANES:].astype(F32)
    cos = cos_ref[...]
    sin = sin_ref[...]
    cqn = (_rms(cq, MLA_Q_RANK) * qnw_ref[...]).astype(BF16)
    qm = _dot(cqn, wq_ref[...])
    qs = _dot(cqn, wqs_ref[...])
    ckn = (_rms(ckv) * kvnw_ref[...]).astype(BF16)
    kn = _dot(ckn, wk_ref[...])
    k_rot = ka * cos + kb * sin
    v_ref[0] = _dot(ckn, wv_ref[...]).astype(BF16)
    for h in range(MLA_HEADS):
        sl = slice(h * LANES, (h + 1) * LANES)
        q_ref[0, h] = ((qm[:, sl] * cos + qs[:, sl] * sin) * (MLA_QK ** -0.5)).astype(BF16)
        k_ref[0, h] = (kn[:, sl] + k_rot).astype(BF16)


def _mla_prep(mla_in, qnw, wq, wqs, kvnw, wk, wv, cos_t, sin_t):
    b, lt, _ = mla_in.shape
    nt = lt // ROW_TILE
    full = lambda a: pl.BlockSpec(a.shape, lambda i, t: (0,) * a.ndim)
    tab = pl.BlockSpec((ROW_TILE, LANES), lambda i, t: (t, 0))
    head = pl.BlockSpec((1, MLA_HEADS, ROW_TILE, LANES), lambda i, t: (i, 0, t, 0))
    return pl.pallas_call(
        _mla_prep_kernel,
        grid=(b, nt),
        in_specs=[pl.BlockSpec((1, ROW_TILE, MLA_IN_W), lambda i, t: (i, t, 0)),
                  full(qnw), full(wq), full(wqs), full(kvnw), full(wk), full(wv), tab, tab],
        out_specs=[head, head, pl.BlockSpec((1, ROW_TILE, MLA_HEADS * MLA_V), lambda i, t: (i, t, 0))],
        out_shape=[jax.ShapeDtypeStruct((b, MLA_HEADS, lt, LANES), BF16),
                   jax.ShapeDtypeStruct((b, MLA_HEADS, lt, LANES), BF16),
                   jax.ShapeDtypeStruct((b, lt, MLA_HEADS * MLA_V), BF16)],
        compiler_params=_params("parallel", "parallel"),
        name="mla_prep",
    )(mla_in, qnw, wq, wqs, kvnw, wk, wv, cos_t, sin_t)


def _attn_kernel(n_ctx, q_ref, k_ref, v_ref, o_ref):
    def attend(n_keys):
        lane_h = jnp.right_shift(lax.broadcasted_iota(I32, (n_keys, MLA_HEADS * MLA_V), 1), 6)
        vv = v_ref[0, :n_keys, :]
        acc = jnp.zeros((ROW_TILE, MLA_HEADS * MLA_V), F32)
        for h in range(MLA_HEADS):
            s = _dot_nt(q_ref[0, h], k_ref[0, h, :n_keys, :])
            p = jnp.exp(s - jnp.max(s, axis=-1, keepdims=True))
            denom = jnp.sum(p, axis=-1, keepdims=True)
            pv = _dot(p.astype(BF16), jnp.where(lane_h == h, vv, jnp.zeros_like(vv)))
            acc = acc + pv * (1.0 / denom)
        o_ref[0] = acc.astype(BF16)

    is_ctx = pl.program_id(1) < n_ctx // ROW_TILE
    pl.when(is_ctx)(lambda: attend(n_ctx))
    pl.when(jnp.logical_not(is_ctx))(lambda: attend(k_ref.shape[2]))


def _attention(q, k, v, n_ctx):
    b, _, lt, _ = q.shape
    nt = lt // ROW_TILE
    return pl.pallas_call(
        functools.partial(_attn_kernel, n_ctx),
        grid=(b, nt),
        in_specs=[pl.BlockSpec((1, MLA_HEADS, ROW_TILE, LANES), lambda i, t: (i, 0, t, 0)),
                  pl.BlockSpec((1, MLA_HEADS, lt, LANES), lambda i, t: (i, 0, 0, 0)),
                  pl.BlockSpec((1, lt, MLA_HEADS * MLA_V), lambda i, t: (i, 0, 0))],
        out_specs=pl.BlockSpec((1, ROW_TILE, MLA_HEADS * MLA_V), lambda i, t: (i, t, 0)),
        out_shape=jax.ShapeDtypeStruct((b, lt, MLA_HEADS * MLA_V), BF16),
        compiler_params=_params("parallel", "arbitrary"),
        name="mla_attention",
    )(q, k, v)


def _outproj_kernel(ssd_ref, gla_ref, mla_ref, h_ref, mod_ref, w_ref, nw_ref, rw_ref, rb_ref,
                    h_out_ref, v8_ref, idx_ref, gate_ref):
    mod = mod_ref[0, 0]
    mix = (_dot(ssd_ref[0], w_ref[:SSD_D_INNER, :])
           + _dot(gla_ref[0], w_ref[SSD_D_INNER:SSD_D_INNER + GLA_W, :])
           + _dot(mla_ref[0], w_ref[SSD_D_INNER + GLA_W:, :]))
    hm = h_ref[0] + mod[2:3, :] * mix
    h_out_ref[0] = hm
    v = (_rms(hm) * nw_ref[...]) * (1.0 + mod[4:5, :]) + mod[3:4, :]
    for j in range(SUBLANES):
        v8_ref[pl.ds(j, ROW_TILE, stride=SUBLANES), :] = v[:, j * LANES:(j + 1) * LANES]
    logits = _dot(v.astype(BF16), rw_ref[...]) + rb_ref[...]
    lane = lax.broadcasted_iota(I32, logits.shape, 1)
    idxs, vals = [], []
    for _ in range(TOP_K):
        top = jnp.max(logits, axis=-1, keepdims=True)
        pick = jnp.min(jnp.where(logits == top, lane, LANES), axis=-1, keepdims=True)
        idxs.append(pick)
        vals.append(top)
        logits = jnp.where(lane == pick, 2.0 * NEG, logits)
    exps = [jnp.exp(t - vals[0]) for t in vals]
    inv = 1.0 / (exps[0] + exps[1] + exps[2] + exps[3])
    idx_out = jnp.zeros(lane.shape, I32)
    gate_out = jnp.zeros(lane.shape, F32)
    for k in range(TOP_K):
        idx_out = jnp.where(lane == k, idxs[k], idx_out)
        gate_out = jnp.where(lane == k, exps[k] * inv, gate_out)
    idx_ref[0] = idx_out
    gate_ref[0] = gate_out


def _out_projection(ssd_o, gla_o, mla_o, h, mod, w_out, norm_w, rw_pad, rb_pad, n_ctx_tiles):
    b, lt, _ = h.shape
    nt = lt // ROW_TILE
    row = lambda w: pl.BlockSpec((1, ROW_TILE, w), lambda i, t: (i, t, 0))
    full = lambda a: pl.BlockSpec(a.shape, lambda i, t: (0,) * a.ndim)
    mod_spec = pl.BlockSpec((1, 1, N_MOD, D_MODEL), lambda i, t: (i, jnp.where(t >= n_ctx_tiles, 1, 0), 0, 0))
    nw = norm_w.reshape(1, D_MODEL)
    return pl.pallas_call(
        _outproj_kernel,
        grid=(b, nt),
        in_specs=[row(SSD_D_INNER), row(GLA_W), row(MLA_HEADS * MLA_V), row(D_MODEL), mod_spec,
                  full(w_out), full(nw), full(rw_pad), full(rb_pad)],
        out_specs=[row(D_MODEL),
                   pl.BlockSpec((ROW_TILE * SUBLANES, LANES), lambda i, t: (i * nt + t, 0)),
                   row(LANES), row(LANES)],
        out_shape=[jax.ShapeDtypeStruct((b, lt, D_MODEL), F32),
                   jax.ShapeDtypeStruct((b * lt * SUBLANES, LANES), F32),
                   jax.ShapeDtypeStruct((b, lt, LANES), I32),
                   jax.ShapeDtypeStruct((b, lt, LANES), F32)],
        compiler_params=_params("parallel", "parallel"),
        name="out_projection",
    )(ssd_o, gla_o, mla_o, h, mod, w_out, nw, rw_pad, rb_pad)


def _expert_w1_kernel(w_ref, g_ref, l_ref):
    blk = 2 * LANES
    row = lax.broadcasted_iota(I32, (blk, LANES), 0)
    col = lax.broadcasted_iota(I32, (blk, LANES), 1)
    even = jnp.where(row == 2 * col, 1.0, 0.0).astype(BF16)
    odd = jnp.where(row == 2 * col + 1, 1.0, 0.0).astype(BF16)
    for c in range(2 * EXPERT_FF // blk):
        w = w_ref[0, :, c * blk:(c + 1) * blk].astype(BF16)
        g_ref[0, :, c * LANES:(c + 1) * LANES] = _dot(w, even).astype(BF16)
        l_ref[0, :, c * LANES:(c + 1) * LANES] = _dot(w, odd).astype(BF16)


def _expert_w1(w1):
    n_e = w1.shape[0]
    out = pl.BlockSpec((1, D_MODEL, EXPERT_FF), lambda e: (e, 0, 0))
    return pl.pallas_call(
        _expert_w1_kernel,
        grid=(n_e,),
        in_specs=[pl.BlockSpec((1, D_MODEL, 2 * EXPERT_FF), lambda e: (e, 0, 0))],
        out_specs=[out, out],
        out_shape=[jax.ShapeDtypeStruct((n_e, D_MODEL, EXPERT_FF), BF16)] * 2,
        compiler_params=_params("parallel"),
        name="expert_w1_split",
    )(w1)


def _moe_kernel(ts, off_ref, cnt_ref, end_ref, tok_ref, gate_ref, x8_ref, w1g_ref, w1l_ref, w2_ref,
                b1g_ref, b1l_ref, b2_ref, f8_ref, xg_ref, yp_ref):
    s = pl.program_id(0)
    e = pl.program_id(1)
    r = MOE_ROWS
    unroll = SUBLANES

    @pl.when(e == 0)
    def _():
        f8_ref[...] = jnp.zeros_like(f8_ref)

    base0 = off_ref[s * N_EXPERTS + e]
    end = end_ref[s * N_EXPERTS + e]

    def tile(t, carry):
        base = base0 + t * r

        def gather(i, c):
            for u in range(unroll):
                rr = i * unroll + u
                tok = tok_ref[0, 0, base + rr]
                src = pl.multiple_of(tok * SUBLANES, SUBLANES)
                xg_ref[pl.ds(rr, SUBLANES, stride=MOE_PLANE), :] = x8_ref[pl.ds(src, SUBLANES), :]
            return c

        lax.fori_loop(0, r // unroll, gather, 0)
        x = jnp.concatenate([xg_ref[j * MOE_PLANE:j * MOE_PLANE + r, :] for j in range(SUBLANES)],
                            axis=1).astype(BF16)
        glu = jnp.minimum(_dot(x, w1g_ref[0]) + b1g_ref[0], SWIGLU_LIMIT)
        lin = jnp.clip(_dot(x, w1l_ref[0]) + b1l_ref[0], -SWIGLU_LIMIT, SWIGLU_LIMIT)
        act = (glu * jax.nn.sigmoid(SWIGLU_ALPHA * glu) * (lin + 1.0)).astype(BF16)
        y = _dot(act, w2_ref[0]) + b2_ref[0]
        for j in range(SUBLANES):
            yp_ref[j * MOE_PLANE:j * MOE_PLANE + r, :] = y[:, j * LANES:(j + 1) * LANES]

        def scatter(i, c):
            new = []
            for u in range(unroll):
                rr = i * unroll + u
                tok = tok_ref[0, 0, base + rr]
                dst = pl.multiple_of(tok * SUBLANES, SUBLANES)
                gate = jnp.where(base + rr < end, gate_ref[0, 0, base + rr], 0.0)
                new.append((dst, f8_ref[pl.ds(dst, SUBLANES), :]
                            + gate * yp_ref[pl.ds(rr, SUBLANES, stride=MOE_PLANE), :]))
            for dst, val in reversed(new):
                f8_ref[pl.ds(dst, SUBLANES), :] = val
            return c

        lax.fori_loop(0, r // unroll, scatter, 0)
        return carry

    lax.fori_loop(0, cnt_ref[s * N_EXPERTS + e], tile, 0)


def _moe(x8, plan, w1g, w1l, w2, b1g, b1l, b2, ts):
    off, cnt, end, tok, gate = plan
    n_super = x8.shape[0] // (ts * SUBLANES)
    cap = tok.shape[-1]
    wspec = pl.BlockSpec((1, D_MODEL, EXPERT_FF), lambda s, e, *_: (e, 0, 0))
    w2spec = pl.BlockSpec((1, EXPERT_FF, D_MODEL), lambda s, e, *_: (e, 0, 0))
    bspec = pl.BlockSpec((1, 1, EXPERT_FF), lambda s, e, *_: (e, 0, 0))
    smem = pl.BlockSpec((1, 1, cap), lambda s, e, *_: (s, 0, 0), memory_space=pltpu.SMEM)
    win = pl.BlockSpec((ts * SUBLANES, LANES), lambda s, e, *_: (s, 0), pipeline_mode=pl.Buffered(1))
    grid_spec = pltpu.PrefetchScalarGridSpec(
        num_scalar_prefetch=3,
        grid=(n_super, N_EXPERTS),
        in_specs=[smem, smem, win, wspec, wspec, w2spec, bspec, bspec, bspec],
        out_specs=pl.BlockSpec((ts * SUBLANES, LANES), lambda s, e, *_: (s, 0), pipeline_mode=pl.Buffered(1)),
        scratch_shapes=[pltpu.VMEM((SUBLANES * MOE_PLANE, LANES), F32),
                        pltpu.VMEM((SUBLANES * MOE_PLANE, LANES), F32)],
    )
    return pl.pallas_call(
        functools.partial(_moe_kernel, ts),
        grid_spec=grid_spec,
        out_shape=jax.ShapeDtypeStruct(x8.shape, F32),
        compiler_params=_params("arbitrary", "arbitrary"),
        name="moe_experts",
    )(off, cnt, end, tok, gate, x8, w1g, w1l, w2, b1g, b1l, b2)


def _moe_plan(idx, gate, ts):
    n_tok = idx.shape[0]
    n_super = n_tok // ts
    n_rows = ts * TOP_K
    flat_e = idx.reshape(n_super, n_rows)
    flat_g = gate.reshape(n_super, n_rows)
    pos = jnp.arange(n_rows, dtype=I32)[None, :]
    key = flat_e * n_rows + pos
    tok_of = jnp.broadcast_to(pos // TOP_K, key.shape)
    _, tok, gat = lax.sort((key, tok_of, flat_g), dimension=1, num_keys=1)
    counts = jnp.sum(flat_e[:, :, None] == jnp.arange(N_EXPERTS, dtype=I32)[None, None, :], axis=1).astype(I32)
    end = jnp.cumsum(counts, axis=1)
    off = end - counts
    tiles = (counts + MOE_ROWS - 1) // MOE_ROWS
    tok = jnp.pad(tok, ((0, 0), (0, MOE_ROWS)))
    gat = jnp.pad(gat, ((0, 0), (0, MOE_ROWS)))
    cap = n_rows + MOE_ROWS
    return (off.reshape(-1).astype(I32), tiles.reshape(-1).astype(I32), end.reshape(-1).astype(I32),
            tok.astype(I32).reshape(n_super, 1, cap), gat.astype(F32).reshape(n_super, 1, cap))


def _final_kernel(h_ref, f8_ref, mod_ref, w_ref, o_ref):
    x = h_ref[0] + mod_ref[0, 0, 5:6, :] * _from_token_vreg(f8_ref, ROW_TILE)
    o_ref[0] = _rms(x) * w_ref[...]


def _final_norm(h, f8, mod, w, n_ctx_tiles):
    b, lt, _ = h.shape
    nt = lt // ROW_TILE
    nl = nt - n_ctx_tiles
    return pl.pallas_call(
        _final_kernel,
        grid=(b, nl),
        in_specs=[pl.BlockSpec((1, ROW_TILE, D_MODEL), lambda i, t: (i, t + n_ctx_tiles, 0)),
                  pl.BlockSpec((ROW_TILE * SUBLANES, LANES), lambda i, t: (i * nt + t + n_ctx_tiles, 0)),
                  pl.BlockSpec((1, 1, N_MOD, D_MODEL), lambda i, t: (i, 1, 0, 0)),
                  pl.BlockSpec((1, D_MODEL), lambda i, t: (0, 0))],
        out_specs=pl.BlockSpec((1, ROW_TILE, D_MODEL), lambda i, t: (i, t, 0)),
        out_shape=jax.ShapeDtypeStruct((b, nl * ROW_TILE, D_MODEL), F32),
        compiler_params=_params("parallel", "parallel"),
        name="final_norm",
    )(h, f8, mod, w.reshape(1, D_MODEL))


def _rope_partner():
    i = jnp.arange(MLA_ROPE)
    return jnp.where((i % 16) < 8, i + 8, i - 8)


def _pad_cols(w, width):
    return jnp.pad(w, ((0, 0), (0, width - w.shape[1])))


def _in_weight(w_in):
    sizes = (SSD_D_INNER, SSD_XBC, 2 * SSD_HEADS, GLA_W, GLA_W, GLA_W, GLA_W, 2 * GLA_GATE_RANK,
             MLA_Q_RANK, MLA_KV_RANK, MLA_ROPE)
    cols, acc = [], 0
    for s in sizes:
        cols.append(w_in[:, acc:acc + s])
        acc += s
    z, xbc, dt, gq, gk, gv, gr, glr, cq, ckv, kpe = cols
    rope_at = lambda w: jnp.pad(w, ((0, 0), (MLA_NOPE, LANES - MLA_NOPE - MLA_ROPE)))
    parts = [z, xbc, _pad_cols(dt, LANES), gq, gk, gv, gr, _pad_cols(glr, LANES), _pad_cols(cq, 2 * LANES), ckv,
             rope_at(kpe), rope_at(kpe[:, _rope_partner()])]
    return jnp.concatenate(parts, axis=1).astype(BF16)


def _mla_weights(w_uq, w_ukv):
    wq = w_uq.reshape(MLA_Q_RANK, MLA_HEADS, MLA_QK)
    nope, rope = wq[..., :MLA_NOPE], wq[..., MLA_NOPE:]
    zeros = jnp.zeros((MLA_Q_RANK, MLA_HEADS, LANES - MLA_QK), F32)
    main = jnp.concatenate([nope, rope, zeros], axis=-1)
    swap = jnp.concatenate([jnp.zeros_like(nope), rope[..., _rope_partner()], zeros], axis=-1)
    pad_rows = lambda w: jnp.pad(w.reshape(MLA_Q_RANK, MLA_HEADS * LANES), ((0, 2 * LANES - MLA_Q_RANK), (0, 0)))
    wkv = w_ukv.reshape(MLA_KV_RANK, MLA_HEADS, MLA_NOPE + MLA_V)
    wk = jnp.pad(wkv[..., :MLA_NOPE], ((0, 0), (0, 0), (0, LANES - MLA_NOPE))).reshape(MLA_KV_RANK, MLA_HEADS * LANES)
    wv = wkv[..., MLA_NOPE:].reshape(MLA_KV_RANK, MLA_HEADS * MLA_V)
    return pad_rows(main).astype(BF16), pad_rows(swap).astype(BF16), wk.astype(BF16), wv.astype(BF16)


def _rope_tables(n_ctx, n_lat):
    pos = jnp.arange(n_lat, dtype=F32)
    rowp = jnp.floor(pos / GRID_W)
    colp = pos - rowp * GRID_W
    half = MLA_ROPE // 2
    inv_freq = 1.0 / (ROPE_BASE ** (jnp.arange(0, half, 2, dtype=F32) / half))
    ang = jnp.stack([rowp[:, None] * inv_freq, colp[:, None] * inv_freq], axis=1)
    cos, sin = jnp.cos(ang), jnp.sin(ang)
    cos32 = jnp.concatenate([cos, cos], axis=2).reshape(n_lat, MLA_ROPE)
    sin32 = jnp.concatenate([-sin, sin], axis=2).reshape(n_lat, MLA_ROPE)
    cos32 = jnp.concatenate([jnp.ones((n_ctx, MLA_ROPE), F32), cos32], axis=0)
    sin32 = jnp.concatenate([jnp.zeros((n_ctx, MLA_ROPE), F32), sin32], axis=0)
    n = n_ctx + n_lat
    cos_t = jnp.concatenate([jnp.ones((n, MLA_NOPE), F32), cos32, jnp.zeros((n, LANES - MLA_QK), F32)], axis=1)
    sin_t = jnp.concatenate([jnp.zeros((n, MLA_NOPE), F32), sin32, jnp.zeros((n, LANES - MLA_QK), F32)], axis=1)
    return cos_t, sin_t


def kernel(x, c, ctx, c_ctx, w_mod, b_mod, norm1_w, w_in, ssd_conv_w, ssd_conv_b, ssd_a_log, ssd_dt_bias, ssd_d, ssd_norm_w, gla_g_up_w, gla_g_up_b, gla_norm_w, mla_q_norm_w, mla_w_uq, mla_kv_norm_w, mla_w_ukv, w_out, norm2_w, router_w, router_b, expert_w1, expert_b1, expert_w2, expert_b2, final_norm_w):
    b, n_lat, _ = x.shape
    n_ctx = ctx.shape[1]
    lt = n_ctx + n_lat
    depth = w_mod.shape[0]
    assert n_ctx % ROW_TILE == 0 and n_lat % ROW_TILE == 0 and n_lat % GRID_W == 0
    n_ctx_tiles = n_ctx // ROW_TILE
    n_ctx_chunks = n_ctx // CHUNK
    n_tok = b * lt
    ts = MOE_SUPER if n_tok % MOE_SUPER == 0 else n_tok
    assert b <= 16

    h = jnp.concatenate([ctx, x], axis=1)
    cvec = jnp.concatenate([c, c_ctx[None, :], jnp.zeros((24 - b - 1, D_MODEL), F32)], axis=0)
    mod_all = _modulation(cvec, w_mod, b_mod).reshape(depth, 24, N_MOD, D_MODEL)
    mods = [jnp.stack([jnp.broadcast_to(mod_all[l, b], (b, N_MOD, D_MODEL)), mod_all[l, :b]], axis=1)
            for l in range(depth)]
    cos_t, sin_t = _rope_tables(n_ctx, n_lat)

    f8 = None
    for l in range(depth):
        outs = _in_projection(h, f8, mods[l - 1] if l else None, mods[l], norm1_w[l], _in_weight(w_in[l]),
                              n_ctx_tiles)
        z, xbc, dt, gla_in, glr, mla_in = outs[:6]
        if f8 is not None:
            h = outs[6]

        a_coef = -jnp.exp(ssd_a_log[l].astype(F32))
        xbc_act = _ssd_conv(xbc, ssd_conv_w[l], ssd_conv_b[l], n_ctx_tiles)
        dtr = jnp.transpose(dt[:, :, :2 * SSD_HEADS], (0, 2, 1)).reshape(b, 2, SSD_HEADS, lt)
        yf = _ssd_scan(False, xbc_act, dt, dtr, a_coef, ssd_dt_bias[l], n_ctx_chunks)
        dsk = jnp.repeat(ssd_d[l].astype(F32), SSD_D_INNER // SSD_HEADS).reshape(1, SSD_D_INNER)
        ssd_o = _ssd_scan(True, xbc_act, dt, dtr, a_coef, ssd_dt_bias[l], n_ctx_chunks,
                          extra=(yf, z, dsk, ssd_norm_w[l].reshape(1, SSD_D_INNER)))

        wg = jnp.zeros((2, LANES, GLA_W), F32)
        for d in range(2):
            wg = wg.at[d, d * GLA_GATE_RANK:(d + 1) * GLA_GATE_RANK].set(gla_g_up_w[l, d])
        of = _gla_scan(False, gla_in, glr, wg, gla_g_up_b[l], n_ctx_chunks)
        gnw = jnp.tile(gla_norm_w[l].astype(F32), GLA_HEADS).reshape(1, GLA_W)
        gla_o = _gla_scan(True, gla_in, glr, wg, gla_g_up_b[l], n_ctx_chunks, extra=(of, gnw))

        wq, wqs, wk, wv = _mla_weights(mla_w_uq[l], mla_w_ukv[l])
        qnw = jnp.pad(mla_q_norm_w[l], (0, 2 * LANES - MLA_Q_RANK)).reshape(1, 2 * LANES)
        qh, kh, vh = _mla_prep(mla_in, qnw, wq, wqs, mla_kv_norm_w[l].reshape(1, MLA_KV_RANK), wk, wv, cos_t, sin_t)
        mla_o = _attention(qh, kh, vh, n_ctx)

        rw = _pad_cols(router_w[l], LANES).astype(BF16)
        rb = jnp.concatenate([router_b[l].astype(F32), jnp.full((LANES - N_EXPERTS,), NEG, F32)]).reshape(1, LANES)
        h, v8, idx, gate = _out_projection(ssd_o, gla_o, mla_o, h, mods[l], w_out[l].astype(BF16), norm2_w[l],
                                           rw, rb, n_ctx_tiles)

        plan = _moe_plan(idx.reshape(n_tok, LANES)[:, :TOP_K], gate.reshape(n_tok, LANES)[:, :TOP_K], ts)
        w1g, w1l = _expert_w1(expert_w1[l])
        f8 = _moe(v8, plan, w1g, w1l, expert_w2[l].astype(BF16),
                  expert_b1[l][:, None, 0::2], expert_b1[l][:, None, 1::2], expert_b2[l][:, None, :], ts)

    return _final_norm(h, f8, mods[depth - 1], final_norm_w, n_ctx_tiles)
```

```python
import functools

import jax
import jax.numpy as jnp
from jax import lax
from jax.experimental import pallas as pl
from jax.experimental.pallas import tpu as pltpu

F32 = jnp.float32
BF16 = jnp.bfloat16
I32 = jnp.int32

D_MODEL = 1024
N_MOD = 6
NORM_EPS = 1e-6
SSD_HEADS, SSD_GROUPS, SSD_STATE = 8, 2, 64
SSD_D_INNER, SSD_XBC = 512, 768
GLA_HEADS, GLA_W, GLA_GATE_RANK, GLA_TAU = 4, 256, 16, 16.0
MLA_HEADS, MLA_Q_RANK, MLA_KV_RANK = 4, 192, 128
MLA_NOPE, MLA_ROPE, MLA_V, MLA_QK = 64, 32, 64, 96
GRID_W, ROPE_BASE = 64, 10000.0
N_EXPERTS, TOP_K, EXPERT_FF = 32, 4, 1024
SWIGLU_ALPHA, SWIGLU_LIMIT = 1.702, 7.0

LANES = 128
SUBLANES = 8
ROW_TILE = 256
CHUNK = 128
MOE_SUPER = 4096
MOE_ROWS = 128
MOE_PLANE = MOE_ROWS + SUBLANES
VMEM_LIMIT = 56 * 1024 * 1024
NEG = -1e30

IN_Z, IN_XBC, IN_DT, IN_GLA, IN_GLR, IN_MLA = 0, 512, 1280, 1408, 2432, 2560
IN_PAD_WIDTH = 3200
MLA_IN_W = 640


def _dot(a, b):
    return jnp.dot(a, b, preferred_element_type=F32)


def _dot_nt(a, b):
    return lax.dot_general(a, b, (((1,), (1,)), ((), ())), preferred_element_type=F32)


def _dot_tn(a, b):
    return lax.dot_general(a, b, (((0,), (0,)), ((), ())), preferred_element_type=F32)


def _split(x):
    hi = x.astype(BF16)
    lo = (x - hi.astype(F32)).astype(BF16)
    return hi, lo


def _dot_sel_rhs(x, m):
    hi, lo = _split(x)
    return _dot(hi, m) + _dot(lo, m)


def _dot_sel_lhs(m, x):
    hi, lo = _split(x)
    return _dot(m, hi) + _dot(m, lo)


def _dot3(a, b):
    ah, al = _split(a)
    bh, bl = _split(b)
    return _dot(ah, bh) + _dot(ah, bl) + _dot(al, bh)


def _softplus(x):
    return jnp.maximum(x, 0.0) + jnp.log(1.0 + jnp.exp(-jnp.abs(x)))


def _silu(x):
    return x * jax.nn.sigmoid(x)


def _rms(x, n=None):
    ms = jnp.sum(x * x, axis=-1, keepdims=True) * (1.0 / (n or x.shape[-1]))
    return x * lax.rsqrt(ms + NORM_EPS)


def _params(*sem):
    return pltpu.CompilerParams(dimension_semantics=sem, vmem_limit_bytes=VMEM_LIMIT)


def _mod_kernel(c_ref, w_ref, b_ref, o_ref):
    cv = c_ref[...]
    o_ref[0] = _dot3(_silu(cv), w_ref[0]) + b_ref[0]


def _modulation(cvec, w_mod, b_mod):
    depth, _, width = w_mod.shape
    rows = cvec.shape[0]
    tn = 1536
    return pl.pallas_call(
        _mod_kernel,
        grid=(depth, width // tn),
        in_specs=[pl.BlockSpec((rows, D_MODEL), lambda l, n: (0, 0)),
                  pl.BlockSpec((1, D_MODEL, tn), lambda l, n: (l, 0, n)),
                  pl.BlockSpec((1, 1, tn), lambda l, n: (l, 0, n))],
        out_specs=pl.BlockSpec((1, rows, tn), lambda l, n: (l, 0, n)),
        out_shape=jax.ShapeDtypeStruct((depth, rows, width), F32),
        compiler_params=_params("parallel", "parallel"),
        name="modulation",
    )(cvec, w_mod, b_mod.reshape(depth, 1, width))


def _from_token_vreg(f8_ref, rows):
    return jnp.concatenate([f8_ref[pl.ds(j, rows, stride=SUBLANES), :] for j in range(SUBLANES)], axis=1)


def _inproj_kernel(has_moe, *refs):
    if has_moe:
        h_ref, f8_ref, pmod_ref, mod_ref, nw_ref, w_ref = refs[:6]
        outs = refs[6:]
    else:
        h_ref, mod_ref, nw_ref, w_ref = refs[:4]
        outs = refs[4:]
    x = h_ref[0]
    if has_moe:
        x = x + pmod_ref[0, 0, 5:6, :] * _from_token_vreg(f8_ref, ROW_TILE)
        outs[6][0] = x
    mod = mod_ref[0, 0]
    u = (_rms(x) * nw_ref[...]) * (1.0 + mod[1:2, :]) + mod[0:1, :]
    ub = u.astype(BF16)
    z_ref, xbc_ref, dt_ref, gla_ref, glr_ref, mla_ref = outs[:6]
    z_ref[0] = _dot(ub, w_ref[:, IN_Z:IN_XBC]).astype(BF16)
    xbc_ref[0] = _dot(ub, w_ref[:, IN_XBC:IN_DT]).astype(BF16)
    dt_ref[0] = _dot(ub, w_ref[:, IN_DT:IN_GLA])
    gla_ref[0] = _dot(ub, w_ref[:, IN_GLA:IN_GLR]).astype(BF16)
    glr_ref[0] = _dot(ub, w_ref[:, IN_GLR:IN_MLA])
    mla_ref[0] = _dot(ub, w_ref[:, IN_MLA:IN_PAD_WIDTH]).astype(BF16)


def _in_projection(h, f8, prev_mod, mod, norm_w, w_pad, n_ctx_tiles):
    b, lt, _ = h.shape
    nt = lt // ROW_TILE
    has_moe = f8 is not None
    row = lambda w: pl.BlockSpec((1, ROW_TILE, w), lambda i, t: (i, t, 0))
    mod_spec = pl.BlockSpec((1, 1, N_MOD, D_MODEL), lambda i, t: (i, jnp.where(t >= n_ctx_tiles, 1, 0), 0, 0))
    in_specs = [row(D_MODEL)]
    args = [h]
    if has_moe:
        in_specs += [pl.BlockSpec((ROW_TILE * SUBLANES, LANES), lambda i, t: (i * nt + t, 0)), mod_spec]
        args += [f8, prev_mod]
    in_specs += [mod_spec, pl.BlockSpec((1, D_MODEL), lambda i, t: (0, 0)),
                 pl.BlockSpec((D_MODEL, IN_PAD_WIDTH), lambda i, t: (0, 0))]
    args += [mod, norm_w.reshape(1, D_MODEL), w_pad]
    widths = [(SSD_D_INNER, BF16), (SSD_XBC, BF16), (LANES, F32), (4 * GLA_W, BF16), (LANES, F32), (MLA_IN_W, BF16)]
    if has_moe:
        widths.append((D_MODEL, F32))
    return pl.pallas_call(
        functools.partial(_inproj_kernel, has_moe),
        grid=(b, nt),
        in_specs=in_specs,
        out_specs=[row(w) for w, _ in widths],
        out_shape=[jax.ShapeDtypeStruct((b, lt, w), dt) for w, dt in widths],
        compiler_params=_params("parallel", "parallel"),
        name="in_projection",
    )(*args)


def _conv_kernel(n_ctx_tiles, prev_ref, cur_ref, next_ref, w_ref, b_ref, o_ref):
    t = pl.program_id(1)
    nt = pl.num_programs(1)
    cur = cur_ref[0].astype(F32)
    has_prev = jnp.logical_and(t != 0, t != n_ctx_tiles)
    has_next = jnp.logical_and(t != n_ctx_tiles - 1, t != nt - 1)
    prev_row = jnp.where(has_prev, prev_ref[0][SUBLANES - 1:SUBLANES, :].astype(F32), 0.0)
    next_row = jnp.where(has_next, next_ref[0][0:1, :].astype(F32), 0.0)
    row = lax.broadcasted_iota(I32, cur.shape, 0)
    before = jnp.where(row == 0, prev_row, pltpu.roll(cur, 1, 0))
    after = jnp.where(row == ROW_TILE - 1, next_row, pltpu.roll(cur, ROW_TILE - 1, 0))
    y = w_ref[0:1, :] * before + w_ref[1:2, :] * cur + w_ref[2:3, :] * after + b_ref[...]
    o_ref[0] = _silu(y).astype(BF16)


def _ssd_conv(xbc, conv_w, conv_b, n_ctx_tiles):
    b, lt, w = xbc.shape
    nt = lt // ROW_TILE
    per = ROW_TILE // SUBLANES
    last = lt // SUBLANES - 1
    return pl.pallas_call(
        functools.partial(_conv_kernel, n_ctx_tiles),
        grid=(b, nt),
        in_specs=[pl.BlockSpec((1, SUBLANES, w), lambda i, t: (i, jnp.maximum(t * per - 1, 0), 0)),
                  pl.BlockSpec((1, ROW_TILE, w), lambda i, t: (i, t, 0)),
                  pl.BlockSpec((1, SUBLANES, w), lambda i, t: (i, jnp.minimum((t + 1) * per, last), 0)),
                  pl.BlockSpec((3, w), lambda i, t: (0, 0)),
                  pl.BlockSpec((1, w), lambda i, t: (0, 0))],
        out_specs=pl.BlockSpec((1, ROW_TILE, w), lambda i, t: (i, t, 0)),
        out_shape=jax.ShapeDtypeStruct((b, lt, w), BF16),
        compiler_params=_params("parallel", "parallel"),
        name="ssd_conv",
    )(xbc, xbc, xbc, conv_w, conv_b.reshape(1, w))


def _scan_chunk(reverse, n_ctx_chunks, n_chunks, j):
    if not reverse:
        return j
    return jnp.where(j < n_ctx_chunks, n_ctx_chunks - 1 - j, n_chunks - 1 - (j - n_ctx_chunks))


def _ssd_kernel(reverse, group, *refs):
    if reverse:
        (xbc_ref, dtc_ref, dtr_ref, acr_ref, acc_ref, dbr_ref, dbc_ref,
         yf_ref, z_ref, dsk_ref, nw_ref, out_ref, st_ref) = refs
    else:
        xbc_ref, dtc_ref, dtr_ref, acr_ref, acc_ref, dbr_ref, dbc_ref, out_ref, st_ref = refs
    q = CHUNK
    d = 1 if reverse else 0

    @pl.when(pl.program_id(1) == 0)
    def _():
        st_ref[...] = jnp.zeros_like(st_ref)

    row = lax.broadcasted_iota(I32, (q, q), 0)
    col = lax.broadcasted_iota(I32, (q, q), 1)
    if not reverse:
        lmat = jnp.where(row >= col, 1.0, 0.0).astype(BF16)
        umat = jnp.where(row <= col, 1.0, 0.0).astype(BF16)
        valid = row >= col
    else:
        lmat = jnp.where(row > col, 1.0, 0.0).astype(BF16)
        umat = jnp.where(row < col, 1.0, 0.0).astype(BF16)
        valid = col >= row
    hrow = lax.broadcasted_iota(I32, (SSD_HEADS, SSD_D_INNER), 0)
    hlane = lax.broadcasted_iota(I32, (SSD_HEADS, SSD_D_INNER), 1)
    expand = jnp.where(jnp.right_shift(hlane, 6) == hrow, 1.0, 0.0).astype(BF16)
    half = SSD_D_INNER // SSD_GROUPS
    lane_h = jnp.right_shift(lax.broadcasted_iota(I32, (q, half), 1), 6)
    lane_g = jnp.right_shift(lax.broadcasted_iota(I32, (q, LANES), 1), 6)

    for bi in range(group):
        xbc = xbc_ref[bi]
        xs = xbc[:, :SSD_D_INNER]
        bm = xbc[:, SSD_D_INNER:SSD_D_INNER + LANES]
        cm = xbc[:, SSD_D_INNER + LANES:]
        dt_c = _softplus(dtc_ref[bi][:, SSD_HEADS * d:SSD_HEADS * (d + 1)] + dbr_ref[0])
        dt_r = _softplus(dtr_ref[bi, 0] + dbc_ref[0])
        a_c = dt_c * acr_ref[0]
        a_r = dt_r * acc_ref[0]
        p_c = _dot_sel_lhs(lmat, a_c)
        p_r = _dot_sel_rhs(a_r, umat)
        if not reverse:
            tot = p_c[q - 1:q, :]
            inter_c = jnp.exp(p_c)
            w_c = jnp.exp(tot - p_c) * dt_c
        else:
            tot = p_c[q - 1:q, :] + a_c[q - 1:q, :]
            inter_c = jnp.exp(tot - p_c)
            w_c = jnp.exp(p_c) * dt_c
        inter_f = _dot_sel_rhs(inter_c, expand)
        w_f = _dot_sel_rhs(w_c, expand)
        dec_f = _dot_sel_rhs(jnp.broadcast_to(jnp.exp(tot), (SUBLANES, SSD_HEADS)), expand)[0:1]
        xs_f = xs.astype(F32)
        xw = (xs_f * w_f).astype(BF16)
        for g in range(SSD_GROUPS):
            sl = slice(g * half, (g + 1) * half)
            cg = jnp.where(lane_g == g, cm, jnp.zeros_like(cm))
            bg = jnp.where(lane_g == g, bm, jnp.zeros_like(bm))
            cb = _dot_nt(cg, bm)
            state = st_ref[bi, g]
            y = _dot(cg, state.astype(BF16)) * inter_f[:, sl]
            xs_g = xs[:, sl]
            for hh in range(SSD_HEADS // SSD_GROUPS):
                h = g * (SSD_HEADS // SSD_GROUPS) + hh
                if not reverse:
                    seg = p_c[:, h:h + 1] - p_r[h:h + 1, :]
                else:
                    seg = p_r[h:h + 1, :] - p_c[:, h:h + 1]
                decay = jnp.exp(jnp.where(valid, seg, NEG))
                scores = (cb * decay * dt_r[h:h + 1, :]).astype(BF16)
                y = y + _dot(scores, jnp.where(lane_h == hh, xs_g, jnp.zeros_like(xs_g)))
            st_ref[bi, g] = state * dec_f[:, sl] + _dot_tn(bg, xw[:, sl])
            if reverse:
                ytot = y + yf_ref[bi][:, sl].astype(F32) + dsk_ref[:, sl] * xs_f[:, sl]
                gated = ytot * _silu(z_ref[bi][:, sl].astype(F32))
                out_ref[bi, :, sl] = (_rms(gated) * nw_ref[:, sl]).astype(BF16)
            else:
                out_ref[bi, :, sl] = y.astype(BF16)


def _scan_group(b):
    return 2 if b % 2 == 0 else 1


def _ssd_scan(reverse, xbc_act, dtc, dtr, a_coef, dt_bias, n_ctx_chunks, extra=None):
    b, lt, _ = xbc_act.shape
    nc = lt // CHUNK
    d = 1 if reverse else 0
    grp = _scan_group(b)
    cidx = functools.partial(_scan_chunk, reverse, n_ctx_chunks, nc)
    row = lambda w: pl.BlockSpec((grp, CHUNK, w), lambda i, j: (i, cidx(j), 0))
    small_r = pl.BlockSpec((1, 1, SSD_HEADS), lambda i, j: (d, 0, 0))
    small_c = pl.BlockSpec((1, SSD_HEADS, 1), lambda i, j: (d, 0, 0))
    in_specs = [row(SSD_XBC), row(LANES),
                pl.BlockSpec((grp, 1, SSD_HEADS, CHUNK), lambda i, j: (i, d, 0, cidx(j))),
                small_r, small_c, small_r, small_c]
    args = [xbc_act, dtc, dtr, a_coef.reshape(2, 1, SSD_HEADS), a_coef.reshape(2, SSD_HEADS, 1),
            dt_bias.reshape(2, 1, SSD_HEADS), dt_bias.reshape(2, SSD_HEADS, 1)]
    if reverse:
        yf, z, dsk, nw = extra
        vec = pl.BlockSpec((1, SSD_D_INNER), lambda i, j: (0, 0))
        in_specs += [row(SSD_D_INNER), row(SSD_D_INNER), vec, vec]
        args += [yf, z, dsk, nw]
    return pl.pallas_call(
        functools.partial(_ssd_kernel, reverse, grp),
        grid=(b // grp, nc),
        in_specs=in_specs,
        out_specs=row(SSD_D_INNER),
        out_shape=jax.ShapeDtypeStruct((b, lt, SSD_D_INNER), BF16),
        scratch_shapes=[pltpu.VMEM((grp, SSD_GROUPS, LANES, SSD_D_INNER // SSD_GROUPS), F32)],
        compiler_params=_params("parallel", "arbitrary"),
        name="ssd_scan_bwd" if reverse else "ssd_scan_fwd",
    )(*args)


def _gla_kernel(reverse, group, *refs):
    if reverse:
        gla_ref, glr_ref, wg_ref, bg_ref, of_ref, nw_ref, out_ref, st_ref = refs
    else:
        gla_ref, glr_ref, wg_ref, bg_ref, out_ref, st_ref = refs
    q = CHUNK
    w = GLA_W

    @pl.when(pl.program_id(1) == 0)
    def _():
        st_ref[...] = jnp.zeros_like(st_ref)

    row = lax.broadcasted_iota(I32, (q, q), 0)
    col = lax.broadcasted_iota(I32, (q, q), 1)
    if not reverse:
        cmat = jnp.where(row >= col, 1.0, 0.0).astype(BF16)
        valid = row >= col
    else:
        cmat = jnp.where(row > col, 1.0, 0.0).astype(BF16)
        valid = col >= row
    lane_h = jnp.right_shift(lax.broadcasted_iota(I32, (q, w), 1), 6)
    r2 = jnp.right_shift(lax.broadcasted_iota(I32, (w, w), 0), 6)
    c2 = jnp.right_shift(lax.broadcasted_iota(I32, (w, w), 1), 6)
    same_head = r2 == c2

    for bi in range(group):
        x = gla_ref[bi]
        qq = x[:, :w].astype(F32) * (64.0 ** -0.5)
        kk = x[:, w:2 * w].astype(F32)
        vv = x[:, 2 * w:3 * w]
        gpre = _dot3(glr_ref[bi], wg_ref[0]) + bg_ref[0]
        g = (jnp.minimum(gpre, 0.0) - jnp.log(1.0 + jnp.exp(-jnp.abs(gpre)))) * (1.0 / GLA_TAU)
        gc = _dot_sel_lhs(cmat, g)
        if not reverse:
            tot = gc[q - 1:q, :]
            q_dec = qq * jnp.exp(gc)
            k_inv = kk * jnp.exp(-gc)
            k_st = kk * jnp.exp(tot - gc)
        else:
            tot = gc[q - 1:q, :] + g[q - 1:q, :]
            q_dec = qq * jnp.exp(tot - gc)
            k_inv = kk * jnp.exp(gc - tot)
            k_st = kk * jnp.exp(gc)
        state = st_ref[bi]
        qd_b = q_dec.astype(BF16)
        ki_b = k_inv.astype(BF16)
        o = _dot_nt(qd_b, state.astype(BF16))
        for h in range(GLA_HEADS):
            att = _dot_nt(jnp.where(lane_h == h, qd_b, jnp.zeros_like(qd_b)), ki_b)
            att = jnp.where(valid, att, 0.0).astype(BF16)
            o = o + _dot(att, jnp.where(lane_h == h, vv, jnp.zeros_like(vv)))
        st_ref[bi] = jnp.where(same_head, state * jnp.exp(tot) + _dot_tn(vv, k_st.astype(BF16)), 0.0)
        if reverse:
            ot = o + of_ref[bi].astype(F32)
            ms = _dot_sel_rhs(ot * ot, jnp.where(same_head, 1.0, 0.0).astype(BF16)) * (1.0 / 64.0)
            rr = x[:, 3 * w:].astype(F32)
            out_ref[bi] = (ot * lax.rsqrt(ms + NORM_EPS) * nw_ref[...] * _silu(rr)).astype(BF16)
        else:
            out_ref[bi] = o.astype(BF16)


def _gla_scan(reverse, gla_in, glr, wg_pad, g_up_b, n_ctx_chunks, extra=None):
    b, lt, _ = gla_in.shape
    nc = lt // CHUNK
    d = 1 if reverse else 0
    grp = _scan_group(b)
    cidx = functools.partial(_scan_chunk, reverse, n_ctx_chunks, nc)
    row = lambda w: pl.BlockSpec((grp, CHUNK, w), lambda i, j: (i, cidx(j), 0))
    in_specs = [row(4 * GLA_W), row(LANES),
                pl.BlockSpec((1, LANES, GLA_W), lambda i, j: (d, 0, 0)),
                pl.BlockSpec((1, 1, GLA_W), lambda i, j: (d, 0, 0))]
    args = [gla_in, glr, wg_pad, g_up_b.reshape(2, 1, GLA_W)]
    if reverse:
        of, nw = extra
        in_specs += [row(GLA_W), pl.BlockSpec((1, GLA_W), lambda i, j: (0, 0))]
        args += [of, nw]
    return pl.pallas_call(
        functools.partial(_gla_kernel, reverse, grp),
        grid=(b // grp, nc),
        in_specs=in_specs,
        out_specs=row(GLA_W),
        out_shape=jax.ShapeDtypeStruct((b, lt, GLA_W), BF16),
        scratch_shapes=[pltpu.VMEM((grp, GLA_W, GLA_W), F32)],
        compiler_params=_params("parallel", "arbitrary"),
        name="gla_scan_bwd" if reverse else "gla_scan_fwd",
    )(*args)


def _mla_prep_kernel(m_ref, qnw_ref, wq_ref, wqs_ref, kvnw_ref, wk_ref, wv_ref, cos_ref, sin_ref,
                     q_ref, k_ref, v_ref):
    m = m_ref[0]
    cq = m[:, :2 * LANES].astype(F32)
    ckv = m[:, 2 * LANES:3 * LANES].astype(F32)
    ka = m[:, 3 * LANES:4 * LANES].astype(F32)
    kb = m[:, 4 * LANES:].astype(F32)
    cos = cos_ref[...]
    sin = sin_ref[...]
    cqn = (_rms(cq, MLA_Q_RANK) * qnw_ref[...]).astype(BF16)
    qm = _dot(cqn, wq_ref[...])
    qs = _dot(cqn, wqs_ref[...])
    ckn = (_rms(ckv) * kvnw_ref[...]).astype(BF16)
    kn = _dot(ckn, wk_ref[...])
    k_rot = ka * cos + kb * sin
    v_ref[0] = _dot(ckn, wv_ref[...]).astype(BF16)
    for h in range(MLA_HEADS):
        sl = slice(h * LANES, (h + 1) * LANES)
        q_ref[0, h] = ((qm[:, sl] * cos + qs[:, sl] * sin) * (MLA_QK ** -0.5)).astype(BF16)
        k_ref[0, h] = (kn[:, sl] + k_rot).astype(BF16)


def _mla_prep(mla_in, qnw, wq, wqs, kvnw, wk, wv, cos_t, sin_t):
    b, lt, _ = mla_in.shape
    nt = lt // ROW_TILE
    full = lambda a: pl.BlockSpec(a.shape, lambda i, t: (0,) * a.ndim)
    tab = pl.BlockSpec((ROW_TILE, LANES), lambda i, t: (t, 0))
    head = pl.BlockSpec((1, MLA_HEADS, ROW_TILE, LANES), lambda i, t: (i, 0, t, 0))
    return pl.pallas_call(
        _mla_prep_kernel,
        grid=(b, nt),
        in_specs=[pl.BlockSpec((1, ROW_TILE, MLA_IN_W), lambda i, t: (i, t, 0)),
                  full(qnw), full(wq), full(wqs), full(kvnw), full(wk), full(wv), tab, tab],
        out_specs=[head, head, pl.BlockSpec((1, ROW_TILE, MLA_HEADS * MLA_V), lambda i, t: (i, t, 0))],
        out_shape=[jax.ShapeDtypeStruct((b, MLA_HEADS, lt, LANES), BF16),
                   jax.ShapeDtypeStruct((b, MLA_HEADS, lt, LANES), BF16),
                   jax.ShapeDtypeStruct((b, lt, MLA_HEADS * MLA_V), BF16)],
        compiler_params=_params("parallel", "parallel"),
        name="mla_prep",
    )(mla_in, qnw, wq, wqs, kvnw, wk, wv, cos_t, sin_t)


def _attn_kernel(n_ctx, q_ref, k_ref, v_ref, o_ref):
    def attend(n_keys):
        lane_h = jnp.right_shift(lax.broadcasted_iota(I32, (n_keys, MLA_HEADS * MLA_V), 1), 6)
        vv = v_ref[0, :n_keys, :]
        acc = jnp.zeros((ROW_TILE, MLA_HEADS * MLA_V), F32)
        for h in range(MLA_HEADS):
            s = _dot_nt(q_ref[0, h], k_ref[0, h, :n_keys, :])
            p = jnp.exp(s - jnp.max(s, axis=-1, keepdims=True))
            denom = jnp.sum(p, axis=-1, keepdims=True)
            pv = _dot(p.astype(BF16), jnp.where(lane_h == h, vv, jnp.zeros_like(vv)))
            acc = acc + pv * (1.0 / denom)
        o_ref[0] = acc.astype(BF16)

    is_ctx = pl.program_id(1) < n_ctx // ROW_TILE
    pl.when(is_ctx)(lambda: attend(n_ctx))
    pl.when(jnp.logical_not(is_ctx))(lambda: attend(k_ref.shape[2]))


def _attention(q, k, v, n_ctx):
    b, _, lt, _ = q.shape
    nt = lt // ROW_TILE
    return pl.pallas_call(
        functools.partial(_attn_kernel, n_ctx),
        grid=(b, nt),
        in_specs=[pl.BlockSpec((1, MLA_HEADS, ROW_TILE, LANES), lambda i, t: (i, 0, t, 0)),
                  pl.BlockSpec((1, MLA_HEADS, lt, LANES), lambda i, t: (i, 0, 0, 0)),
                  pl.BlockSpec((1, lt, MLA_HEADS * MLA_V), lambda i, t: (i, 0, 0))],
        out_specs=pl.BlockSpec((1, ROW_TILE, MLA_HEADS * MLA_V), lambda i, t: (i, t, 0)),
        out_shape=jax.ShapeDtypeStruct((b, lt, MLA_HEADS * MLA_V), BF16),
        compiler_params=_params("parallel", "arbitrary"),
        name="mla_attention",
    )(q, k, v)


def _outproj_kernel(ssd_ref, gla_ref, mla_ref, h_ref, mod_ref, w_ref, nw_ref, rw_ref, rb_ref,
                    h_out_ref, v8_ref, idx_ref, gate_ref):
    mod = mod_ref[0, 0]
    mix = (_dot(ssd_ref[0], w_ref[:SSD_D_INNER, :])
           + _dot(gla_ref[0], w_ref[SSD_D_INNER:SSD_D_INNER + GLA_W, :])
           + _dot(mla_ref[0], w_ref[SSD_D_INNER + GLA_W:, :]))
    hm = h_ref[0] + mod[2:3, :] * mix
    h_out_ref[0] = hm
    v = (_rms(hm) * nw_ref[...]) * (1.0 + mod[4:5, :]) + mod[3:4, :]
    for j in range(SUBLANES):
        v8_ref[pl.ds(j, ROW_TILE, stride=SUBLANES), :] = v[:, j * LANES:(j + 1) * LANES]
    logits = _dot(v.astype(BF16), rw_ref[...]) + rb_ref[...]
    lane = lax.broadcasted_iota(I32, logits.shape, 1).astype(F32)
    idxs, vals = [], []
    for _ in range(TOP_K):
        top = jnp.max(logits, axis=-1, keepdims=True)
        pick = jnp.min(jnp.where(logits == top, lane, float(LANES)), axis=-1, keepdims=True)
        idxs.append(pick)
        vals.append(top)
        logits = jnp.where(lane == pick, 2.0 * NEG, logits)
    exps = [jnp.exp(t - vals[0]) for t in vals]
    inv = 1.0 / (exps[0] + exps[1] + exps[2] + exps[3])
    idx_out = jnp.zeros(lane.shape, F32)
    gate_out = jnp.zeros(lane.shape, F32)
    for k in range(TOP_K):
        idx_out = jnp.where(lane == float(k), idxs[k], idx_out)
        gate_out = jnp.where(lane == float(k), exps[k] * inv, gate_out)
    idx_ref[0] = idx_out.astype(I32)
    gate_ref[0] = gate_out


def _out_projection(ssd_o, gla_o, mla_o, h, mod, w_out, norm_w, rw_pad, rb_pad, n_ctx_tiles):
    b, lt, _ = h.shape
    nt = lt // ROW_TILE
    row = lambda w: pl.BlockSpec((1, ROW_TILE, w), lambda i, t: (i, t, 0))
    full = lambda a: pl.BlockSpec(a.shape, lambda i, t: (0,) * a.ndim)
    mod_spec = pl.BlockSpec((1, 1, N_MOD, D_MODEL), lambda i, t: (i, jnp.where(t >= n_ctx_tiles, 1, 0), 0, 0))
    nw = norm_w.reshape(1, D_MODEL)
    return pl.pallas_call(
        _outproj_kernel,
        grid=(b, nt),
        in_specs=[row(SSD_D_INNER), row(GLA_W), row(MLA_HEADS * MLA_V), row(D_MODEL), mod_spec,
                  full(w_out), full(nw), full(rw_pad), full(rb_pad)],
        out_specs=[row(D_MODEL),
                   pl.BlockSpec((ROW_TILE * SUBLANES, LANES), lambda i, t: (i * nt + t, 0)),
                   row(LANES), row(LANES)],
        out_shape=[jax.ShapeDtypeStruct((b, lt, D_MODEL), F32),
                   jax.ShapeDtypeStruct((b * lt * SUBLANES, LANES), F32),
                   jax.ShapeDtypeStruct((b, lt, LANES), I32),
                   jax.ShapeDtypeStruct((b, lt, LANES), F32)],
        compiler_params=_params("parallel", "parallel"),
        name="out_projection",
    )(ssd_o, gla_o, mla_o, h, mod, w_out, nw, rw_pad, rb_pad)


def _expert_w1_kernel(w_ref, g_ref, l_ref):
    blk = 2 * LANES
    row = lax.broadcasted_iota(I32, (blk, LANES), 0)
    col = lax.broadcasted_iota(I32, (blk, LANES), 1)
    even = jnp.where(row == 2 * col, 1.0, 0.0).astype(BF16)
    odd = jnp.where(row == 2 * col + 1, 1.0, 0.0).astype(BF16)
    for c in range(2 * EXPERT_FF // blk):
        w = w_ref[0, :, c * blk:(c + 1) * blk].astype(BF16)
        g_ref[0, :, c * LANES:(c + 1) * LANES] = _dot(w, even).astype(BF16)
        l_ref[0, :, c * LANES:(c + 1) * LANES] = _dot(w, odd).astype(BF16)


def _expert_w1(w1):
    n_e = w1.shape[0]
    out = pl.BlockSpec((1, D_MODEL, EXPERT_FF), lambda e: (e, 0, 0))
    return pl.pallas_call(
        _expert_w1_kernel,
        grid=(n_e,),
        in_specs=[pl.BlockSpec((1, D_MODEL, 2 * EXPERT_FF), lambda e: (e, 0, 0))],
        out_specs=[out, out],
        out_shape=[jax.ShapeDtypeStruct((n_e, D_MODEL, EXPERT_FF), BF16)] * 2,
        compiler_params=_params("parallel"),
        name="expert_w1_split",
    )(w1)


def _moe_kernel(ts, off_ref, cnt_ref, end_ref, tok_ref, gate_ref, x8_ref, w1g_ref, w1l_ref, w2_ref,
                b1g_ref, b1l_ref, b2_ref, f8_ref, xa_ref, xb_ref, ya_ref, yb_ref):
    s = pl.program_id(0)
    e = pl.program_id(1)
    r = MOE_ROWS
    batch = SUBLANES

    @pl.when(e == 0)
    def _():
        f8_ref[...] = jnp.zeros_like(f8_ref)

    base0 = off_ref[s * N_EXPERTS + e]
    end = end_ref[s * N_EXPERTS + e]
    n_tiles = cnt_ref[s * N_EXPERTS + e]

    def gather(base, xg_ref):
        for rr in range(r):
            src = pl.multiple_of(tok_ref[0, 0, base + rr] * SUBLANES, SUBLANES)
            xg_ref[pl.ds(rr, SUBLANES, stride=MOE_PLANE), :] = x8_ref[pl.ds(src, SUBLANES), :]

    def ffn(xg_ref, yp_ref):
        x = jnp.concatenate([xg_ref[j * MOE_PLANE:j * MOE_PLANE + r, :] for j in range(SUBLANES)],
                            axis=1).astype(BF16)
        glu = jnp.minimum(_dot(x, w1g_ref[0]) + b1g_ref[0], SWIGLU_LIMIT)
        lin = jnp.clip(_dot(x, w1l_ref[0]) + b1l_ref[0], -SWIGLU_LIMIT, SWIGLU_LIMIT)
        act = (glu * jax.nn.sigmoid(SWIGLU_ALPHA * glu) * (lin + 1.0)).astype(BF16)
        y = _dot(act, w2_ref[0]) + b2_ref[0]
        for j in range(SUBLANES):
            yp_ref[j * MOE_PLANE:j * MOE_PLANE + r, :] = y[:, j * LANES:(j + 1) * LANES]

    def scatter(base, yp_ref):
        for r0 in range(0, r, batch):
            new = []
            for rr in range(r0, r0 + batch):
                dst = pl.multiple_of(tok_ref[0, 0, base + rr] * SUBLANES, SUBLANES)
                gate = jnp.where(base + rr < end, gate_ref[0, 0, base + rr], 0.0)
                new.append((dst, f8_ref[pl.ds(dst, SUBLANES), :]
                            + gate * yp_ref[pl.ds(rr, SUBLANES, stride=MOE_PLANE), :]))
            for dst, val in reversed(new):
                f8_ref[pl.ds(dst, SUBLANES), :] = val

    def pair(i, carry):
        base = base0 + i * (2 * r)
        gather(base, xa_ref)
        gather(base + r, xb_ref)
        ffn(xa_ref, ya_ref)
        ffn(xb_ref, yb_ref)
        scatter(base, ya_ref)
        scatter(base + r, yb_ref)
        return carry

    lax.fori_loop(0, jnp.right_shift(n_tiles, 1), pair, 0)

    @pl.when(jnp.bitwise_and(n_tiles, 1) == 1)
    def _():
        base = base0 + (n_tiles - 1) * r
        gather(base, xa_ref)
        ffn(xa_ref, ya_ref)
        scatter(base, ya_ref)


def _moe(x8, plan, w1g, w1l, w2, b1g, b1l, b2, ts):
    off, cnt, end, tok, gate = plan
    n_super = x8.shape[0] // (ts * SUBLANES)
    cap = tok.shape[-1]
    wspec = pl.BlockSpec((1, D_MODEL, EXPERT_FF), lambda s, e, *_: (e, 0, 0))
    w2spec = pl.BlockSpec((1, EXPERT_FF, D_MODEL), lambda s, e, *_: (e, 0, 0))
    bspec = pl.BlockSpec((1, 1, EXPERT_FF), lambda s, e, *_: (e, 0, 0))
    smem = pl.BlockSpec((1, 1, cap), lambda s, e, *_: (s, 0, 0), memory_space=pltpu.SMEM)
    win = pl.BlockSpec((ts * SUBLANES, LANES), lambda s, e, *_: (s, 0), pipeline_mode=pl.Buffered(1))
    grid_spec = pltpu.PrefetchScalarGridSpec(
        num_scalar_prefetch=3,
        grid=(n_super, N_EXPERTS),
        in_specs=[smem, smem, win, wspec, wspec, w2spec, bspec, bspec, bspec],
        out_specs=pl.BlockSpec((ts * SUBLANES, LANES), lambda s, e, *_: (s, 0), pipeline_mode=pl.Buffered(1)),
        scratch_shapes=[pltpu.VMEM((SUBLANES * MOE_PLANE, LANES), F32)] * 4,
    )
    return pl.pallas_call(
        functools.partial(_moe_kernel, ts),
        grid_spec=grid_spec,
        out_shape=jax.ShapeDtypeStruct(x8.shape, F32),
        compiler_params=_params("arbitrary", "arbitrary"),
        name="moe_experts",
    )(off, cnt, end, tok, gate, x8, w1g, w1l, w2, b1g, b1l, b2)


def _moe_plan(idx, gate, ts):
    n_tok = idx.shape[0]
    n_super = n_tok // ts
    n_rows = ts * TOP_K
    flat_e = idx.reshape(n_super, n_rows)
    flat_g = gate.reshape(n_super, n_rows)
    pos = jnp.arange(n_rows, dtype=I32)[None, :]
    key = flat_e * n_rows + pos
    tok_of = jnp.broadcast_to(pos // TOP_K, key.shape)
    _, tok, gat = lax.sort((key, tok_of, flat_g), dimension=1, num_keys=1)
    counts = jnp.sum(flat_e[:, :, None] == jnp.arange(N_EXPERTS, dtype=I32)[None, None, :], axis=1).astype(I32)
    end = jnp.cumsum(counts, axis=1)
    off = end - counts
    tiles = (counts + MOE_ROWS - 1) // MOE_ROWS
    tok = jnp.pad(tok, ((0, 0), (0, MOE_ROWS)))
    gat = jnp.pad(gat, ((0, 0), (0, MOE_ROWS)))
    cap = n_rows + MOE_ROWS
    return (off.reshape(-1).astype(I32), tiles.reshape(-1).astype(I32), end.reshape(-1).astype(I32),
            tok.astype(I32).reshape(n_super, 1, cap), gat.astype(F32).reshape(n_super, 1, cap))


def _final_kernel(h_ref, f8_ref, mod_ref, w_ref, o_ref):
    x = h_ref[0] + mod_ref[0, 0, 5:6, :] * _from_token_vreg(f8_ref, ROW_TILE)
    o_ref[0] = _rms(x) * w_ref[...]


def _final_norm(h, f8, mod, w, n_ctx_tiles):
    b, lt, _ = h.shape
    nt = lt // ROW_TILE
    nl = nt - n_ctx_tiles
    return pl.pallas_call(
        _final_kernel,
        grid=(b, nl),
        in_specs=[pl.BlockSpec((1, ROW_TILE, D_MODEL), lambda i, t: (i, t + n_ctx_tiles, 0)),
                  pl.BlockSpec((ROW_TILE * SUBLANES, LANES), lambda i, t: (i * nt + t + n_ctx_tiles, 0)),
                  pl.BlockSpec((1, 1, N_MOD, D_MODEL), lambda i, t: (i, 1, 0, 0)),
                  pl.BlockSpec((1, D_MODEL), lambda i, t: (0, 0))],
        out_specs=pl.BlockSpec((1, ROW_TILE, D_MODEL), lambda i, t: (i, t, 0)),
        out_shape=jax.ShapeDtypeStruct((b, nl * ROW_TILE, D_MODEL), F32),
        compiler_params=_params("parallel", "parallel"),
        name="final_norm",
    )(h, f8, mod, w.reshape(1, D_MODEL))


def _rope_partner():
    i = jnp.arange(MLA_ROPE)
    return jnp.where((i % 16) < 8, i + 8, i - 8)


def _pad_cols(w, width):
    return jnp.pad(w, ((0, 0), (0, width - w.shape[1])))


def _in_weight(w_in):
    sizes = (SSD_D_INNER, SSD_XBC, 2 * SSD_HEADS, GLA_W, GLA_W, GLA_W, GLA_W, 2 * GLA_GATE_RANK,
             MLA_Q_RANK, MLA_KV_RANK, MLA_ROPE)
    cols, acc = [], 0
    for s in sizes:
        cols.append(w_in[:, acc:acc + s])
        acc += s
    z, xbc, dt, gq, gk, gv, gr, glr, cq, ckv, kpe = cols
    rope_at = lambda w: jnp.pad(w, ((0, 0), (MLA_NOPE, LANES - MLA_NOPE - MLA_ROPE)))
    parts = [z, xbc, _pad_cols(dt, LANES), gq, gk, gv, gr, _pad_cols(glr, LANES), _pad_cols(cq, 2 * LANES), ckv,
             rope_at(kpe), rope_at(kpe[:, _rope_partner()])]
    return jnp.concatenate(parts, axis=1).astype(BF16)


def _mla_weights(w_uq, w_ukv):
    wq = w_uq.reshape(MLA_Q_RANK, MLA_HEADS, MLA_QK)
    nope, rope = wq[..., :MLA_NOPE], wq[..., MLA_NOPE:]
    zeros = jnp.zeros((MLA_Q_RANK, MLA_HEADS, LANES - MLA_QK), F32)
    main = jnp.concatenate([nope, rope, zeros], axis=-1)
    swap = jnp.concatenate([jnp.zeros_like(nope), rope[..., _rope_partner()], zeros], axis=-1)
    pad_rows = lambda w: jnp.pad(w.reshape(MLA_Q_RANK, MLA_HEADS * LANES), ((0, 2 * LANES - MLA_Q_RANK), (0, 0)))
    wkv = w_ukv.reshape(MLA_KV_RANK, MLA_HEADS, MLA_NOPE + MLA_V)
    wk = jnp.pad(wkv[..., :MLA_NOPE], ((0, 0), (0, 0), (0, LANES - MLA_NOPE))).reshape(MLA_KV_RANK, MLA_HEADS * LANES)
    wv = wkv[..., MLA_NOPE:].reshape(MLA_KV_RANK, MLA_HEADS * MLA_V)
    return pad_rows(main).astype(BF16), pad_rows(swap).astype(BF16), wk.astype(BF16), wv.astype(BF16)


def _rope_tables(n_ctx, n_lat):
    pos = jnp.arange(n_lat, dtype=F32)
    rowp = jnp.floor(pos / GRID_W)
    colp = pos - rowp * GRID_W
    half = MLA_ROPE // 2
    inv_freq = 1.0 / (ROPE_BASE ** (jnp.arange(0, half, 2, dtype=F32) / half))
    ang = jnp.stack([rowp[:, None] * inv_freq, colp[:, None] * inv_freq], axis=1)
    cos, sin = jnp.cos(ang), jnp.sin(ang)
    cos32 = jnp.concatenate([cos, cos], axis=2).reshape(n_lat, MLA_ROPE)
    sin32 = jnp.concatenate([-sin, sin], axis=2).reshape(n_lat, MLA_ROPE)
    cos32 = jnp.concatenate([jnp.ones((n_ctx, MLA_ROPE), F32), cos32], axis=0)
    sin32 = jnp.concatenate([jnp.zeros((n_ctx, MLA_ROPE), F32), sin32], axis=0)
    n = n_ctx + n_lat
    cos_t = jnp.concatenate([jnp.ones((n, MLA_NOPE), F32), cos32, jnp.zeros((n, LANES - MLA_QK), F32)], axis=1)
    sin_t = jnp.concatenate([jnp.zeros((n, MLA_NOPE), F32), sin32, jnp.zeros((n, LANES - MLA_QK), F32)], axis=1)
    return cos_t, sin_t


def kernel(x, c, ctx, c_ctx, w_mod, b_mod, norm1_w, w_in, ssd_conv_w, ssd_conv_b, ssd_a_log, ssd_dt_bias, ssd_d, ssd_norm_w, gla_g_up_w, gla_g_up_b, gla_norm_w, mla_q_norm_w, mla_w_uq, mla_kv_norm_w, mla_w_ukv, w_out, norm2_w, router_w, router_b, expert_w1, expert_b1, expert_w2, expert_b2, final_norm_w):
    b, n_lat, _ = x.shape
    n_ctx = ctx.shape[1]
    lt = n_ctx + n_lat
    depth = w_mod.shape[0]
    assert n_ctx % ROW_TILE == 0 and n_lat % ROW_TILE == 0 and n_lat % GRID_W == 0
    n_ctx_tiles = n_ctx // ROW_TILE
    n_ctx_chunks = n_ctx // CHUNK
    n_tok = b * lt
    ts = MOE_SUPER if n_tok % MOE_SUPER == 0 else n_tok
    assert b <= 16

    h = jnp.concatenate([ctx, x], axis=1)
    cvec = jnp.concatenate([c, c_ctx[None, :], jnp.zeros((24 - b - 1, D_MODEL), F32)], axis=0)
    mod_all = _modulation(cvec, w_mod, b_mod).reshape(depth, 24, N_MOD, D_MODEL)
    mods = [jnp.stack([jnp.broadcast_to(mod_all[l, b], (b, N_MOD, D_MODEL)), mod_all[l, :b]], axis=1)
            for l in range(depth)]
    cos_t, sin_t = _rope_tables(n_ctx, n_lat)

    f8 = None
    for l in range(depth):
        outs = _in_projection(h, f8, mods[l - 1] if l else None, mods[l], norm1_w[l], _in_weight(w_in[l]),
                              n_ctx_tiles)
        z, xbc, dt, gla_in, glr, mla_in = outs[:6]
        if f8 is not None:
            h = outs[6]

        a_coef = -jnp.exp(ssd_a_log[l].astype(F32))
        xbc_act = _ssd_conv(xbc, ssd_conv_w[l], ssd_conv_b[l], n_ctx_tiles)
        dtr = jnp.transpose(dt[:, :, :2 * SSD_HEADS], (0, 2, 1)).reshape(b, 2, SSD_HEADS, lt)
        yf = _ssd_scan(False, xbc_act, dt, dtr, a_coef, ssd_dt_bias[l], n_ctx_chunks)
        dsk = jnp.repeat(ssd_d[l].astype(F32), SSD_D_INNER // SSD_HEADS).reshape(1, SSD_D_INNER)
        ssd_o = _ssd_scan(True, xbc_act, dt, dtr, a_coef, ssd_dt_bias[l], n_ctx_chunks,
                          extra=(yf, z, dsk, ssd_norm_w[l].reshape(1, SSD_D_INNER)))

        wg = jnp.zeros((2, LANES, GLA_W), F32)
        for d in range(2):
            wg = wg.at[d, d * GLA_GATE_RANK:(d + 1) * GLA_GATE_RANK].set(gla_g_up_w[l, d])
        of = _gla_scan(False, gla_in, glr, wg, gla_g_up_b[l], n_ctx_chunks)
        gnw = jnp.tile(gla_norm_w[l].astype(F32), GLA_HEADS).reshape(1, GLA_W)
        gla_o = _gla_scan(True, gla_in, glr, wg, gla_g_up_b[l], n_ctx_chunks, extra=(of, gnw))

        wq, wqs, wk, wv = _mla_weights(mla_w_uq[l], mla_w_ukv[l])
        qnw = jnp.pad(mla_q_norm_w[l], (0, 2 * LANES - MLA_Q_RANK)).reshape(1, 2 * LANES)
        qh, kh, vh = _mla_prep(mla_in, qnw, wq, wqs, mla_kv_norm_w[l].reshape(1, MLA_KV_RANK), wk, wv, cos_t, sin_t)
        mla_o = _attention(qh, kh, vh, n_ctx)

        rw = _pad_cols(router_w[l], LANES).astype(BF16)
        rb = jnp.concatenate([router_b[l].astype(F32), jnp.full((LANES - N_EXPERTS,), NEG, F32)]).reshape(1, LANES)
        h, v8, idx, gate = _out_projection(ssd_o, gla_o, mla_o, h, mods[l], w_out[l].astype(BF16), norm2_w[l],
                                           rw, rb, n_ctx_tiles)

        plan = _moe_plan(idx.reshape(n_tok, LANES)[:, :TOP_K], gate.reshape(n_tok, LANES)[:, :TOP_K], ts)
        w1g, w1l = _expert_w1(expert_w1[l])
        f8 = _moe(v8, plan, w1g, w1l, expert_w2[l].astype(BF16),
                  expert_b1[l][:, None, 0::2], expert_b1[l][:, None, 1::2], expert_b2[l][:, None, :], ts)

    return _final_norm(h, f8, mods[depth - 1], final_norm_w, n_ctx_tiles)
```

```python
import functools

import jax
import jax.numpy as jnp
from jax import lax
from jax.experimental import pallas as pl
from jax.experimental.pallas import tpu as pltpu

F32 = jnp.float32
BF16 = jnp.bfloat16
I32 = jnp.int32

D_MODEL = 1024
N_MOD = 6
NORM_EPS = 1e-6
SSD_HEADS, SSD_GROUPS, SSD_STATE = 8, 2, 64
SSD_D_INNER, SSD_XBC = 512, 768
GLA_HEADS, GLA_W, GLA_GATE_RANK, GLA_TAU = 4, 256, 16, 16.0
MLA_HEADS, MLA_Q_RANK, MLA_KV_RANK = 4, 192, 128
MLA_NOPE, MLA_ROPE, MLA_V, MLA_QK = 64, 32, 64, 96
GRID_W, ROPE_BASE = 64, 10000.0
N_EXPERTS, TOP_K, EXPERT_FF = 32, 4, 1024
SWIGLU_ALPHA, SWIGLU_LIMIT = 1.702, 7.0

LANES = 128
SUBLANES = 8
ROW_TILE = 256
CHUNK = 128
MOE_SUPER = 4096
MOE_ROWS = 128
MOE_PLANE = MOE_ROWS + SUBLANES
VMEM_LIMIT = 56 * 1024 * 1024
NEG = -1e30

IN_Z, IN_XBC, IN_DT, IN_GLA, IN_GLR, IN_MLA = 0, 512, 1280, 1408, 2432, 2560
IN_PAD_WIDTH = 3200
MLA_IN_W = 640


def _dot(a, b):
    return jnp.dot(a, b, preferred_element_type=F32)


def _dot_nt(a, b):
    return lax.dot_general(a, b, (((1,), (1,)), ((), ())), preferred_element_type=F32)


def _dot_tn(a, b):
    return lax.dot_general(a, b, (((0,), (0,)), ((), ())), preferred_element_type=F32)


def _split(x):
    hi = x.astype(BF16)
    lo = (x - hi.astype(F32)).astype(BF16)
    return hi, lo


def _dot_sel_rhs(x, m):
    hi, lo = _split(x)
    return _dot(hi, m) + _dot(lo, m)


def _dot_sel_lhs(m, x):
    hi, lo = _split(x)
    return _dot(m, hi) + _dot(m, lo)


def _dot3(a, b):
    ah, al = _split(a)
    bh, bl = _split(b)
    return _dot(ah, bh) + _dot(ah, bl) + _dot(al, bh)


def _softplus(x):
    return jnp.maximum(x, 0.0) + jnp.log(1.0 + jnp.exp(-jnp.abs(x)))


def _silu(x):
    return x * jax.nn.sigmoid(x)


def _rms(x, n=None):
    ms = jnp.sum(x * x, axis=-1, keepdims=True) * (1.0 / (n or x.shape[-1]))
    return x * lax.rsqrt(ms + NORM_EPS)


def _params(*sem):
    return pltpu.CompilerParams(dimension_semantics=sem, vmem_limit_bytes=VMEM_LIMIT)


def _mod_kernel(c_ref, w_ref, b_ref, o_ref):
    cv = c_ref[...]
    o_ref[0] = _dot3(_silu(cv), w_ref[0]) + b_ref[0]


def _modulation(cvec, w_mod, b_mod):
    depth, _, width = w_mod.shape
    rows = cvec.shape[0]
    tn = 1536
    return pl.pallas_call(
        _mod_kernel,
        grid=(depth, width // tn),
        in_specs=[pl.BlockSpec((rows, D_MODEL), lambda l, n: (0, 0)),
                  pl.BlockSpec((1, D_MODEL, tn), lambda l, n: (l, 0, n)),
                  pl.BlockSpec((1, 1, tn), lambda l, n: (l, 0, n))],
        out_specs=pl.BlockSpec((1, rows, tn), lambda l, n: (l, 0, n)),
        out_shape=jax.ShapeDtypeStruct((depth, rows, width), F32),
        compiler_params=_params("parallel", "parallel"),
        name="modulation",
    )(cvec, w_mod, b_mod.reshape(depth, 1, width))


def _from_token_vreg(f8_ref, rows):
    return jnp.concatenate([f8_ref[pl.ds(j, rows, stride=SUBLANES), :] for j in range(SUBLANES)], axis=1)


def _inproj_kernel(has_moe, *refs):
    if has_moe:
        h_ref, f8_ref, pmod_ref, mod_ref, nw_ref, w_ref = refs[:6]
        outs = refs[6:]
    else:
        h_ref, mod_ref, nw_ref, w_ref = refs[:4]
        outs = refs[4:]
    x = h_ref[0]
    if has_moe:
        x = x + pmod_ref[0, 0, 5:6, :] * _from_token_vreg(f8_ref, ROW_TILE)
        outs[6][0] = x
    mod = mod_ref[0, 0]
    u = (_rms(x) * nw_ref[...]) * (1.0 + mod[1:2, :]) + mod[0:1, :]
    ub = u.astype(BF16)
    z_ref, xbc_ref, dt_ref, gla_ref, glr_ref, mla_ref = outs[:6]
    z_ref[0] = _dot(ub, w_ref[:, IN_Z:IN_XBC]).astype(BF16)
    xbc_ref[0] = _dot(ub, w_ref[:, IN_XBC:IN_DT]).astype(BF16)
    dt_ref[0] = _dot(ub, w_ref[:, IN_DT:IN_GLA])
    gla_ref[0] = _dot(ub, w_ref[:, IN_GLA:IN_GLR]).astype(BF16)
    glr_ref[0] = _dot(ub, w_ref[:, IN_GLR:IN_MLA])
    mla_ref[0] = _dot(ub, w_ref[:, IN_MLA:IN_PAD_WIDTH]).astype(BF16)


def _in_projection(h, f8, prev_mod, mod, norm_w, w_pad, n_ctx_tiles):
    b, lt, _ = h.shape
    nt = lt // ROW_TILE
    has_moe = f8 is not None
    row = lambda w: pl.BlockSpec((1, ROW_TILE, w), lambda i, t: (i, t, 0))
    mod_spec = pl.BlockSpec((1, 1, N_MOD, D_MODEL), lambda i, t: (i, jnp.where(t >= n_ctx_tiles, 1, 0), 0, 0))
    in_specs = [row(D_MODEL)]
    args = [h]
    if has_moe:
        in_specs += [pl.BlockSpec((ROW_TILE * SUBLANES, LANES), lambda i, t: (i * nt + t, 0)), mod_spec]
        args += [f8, prev_mod]
    in_specs += [mod_spec, pl.BlockSpec((1, D_MODEL), lambda i, t: (0, 0)),
                 pl.BlockSpec((D_MODEL, IN_PAD_WIDTH), lambda i, t: (0, 0))]
    args += [mod, norm_w.reshape(1, D_MODEL), w_pad]
    widths = [(SSD_D_INNER, BF16), (SSD_XBC, BF16), (LANES, F32), (4 * GLA_W, BF16), (LANES, F32), (MLA_IN_W, BF16)]
    if has_moe:
        widths.append((D_MODEL, F32))
    return pl.pallas_call(
        functools.partial(_inproj_kernel, has_moe),
        grid=(b, nt),
        in_specs=in_specs,
        out_specs=[row(w) for w, _ in widths],
        out_shape=[jax.ShapeDtypeStruct((b, lt, w), dt) for w, dt in widths],
        compiler_params=_params("parallel", "parallel"),
        name="in_projection",
    )(*args)


def _conv_kernel(n_ctx_tiles, prev_ref, cur_ref, next_ref, w_ref, b_ref, o_ref):
    t = pl.program_id(1)
    nt = pl.num_programs(1)
    cur = cur_ref[0].astype(F32)
    has_prev = jnp.logical_and(t != 0, t != n_ctx_tiles)
    has_next = jnp.logical_and(t != n_ctx_tiles - 1, t != nt - 1)
    prev_row = jnp.where(has_prev, prev_ref[0][SUBLANES - 1:SUBLANES, :].astype(F32), 0.0)
    next_row = jnp.where(has_next, next_ref[0][0:1, :].astype(F32), 0.0)
    row = lax.broadcasted_iota(I32, cur.shape, 0)
    before = jnp.where(row == 0, prev_row, pltpu.roll(cur, 1, 0))
    after = jnp.where(row == ROW_TILE - 1, next_row, pltpu.roll(cur, ROW_TILE - 1, 0))
    y = w_ref[0:1, :] * before + w_ref[1:2, :] * cur + w_ref[2:3, :] * after + b_ref[...]
    o_ref[0] = _silu(y).astype(BF16)


def _ssd_conv(xbc, conv_w, conv_b, n_ctx_tiles):
    b, lt, w = xbc.shape
    nt = lt // ROW_TILE
    per = ROW_TILE // SUBLANES
    last = lt // SUBLANES - 1
    return pl.pallas_call(
        functools.partial(_conv_kernel, n_ctx_tiles),
        grid=(b, nt),
        in_specs=[pl.BlockSpec((1, SUBLANES, w), lambda i, t: (i, jnp.maximum(t * per - 1, 0), 0)),
                  pl.BlockSpec((1, ROW_TILE, w), lambda i, t: (i, t, 0)),
                  pl.BlockSpec((1, SUBLANES, w), lambda i, t: (i, jnp.minimum((t + 1) * per, last), 0)),
                  pl.BlockSpec((3, w), lambda i, t: (0, 0)),
                  pl.BlockSpec((1, w), lambda i, t: (0, 0))],
        out_specs=pl.BlockSpec((1, ROW_TILE, w), lambda i, t: (i, t, 0)),
        out_shape=jax.ShapeDtypeStruct((b, lt, w), BF16),
        compiler_params=_params("parallel", "parallel"),
        name="ssd_conv",
    )(xbc, xbc, xbc, conv_w, conv_b.reshape(1, w))


def _scan_chunk(reverse, n_ctx_chunks, n_chunks, j):
    if not reverse:
        return j
    return jnp.where(j < n_ctx_chunks, n_ctx_chunks - 1 - j, n_chunks - 1 - (j - n_ctx_chunks))


def _ssd_kernel(reverse, group, *refs):
    if reverse:
        (xbc_ref, dtc_ref, dtr_ref, acr_ref, acc_ref, dbr_ref, dbc_ref,
         yf_ref, z_ref, dsk_ref, nw_ref, out_ref, st_ref) = refs
    else:
        xbc_ref, dtc_ref, dtr_ref, acr_ref, acc_ref, dbr_ref, dbc_ref, out_ref, st_ref = refs
    q = CHUNK
    d = 1 if reverse else 0

    @pl.when(pl.program_id(1) == 0)
    def _():
        st_ref[...] = jnp.zeros_like(st_ref)

    row = lax.broadcasted_iota(I32, (q, q), 0)
    col = lax.broadcasted_iota(I32, (q, q), 1)
    if not reverse:
        lmat = jnp.where(row >= col, 1.0, 0.0).astype(BF16)
        umat = jnp.where(row <= col, 1.0, 0.0).astype(BF16)
        valid = row >= col
    else:
        lmat = jnp.where(row > col, 1.0, 0.0).astype(BF16)
        umat = jnp.where(row < col, 1.0, 0.0).astype(BF16)
        valid = col >= row
    hrow = lax.broadcasted_iota(I32, (SSD_HEADS, SSD_D_INNER), 0)
    hlane = lax.broadcasted_iota(I32, (SSD_HEADS, SSD_D_INNER), 1)
    expand = jnp.where(jnp.right_shift(hlane, 6) == hrow, 1.0, 0.0).astype(BF16)
    half = SSD_D_INNER // SSD_GROUPS
    lane_h = jnp.right_shift(lax.broadcasted_iota(I32, (q, half), 1), 6)
    lane_g = jnp.right_shift(lax.broadcasted_iota(I32, (q, LANES), 1), 6)

    for bi in range(group):
        xbc = xbc_ref[bi]
        xs = xbc[:, :SSD_D_INNER]
        bm = xbc[:, SSD_D_INNER:SSD_D_INNER + LANES]
        cm = xbc[:, SSD_D_INNER + LANES:]
        dt_c = _softplus(dtc_ref[bi][:, SSD_HEADS * d:SSD_HEADS * (d + 1)] + dbr_ref[0])
        dt_r = _softplus(dtr_ref[bi, 0] + dbc_ref[0])
        a_c = dt_c * acr_ref[0]
        a_r = dt_r * acc_ref[0]
        p_c = _dot_sel_lhs(lmat, a_c)
        p_r = _dot_sel_rhs(a_r, umat)
        if not reverse:
            tot = p_c[q - 1:q, :]
            inter_c = jnp.exp(p_c)
            w_c = jnp.exp(tot - p_c) * dt_c
        else:
            tot = p_c[q - 1:q, :] + a_c[q - 1:q, :]
            inter_c = jnp.exp(tot - p_c)
            w_c = jnp.exp(p_c) * dt_c
        inter_f = _dot_sel_rhs(inter_c, expand)
        w_f = _dot_sel_rhs(w_c, expand)
        dec_f = _dot_sel_rhs(jnp.broadcast_to(jnp.exp(tot), (SUBLANES, SSD_HEADS)), expand)[0:1]
        xs_f = xs.astype(F32)
        xw = (xs_f * w_f).astype(BF16)
        for g in range(SSD_GROUPS):
            sl = slice(g * half, (g + 1) * half)
            cg = jnp.where(lane_g == g, cm, jnp.zeros_like(cm))
            bg = jnp.where(lane_g == g, bm, jnp.zeros_like(bm))
            cb = _dot_nt(cg, bm)
            state = st_ref[bi, g]
            y = _dot(cg, state.astype(BF16)) * inter_f[:, sl]
            xs_g = xs[:, sl]
            for hh in range(SSD_HEADS // SSD_GROUPS):
                h = g * (SSD_HEADS // SSD_GROUPS) + hh
                if not reverse:
                    seg = p_c[:, h:h + 1] - p_r[h:h + 1, :]
                else:
                    seg = p_r[h:h + 1, :] - p_c[:, h:h + 1]
                decay = jnp.exp(jnp.where(valid, seg, NEG))
                scores = (cb * decay * dt_r[h:h + 1, :]).astype(BF16)
                y = y + _dot(scores, jnp.where(lane_h == hh, xs_g, jnp.zeros_like(xs_g)))
            st_ref[bi, g] = state * dec_f[:, sl] + _dot_tn(bg, xw[:, sl])
            if reverse:
                ytot = y + yf_ref[bi][:, sl].astype(F32) + dsk_ref[:, sl] * xs_f[:, sl]
                gated = ytot * _silu(z_ref[bi][:, sl].astype(F32))
                out_ref[bi, :, sl] = (_rms(gated) * nw_ref[:, sl]).astype(BF16)
            else:
                out_ref[bi, :, sl] = y.astype(BF16)


def _scan_group(b):
    return next(g for g in (4, 2, 1) if b % g == 0)


def _ssd_scan(reverse, xbc_act, dtc, dtr, a_coef, dt_bias, n_ctx_chunks, extra=None):
    b, lt, _ = xbc_act.shape
    nc = lt // CHUNK
    d = 1 if reverse else 0
    grp = _scan_group(b)
    cidx = functools.partial(_scan_chunk, reverse, n_ctx_chunks, nc)
    row = lambda w: pl.BlockSpec((grp, CHUNK, w), lambda i, j: (i, cidx(j), 0))
    small_r = pl.BlockSpec((1, 1, SSD_HEADS), lambda i, j: (d, 0, 0))
    small_c = pl.BlockSpec((1, SSD_HEADS, 1), lambda i, j: (d, 0, 0))
    in_specs = [row(SSD_XBC), row(LANES),
                pl.BlockSpec((grp, 1, SSD_HEADS, CHUNK), lambda i, j: (i, d, 0, cidx(j))),
                small_r, small_c, small_r, small_c]
    args = [xbc_act, dtc, dtr, a_coef.reshape(2, 1, SSD_HEADS), a_coef.reshape(2, SSD_HEADS, 1),
            dt_bias.reshape(2, 1, SSD_HEADS), dt_bias.reshape(2, SSD_HEADS, 1)]
    if reverse:
        yf, z, dsk, nw = extra
        vec = pl.BlockSpec((1, SSD_D_INNER), lambda i, j: (0, 0))
        in_specs += [row(SSD_D_INNER), row(SSD_D_INNER), vec, vec]
        args += [yf, z, dsk, nw]
    return pl.pallas_call(
        functools.partial(_ssd_kernel, reverse, grp),
        grid=(b // grp, nc),
        in_specs=in_specs,
        out_specs=row(SSD_D_INNER),
        out_shape=jax.ShapeDtypeStruct((b, lt, SSD_D_INNER), BF16),
        scratch_shapes=[pltpu.VMEM((grp, SSD_GROUPS, LANES, SSD_D_INNER // SSD_GROUPS), F32)],
        compiler_params=_params("parallel", "arbitrary"),
        name="ssd_scan_bwd" if reverse else "ssd_scan_fwd",
    )(*args)


def _gla_kernel(reverse, group, *refs):
    if reverse:
        gla_ref, glr_ref, wg_ref, bg_ref, of_ref, nw_ref, out_ref, st_ref = refs
    else:
        gla_ref, glr_ref, wg_ref, bg_ref, out_ref, st_ref = refs
    q = CHUNK
    w = GLA_W

    @pl.when(pl.program_id(1) == 0)
    def _():
        st_ref[...] = jnp.zeros_like(st_ref)

    row = lax.broadcasted_iota(I32, (q, q), 0)
    col = lax.broadcasted_iota(I32, (q, q), 1)
    if not reverse:
        cmat = jnp.where(row >= col, 1.0, 0.0).astype(BF16)
        valid = row >= col
    else:
        cmat = jnp.where(row > col, 1.0, 0.0).astype(BF16)
        valid = col >= row
    lane_h = jnp.right_shift(lax.broadcasted_iota(I32, (q, w), 1), 6)
    r2 = jnp.right_shift(lax.broadcasted_iota(I32, (w, w), 0), 6)
    c2 = jnp.right_shift(lax.broadcasted_iota(I32, (w, w), 1), 6)
    same_head = r2 == c2

    for bi in range(group):
        x = gla_ref[bi]
        qq = x[:, :w].astype(F32) * (64.0 ** -0.5)
        kk = x[:, w:2 * w].astype(F32)
        vv = x[:, 2 * w:3 * w]
        gpre = _dot3(glr_ref[bi], wg_ref[0]) + bg_ref[0]
        g = (jnp.minimum(gpre, 0.0) - jnp.log(1.0 + jnp.exp(-jnp.abs(gpre)))) * (1.0 / GLA_TAU)
        gc = _dot_sel_lhs(cmat, g)
        if not reverse:
            tot = gc[q - 1:q, :]
            q_dec = qq * jnp.exp(gc)
            k_inv = kk * jnp.exp(-gc)
            k_st = kk * jnp.exp(tot - gc)
        else:
            tot = gc[q - 1:q, :] + g[q - 1:q, :]
            q_dec = qq * jnp.exp(tot - gc)
            k_inv = kk * jnp.exp(gc - tot)
            k_st = kk * jnp.exp(gc)
        state = st_ref[bi]
        qd_b = q_dec.astype(BF16)
        ki_b = k_inv.astype(BF16)
        o = _dot_nt(qd_b, state.astype(BF16))
        for h in range(GLA_HEADS):
            att = _dot_nt(jnp.where(lane_h == h, qd_b, jnp.zeros_like(qd_b)), ki_b)
            att = jnp.where(valid, att, 0.0).astype(BF16)
            o = o + _dot(att, jnp.where(lane_h == h, vv, jnp.zeros_like(vv)))
        st_ref[bi] = jnp.where(same_head, state * jnp.exp(tot) + _dot_tn(vv, k_st.astype(BF16)), 0.0)
        if reverse:
            ot = o + of_ref[bi].astype(F32)
            ms = _dot_sel_rhs(ot * ot, jnp.where(same_head, 1.0, 0.0).astype(BF16)) * (1.0 / 64.0)
            rr = x[:, 3 * w:].astype(F32)
            out_ref[bi] = (ot * lax.rsqrt(ms + NORM_EPS) * nw_ref[...] * _silu(rr)).astype(BF16)
        else:
            out_ref[bi] = o.astype(BF16)


def _gla_scan(reverse, gla_in, glr, wg_pad, g_up_b, n_ctx_chunks, extra=None):
    b, lt, _ = gla_in.shape
    nc = lt // CHUNK
    d = 1 if reverse else 0
    grp = _scan_group(b)
    cidx = functools.partial(_scan_chunk, reverse, n_ctx_chunks, nc)
    row = lambda w: pl.BlockSpec((grp, CHUNK, w), lambda i, j: (i, cidx(j), 0))
    in_specs = [row(4 * GLA_W), row(LANES),
                pl.BlockSpec((1, LANES, GLA_W), lambda i, j: (d, 0, 0)),
                pl.BlockSpec((1, 1, GLA_W), lambda i, j: (d, 0, 0))]
    args = [gla_in, glr, wg_pad, g_up_b.reshape(2, 1, GLA_W)]
    if reverse:
        of, nw = extra
        in_specs += [row(GLA_W), pl.BlockSpec((1, GLA_W), lambda i, j: (0, 0))]
        args += [of, nw]
    return pl.pallas_call(
        functools.partial(_gla_kernel, reverse, grp),
        grid=(b // grp, nc),
        in_specs=in_specs,
        out_specs=row(GLA_W),
        out_shape=jax.ShapeDtypeStruct((b, lt, GLA_W), BF16),
        scratch_shapes=[pltpu.VMEM((grp, GLA_W, GLA_W), F32)],
        compiler_params=_params("parallel", "arbitrary"),
        name="gla_scan_bwd" if reverse else "gla_scan_fwd",
    )(*args)


def _mla_prep_kernel(m_ref, qnw_ref, wq_ref, wqs_ref, kvnw_ref, wk_ref, wv_ref, cos_ref, sin_ref,
                     q_ref, k_ref, v_ref):
    m = m_ref[0]
    cq = m[:, :2 * LANES].astype(F32)
    ckv = m[:, 2 * LANES:3 * LANES].astype(F32)
    ka = m[:, 3 * LANES:4 * LANES].astype(F32)
    kb = m[:, 4 * LANES:].astype(F32)
    cos = cos_ref[...]
    sin = sin_ref[...]
    cqn = (_rms(cq, MLA_Q_RANK) * qnw_ref[...]).astype(BF16)
    qm = _dot(cqn, wq_ref[...])
    qs = _dot(cqn, wqs_ref[...])
    ckn = (_rms(ckv) * kvnw_ref[...]).astype(BF16)
    kn = _dot(ckn, wk_ref[...])
    k_rot = ka * cos + kb * sin
    v_ref[0] = _dot(ckn, wv_ref[...]).astype(BF16)
    for h in range(MLA_HEADS):
        sl = slice(h * LANES, (h + 1) * LANES)
        q_ref[0, h] = ((qm[:, sl] * cos + qs[:, sl] * sin) * (MLA_QK ** -0.5)).astype(BF16)
        k_ref[0, h] = (kn[:, sl] + k_rot).astype(BF16)


def _mla_prep(mla_in, qnw, wq, wqs, kvnw, wk, wv, cos_t, sin_t):
    b, lt, _ = mla_in.shape
    nt = lt // ROW_TILE
    full = lambda a: pl.BlockSpec(a.shape, lambda i, t: (0,) * a.ndim)
    tab = pl.BlockSpec((ROW_TILE, LANES), lambda i, t: (t, 0))
    head = pl.BlockSpec((1, MLA_HEADS, ROW_TILE, LANES), lambda i, t: (i, 0, t, 0))
    return pl.pallas_call(
        _mla_prep_kernel,
        grid=(b, nt),
        in_specs=[pl.BlockSpec((1, ROW_TILE, MLA_IN_W), lambda i, t: (i, t, 0)),
                  full(qnw), full(wq), full(wqs), full(kvnw), full(wk), full(wv), tab, tab],
        out_specs=[head, head, pl.BlockSpec((1, ROW_TILE, MLA_HEADS * MLA_V), lambda i, t: (i, t, 0))],
        out_shape=[jax.ShapeDtypeStruct((b, MLA_HEADS, lt, LANES), BF16),
                   jax.ShapeDtypeStruct((b, MLA_HEADS, lt, LANES), BF16),
                   jax.ShapeDtypeStruct((b, lt, MLA_HEADS * MLA_V), BF16)],
        compiler_params=_params("parallel", "parallel"),
        name="mla_prep",
    )(mla_in, qnw, wq, wqs, kvnw, wk, wv, cos_t, sin_t)


def _attn_kernel(n_ctx, q_ref, k_ref, v_ref, o_ref):
    def attend(n_keys):
        lane_h = jnp.right_shift(lax.broadcasted_iota(I32, (n_keys, MLA_HEADS * MLA_V), 1), 6)
        vv = v_ref[0, :n_keys, :]
        acc = jnp.zeros((ROW_TILE, MLA_HEADS * MLA_V), F32)
        for h in range(MLA_HEADS):
            s = _dot_nt(q_ref[0, h], k_ref[0, h, :n_keys, :])
            p = jnp.exp(s - jnp.max(s, axis=-1, keepdims=True))
            denom = jnp.sum(p, axis=-1, keepdims=True)
            pv = _dot(p.astype(BF16), jnp.where(lane_h == h, vv, jnp.zeros_like(vv)))
            acc = acc + pv * (1.0 / denom)
        o_ref[0] = acc.astype(BF16)

    is_ctx = pl.program_id(1) < n_ctx // ROW_TILE
    pl.when(is_ctx)(lambda: attend(n_ctx))
    pl.when(jnp.logical_not(is_ctx))(lambda: attend(k_ref.shape[2]))


def _attention(q, k, v, n_ctx):
    b, _, lt, _ = q.shape
    nt = lt // ROW_TILE
    return pl.pallas_call(
        functools.partial(_attn_kernel, n_ctx),
        grid=(b, nt),
        in_specs=[pl.BlockSpec((1, MLA_HEADS, ROW_TILE, LANES), lambda i, t: (i, 0, t, 0)),
                  pl.BlockSpec((1, MLA_HEADS, lt, LANES), lambda i, t: (i, 0, 0, 0)),
                  pl.BlockSpec((1, lt, MLA_HEADS * MLA_V), lambda i, t: (i, 0, 0))],
        out_specs=pl.BlockSpec((1, ROW_TILE, MLA_HEADS * MLA_V), lambda i, t: (i, t, 0)),
        out_shape=jax.ShapeDtypeStruct((b, lt, MLA_HEADS * MLA_V), BF16),
        compiler_params=_params("parallel", "arbitrary"),
        name="mla_attention",
    )(q, k, v)


def _outproj_kernel(ssd_ref, gla_ref, mla_ref, h_ref, mod_ref, w_ref, nw_ref, rw_ref, rb_ref,
                    h_out_ref, v8_ref, idx_ref, gate_ref):
    mod = mod_ref[0, 0]
    mix = (_dot(ssd_ref[0], w_ref[:SSD_D_INNER, :])
           + _dot(gla_ref[0], w_ref[SSD_D_INNER:SSD_D_INNER + GLA_W, :])
           + _dot(mla_ref[0], w_ref[SSD_D_INNER + GLA_W:, :]))
    hm = h_ref[0] + mod[2:3, :] * mix
    h_out_ref[0] = hm
    v = (_rms(hm) * nw_ref[...]) * (1.0 + mod[4:5, :]) + mod[3:4, :]
    for j in range(SUBLANES):
        v8_ref[pl.ds(j, ROW_TILE, stride=SUBLANES), :] = v[:, j * LANES:(j + 1) * LANES]
    logits = _dot(v.astype(BF16), rw_ref[...]) + rb_ref[...]
    lane = lax.broadcasted_iota(I32, logits.shape, 1).astype(F32)
    idxs, vals = [], []
    for _ in range(TOP_K):
        top = jnp.max(logits, axis=-1, keepdims=True)
        pick = jnp.min(jnp.where(logits == top, lane, float(LANES)), axis=-1, keepdims=True)
        idxs.append(pick)
        vals.append(top)
        logits = jnp.where(lane == pick, 2.0 * NEG, logits)
    exps = [jnp.exp(t - vals[0]) for t in vals]
    inv = 1.0 / (exps[0] + exps[1] + exps[2] + exps[3])
    idx_out = jnp.zeros(lane.shape, F32)
    gate_out = jnp.zeros(lane.shape, F32)
    for k in range(TOP_K):
        idx_out = jnp.where(lane == float(k), idxs[k], idx_out)
        gate_out = jnp.where(lane == float(k), exps[k] * inv, gate_out)
    idx_ref[0] = idx_out.astype(I32)
    gate_ref[0] = gate_out


def _out_projection(ssd_o, gla_o, mla_o, h, mod, w_out, norm_w, rw_pad, rb_pad, n_ctx_tiles):
    b, lt, _ = h.shape
    nt = lt // ROW_TILE
    row = lambda w: pl.BlockSpec((1, ROW_TILE, w), lambda i, t: (i, t, 0))
    full = lambda a: pl.BlockSpec(a.shape, lambda i, t: (0,) * a.ndim)
    mod_spec = pl.BlockSpec((1, 1, N_MOD, D_MODEL), lambda i, t: (i, jnp.where(t >= n_ctx_tiles, 1, 0), 0, 0))
    nw = norm_w.reshape(1, D_MODEL)
    return pl.pallas_call(
        _outproj_kernel,
        grid=(b, nt),
        in_specs=[row(SSD_D_INNER), row(GLA_W), row(MLA_HEADS * MLA_V), row(D_MODEL), mod_spec,
                  full(w_out), full(nw), full(rw_pad), full(rb_pad)],
        out_specs=[row(D_MODEL),
                   pl.BlockSpec((ROW_TILE * SUBLANES, LANES), lambda i, t: (i * nt + t, 0)),
                   row(LANES), row(LANES)],
        out_shape=[jax.ShapeDtypeStruct((b, lt, D_MODEL), F32),
                   jax.ShapeDtypeStruct((b * lt * SUBLANES, LANES), F32),
                   jax.ShapeDtypeStruct((b, lt, LANES), I32),
                   jax.ShapeDtypeStruct((b, lt, LANES), F32)],
        compiler_params=_params("parallel", "parallel"),
        name="out_projection",
    )(ssd_o, gla_o, mla_o, h, mod, w_out, nw, rw_pad, rb_pad)


def _expert_w1_kernel(w_ref, g_ref, l_ref):
    blk = 2 * LANES
    row = lax.broadcasted_iota(I32, (blk, LANES), 0)
    col = lax.broadcasted_iota(I32, (blk, LANES), 1)
    even = jnp.where(row == 2 * col, 1.0, 0.0).astype(BF16)
    odd = jnp.where(row == 2 * col + 1, 1.0, 0.0).astype(BF16)
    for c in range(2 * EXPERT_FF // blk):
        w = w_ref[0, :, c * blk:(c + 1) * blk].astype(BF16)
        g_ref[0, :, c * LANES:(c + 1) * LANES] = _dot(w, even).astype(BF16)
        l_ref[0, :, c * LANES:(c + 1) * LANES] = _dot(w, odd).astype(BF16)


def _expert_w1(w1):
    n_e = w1.shape[0]
    out = pl.BlockSpec((1, D_MODEL, EXPERT_FF), lambda e: (e, 0, 0))
    return pl.pallas_call(
        _expert_w1_kernel,
        grid=(n_e,),
        in_specs=[pl.BlockSpec((1, D_MODEL, 2 * EXPERT_FF), lambda e: (e, 0, 0))],
        out_specs=[out, out],
        out_shape=[jax.ShapeDtypeStruct((n_e, D_MODEL, EXPERT_FF), BF16)] * 2,
        compiler_params=_params("parallel"),
        name="expert_w1_split",
    )(w1)


def _moe_kernel(ts, off_ref, cnt_ref, end_ref, tok_ref, gate_ref, x8_ref, w1g_ref, w1l_ref, w2_ref,
                b1g_ref, b1l_ref, b2_ref, f8_ref, xa_ref, xb_ref, ya_ref, yb_ref):
    s = pl.program_id(0)
    e = pl.program_id(1)
    r = MOE_ROWS
    batch = SUBLANES

    @pl.when(e == 0)
    def _():
        f8_ref[...] = jnp.zeros_like(f8_ref)

    base0 = off_ref[s * N_EXPERTS + e]
    end = end_ref[s * N_EXPERTS + e]
    n_tiles = cnt_ref[s * N_EXPERTS + e]

    def gather(base, xg_ref):
        for rr in range(r):
            src = pl.multiple_of(tok_ref[0, 0, base + rr] * SUBLANES, SUBLANES)
            xg_ref[pl.ds(rr, SUBLANES, stride=MOE_PLANE), :] = x8_ref[pl.ds(src, SUBLANES), :]

    def ffn(xg_ref, yp_ref):
        x = jnp.concatenate([xg_ref[j * MOE_PLANE:j * MOE_PLANE + r, :] for j in range(SUBLANES)],
                            axis=1).astype(BF16)
        glu = jnp.minimum(_dot(x, w1g_ref[0]) + b1g_ref[0], SWIGLU_LIMIT)
        lin = jnp.clip(_dot(x, w1l_ref[0]) + b1l_ref[0], -SWIGLU_LIMIT, SWIGLU_LIMIT)
        sig = 0.5 * jnp.tanh((0.5 * SWIGLU_ALPHA) * glu) + 0.5
        act = (glu * sig * (lin + 1.0)).astype(BF16)
        y = _dot(act, w2_ref[0]) + b2_ref[0]
        for j in range(SUBLANES):
            yp_ref[j * MOE_PLANE:j * MOE_PLANE + r, :] = y[:, j * LANES:(j + 1) * LANES]

    def scatter(base, yp_ref):
        for r0 in range(0, r, batch):
            new = []
            for rr in range(r0, r0 + batch):
                dst = pl.multiple_of(tok_ref[0, 0, base + rr] * SUBLANES, SUBLANES)
                gate = jnp.where(base + rr < end, gate_ref[0, 0, base + rr], 0.0)
                new.append((dst, f8_ref[pl.ds(dst, SUBLANES), :]
                            + gate * yp_ref[pl.ds(rr, SUBLANES, stride=MOE_PLANE), :]))
            for dst, val in reversed(new):
                f8_ref[pl.ds(dst, SUBLANES), :] = val

    def pair(i, carry):
        base = base0 + i * (2 * r)
        gather(base, xa_ref)
        gather(base + r, xb_ref)
        ffn(xa_ref, ya_ref)
        ffn(xb_ref, yb_ref)
        scatter(base, ya_ref)
        scatter(base + r, yb_ref)
        return carry

    lax.fori_loop(0, jnp.right_shift(n_tiles, 1), pair, 0)

    @pl.when(jnp.bitwise_and(n_tiles, 1) == 1)
    def _():
        base = base0 + (n_tiles - 1) * r
        gather(base, xa_ref)
        ffn(xa_ref, ya_ref)
        scatter(base, ya_ref)


def _moe(x8, plan, w1g, w1l, w2, b1g, b1l, b2, ts):
    off, cnt, end, tok, gate = plan
    n_super = x8.shape[0] // (ts * SUBLANES)
    cap = tok.shape[-1]
    wspec = pl.BlockSpec((1, D_MODEL, EXPERT_FF), lambda s, e, *_: (e, 0, 0))
    w2spec = pl.BlockSpec((1, EXPERT_FF, D_MODEL), lambda s, e, *_: (e, 0, 0))
    bspec = pl.BlockSpec((1, 1, EXPERT_FF), lambda s, e, *_: (e, 0, 0))
    smem = pl.BlockSpec((1, 1, cap), lambda s, e, *_: (s, 0, 0), memory_space=pltpu.SMEM)
    win = pl.BlockSpec((ts * SUBLANES, LANES), lambda s, e, *_: (s, 0), pipeline_mode=pl.Buffered(1))
    grid_spec = pltpu.PrefetchScalarGridSpec(
        num_scalar_prefetch=3,
        grid=(n_super, N_EXPERTS),
        in_specs=[smem, smem, win, wspec, wspec, w2spec, bspec, bspec, bspec],
        out_specs=pl.BlockSpec((ts * SUBLANES, LANES), lambda s, e, *_: (s, 0), pipeline_mode=pl.Buffered(1)),
        scratch_shapes=[pltpu.VMEM((SUBLANES * MOE_PLANE, LANES), F32)] * 4,
    )
    return pl.pallas_call(
        functools.partial(_moe_kernel, ts),
        grid_spec=grid_spec,
        out_shape=jax.ShapeDtypeStruct(x8.shape, F32),
        compiler_params=_params("arbitrary", "arbitrary"),
        name="moe_experts",
    )(off, cnt, end, tok, gate, x8, w1g, w1l, w2, b1g, b1l, b2)


def _moe_plan(idx, gate, ts):
    n_tok = idx.shape[0]
    n_super = n_tok // ts
    n_rows = ts * TOP_K
    flat_e = idx.reshape(n_super, n_rows)
    flat_g = gate.reshape(n_super, n_rows)
    pos = jnp.arange(n_rows, dtype=I32)[None, :]
    key = flat_e * n_rows + pos
    tok_of = jnp.broadcast_to(pos // TOP_K, key.shape)
    _, tok, gat = lax.sort((key, tok_of, flat_g), dimension=1, num_keys=1)
    counts = jnp.sum(flat_e[:, :, None] == jnp.arange(N_EXPERTS, dtype=I32)[None, None, :], axis=1).astype(I32)
    end = jnp.cumsum(counts, axis=1)
    off = end - counts
    tiles = (counts + MOE_ROWS - 1) // MOE_ROWS
    tok = jnp.pad(tok, ((0, 0), (0, MOE_ROWS)))
    gat = jnp.pad(gat, ((0, 0), (0, MOE_ROWS)))
    cap = n_rows + MOE_ROWS
    return (off.reshape(-1).astype(I32), tiles.reshape(-1).astype(I32), end.reshape(-1).astype(I32),
            tok.astype(I32).reshape(n_super, 1, cap), gat.astype(F32).reshape(n_super, 1, cap))


def _final_kernel(h_ref, f8_ref, mod_ref, w_ref, o_ref):
    x = h_ref[0] + mod_ref[0, 0, 5:6, :] * _from_token_vreg(f8_ref, ROW_TILE)
    o_ref[0] = _rms(x) * w_ref[...]


def _final_norm(h, f8, mod, w, n_ctx_tiles):
    b, lt, _ = h.shape
    nt = lt // ROW_TILE
    nl = nt - n_ctx_tiles
    return pl.pallas_call(
        _final_kernel,
        grid=(b, nl),
        in_specs=[pl.BlockSpec((1, ROW_TILE, D_MODEL), lambda i, t: (i, t + n_ctx_tiles, 0)),
                  pl.BlockSpec((ROW_TILE * SUBLANES, LANES), lambda i, t: (i * nt + t + n_ctx_tiles, 0)),
                  pl.BlockSpec((1, 1, N_MOD, D_MODEL), lambda i, t: (i, 1, 0, 0)),
                  pl.BlockSpec((1, D_MODEL), lambda i, t: (0, 0))],
        out_specs=pl.BlockSpec((1, ROW_TILE, D_MODEL), lambda i, t: (i, t, 0)),
        out_shape=jax.ShapeDtypeStruct((b, nl * ROW_TILE, D_MODEL), F32),
        compiler_params=_params("parallel", "parallel"),
        name="final_norm",
    )(h, f8, mod, w.reshape(1, D_MODEL))


def _rope_partner():
    i = jnp.arange(MLA_ROPE)
    return jnp.where((i % 16) < 8, i + 8, i - 8)


def _pad_cols(w, width):
    return jnp.pad(w, ((0, 0), (0, width - w.shape[1])))


def _in_weight(w_in):
    sizes = (SSD_D_INNER, SSD_XBC, 2 * SSD_HEADS, GLA_W, GLA_W, GLA_W, GLA_W, 2 * GLA_GATE_RANK,
             MLA_Q_RANK, MLA_KV_RANK, MLA_ROPE)
    cols, acc = [], 0
    for s in sizes:
        cols.append(w_in[:, acc:acc + s])
        acc += s
    z, xbc, dt, gq, gk, gv, gr, glr, cq, ckv, kpe = cols
    rope_at = lambda w: jnp.pad(w, ((0, 0), (MLA_NOPE, LANES - MLA_NOPE - MLA_ROPE)))
    parts = [z, xbc, _pad_cols(dt, LANES), gq, gk, gv, gr, _pad_cols(glr, LANES), _pad_cols(cq, 2 * LANES), ckv,
             rope_at(kpe), rope_at(kpe[:, _rope_partner()])]
    return jnp.concatenate(parts, axis=1).astype(BF16)


def _mla_weights(w_uq, w_ukv):
    wq = w_uq.reshape(MLA_Q_RANK, MLA_HEADS, MLA_QK)
    nope, rope = wq[..., :MLA_NOPE], wq[..., MLA_NOPE:]
    zeros = jnp.zeros((MLA_Q_RANK, MLA_HEADS, LANES - MLA_QK), F32)
    main = jnp.concatenate([nope, rope, zeros], axis=-1)
    swap = jnp.concatenate([jnp.zeros_like(nope), rope[..., _rope_partner()], zeros], axis=-1)
    pad_rows = lambda w: jnp.pad(w.reshape(MLA_Q_RANK, MLA_HEADS * LANES), ((0, 2 * LANES - MLA_Q_RANK), (0, 0)))
    wkv = w_ukv.reshape(MLA_KV_RANK, MLA_HEADS, MLA_NOPE + MLA_V)
    wk = jnp.pad(wkv[..., :MLA_NOPE], ((0, 0), (0, 0), (0, LANES - MLA_NOPE))).reshape(MLA_KV_RANK, MLA_HEADS * LANES)
    wv = wkv[..., MLA_NOPE:].reshape(MLA_KV_RANK, MLA_HEADS * MLA_V)
    return pad_rows(main).astype(BF16), pad_rows(swap).astype(BF16), wk.astype(BF16), wv.astype(BF16)


def _rope_tables(n_ctx, n_lat):
    pos = jnp.arange(n_lat, dtype=F32)
    rowp = jnp.floor(pos / GRID_W)
    colp = pos - rowp * GRID_W
    half = MLA_ROPE // 2
    inv_freq = 1.0 / (ROPE_BASE ** (jnp.arange(0, half, 2, dtype=F32) / half))
    ang = jnp.stack([rowp[:, None] * inv_freq, colp[:, None] * inv_freq], axis=1)
    cos, sin = jnp.cos(ang), jnp.sin(ang)
    cos32 = jnp.concatenate([cos, cos], axis=2).reshape(n_lat, MLA_ROPE)
    sin32 = jnp.concatenate([-sin, sin], axis=2).reshape(n_lat, MLA_ROPE)
    cos32 = jnp.concatenate([jnp.ones((n_ctx, MLA_ROPE), F32), cos32], axis=0)
    sin32 = jnp.concatenate([jnp.zeros((n_ctx, MLA_ROPE), F32), sin32], axis=0)
    n = n_ctx + n_lat
    cos_t = jnp.concatenate([jnp.ones((n, MLA_NOPE), F32), cos32, jnp.zeros((n, LANES - MLA_QK), F32)], axis=1)
    sin_t = jnp.concatenate([jnp.zeros((n, MLA_NOPE), F32), sin32, jnp.zeros((n, LANES - MLA_QK), F32)], axis=1)
    return cos_t, sin_t


def kernel(x, c, ctx, c_ctx, w_mod, b_mod, norm1_w, w_in, ssd_conv_w, ssd_conv_b, ssd_a_log, ssd_dt_bias, ssd_d, ssd_norm_w, gla_g_up_w, gla_g_up_b, gla_norm_w, mla_q_norm_w, mla_w_uq, mla_kv_norm_w, mla_w_ukv, w_out, norm2_w, router_w, router_b, expert_w1, expert_b1, expert_w2, expert_b2, final_norm_w):
    b, n_lat, _ = x.shape
    n_ctx = ctx.shape[1]
    lt = n_ctx + n_lat
    depth = w_mod.shape[0]
    assert n_ctx % ROW_TILE == 0 and n_lat % ROW_TILE == 0 and n_lat % GRID_W == 0
    n_ctx_tiles = n_ctx // ROW_TILE
    n_ctx_chunks = n_ctx // CHUNK
    n_tok = b * lt
    ts = MOE_SUPER if n_tok % MOE_SUPER == 0 else n_tok
    assert b <= 16

    h = jnp.concatenate([ctx, x], axis=1)
    cvec = jnp.concatenate([c, c_ctx[None, :], jnp.zeros((24 - b - 1, D_MODEL), F32)], axis=0)
    mod_all = _modulation(cvec, w_mod, b_mod).reshape(depth, 24, N_MOD, D_MODEL)
    mods = [jnp.stack([jnp.broadcast_to(mod_all[l, b], (b, N_MOD, D_MODEL)), mod_all[l, :b]], axis=1)
            for l in range(depth)]
    cos_t, sin_t = _rope_tables(n_ctx, n_lat)

    f8 = None
    for l in range(depth):
        outs = _in_projection(h, f8, mods[l - 1] if l else None, mods[l], norm1_w[l], _in_weight(w_in[l]),
                              n_ctx_tiles)
        z, xbc, dt, gla_in, glr, mla_in = outs[:6]
        if f8 is not None:
            h = outs[6]

        a_coef = -jnp.exp(ssd_a_log[l].astype(F32))
        xbc_act = _ssd_conv(xbc, ssd_conv_w[l], ssd_conv_b[l], n_ctx_tiles)
        dtr = jnp.transpose(dt[:, :, :2 * SSD_HEADS], (0, 2, 1)).reshape(b, 2, SSD_HEADS, lt)
        yf = _ssd_scan(False, xbc_act, dt, dtr, a_coef, ssd_dt_bias[l], n_ctx_chunks)
        dsk = jnp.repeat(ssd_d[l].astype(F32), SSD_D_INNER // SSD_HEADS).reshape(1, SSD_D_INNER)
        ssd_o = _ssd_scan(True, xbc_act, dt, dtr, a_coef, ssd_dt_bias[l], n_ctx_chunks,
                          extra=(yf, z, dsk, ssd_norm_w[l].reshape(1, SSD_D_INNER)))

        wg = jnp.zeros((2, LANES, GLA_W), F32)
        for d in range(2):
            wg = wg.at[d, d * GLA_GATE_RANK:(d + 1) * GLA_GATE_RANK].set(gla_g_up_w[l, d])
        of = _gla_scan(False, gla_in, glr, wg, gla_g_up_b[l], n_ctx_chunks)
        gnw = jnp.tile(gla_norm_w[l].astype(F32), GLA_HEADS).reshape(1, GLA_W)
        gla_o = _gla_scan(True, gla_in, glr, wg, gla_g_up_b[l], n_ctx_chunks, extra=(of, gnw))

        wq, wqs, wk, wv = _mla_weights(mla_w_uq[l], mla_w_ukv[l])
        qnw = jnp.pad(mla_q_norm_w[l], (0, 2 * LANES - MLA_Q_RANK)).reshape(1, 2 * LANES)
        qh, kh, vh = _mla_prep(mla_in, qnw, wq, wqs, mla_kv_norm_w[l].reshape(1, MLA_KV_RANK), wk, wv, cos_t, sin_t)
        mla_o = _attention(qh, kh, vh, n_ctx)

        rw = _pad_cols(router_w[l], LANES).astype(BF16)
        rb = jnp.concatenate([router_b[l].astype(F32), jnp.full((LANES - N_EXPERTS,), NEG, F32)]).reshape(1, LANES)
        h, v8, idx, gate = _out_projection(ssd_o, gla_o, mla_o, h, mods[l], w_out[l].astype(BF16), norm2_w[l],
                                           rw, rb, n_ctx_tiles)

        if l == depth - 1:
            idx = jnp.where((jnp.arange(lt) < n_ctx)[None, :, None], N_EXPERTS, idx)
        plan = _moe_plan(idx.reshape(n_tok, LANES)[:, :TOP_K], gate.reshape(n_tok, LANES)[:, :TOP_K], ts)
        w1g, w1l = _expert_w1(expert_w1[l])
        f8 = _moe(v8, plan, w1g, w1l, expert_w2[l].astype(BF16),
                  expert_b1[l][:, None, 0::2], expert_b1[l][:, None, 1::2], expert_b2[l][:, None, :], ts)

    return _final_norm(h, f8, mods[depth - 1], final_norm_w, n_ctx_tiles)
```

```python
import functools

import jax
import jax.numpy as jnp
from jax import lax
from jax.experimental import pallas as pl
from jax.experimental.pallas import tpu as pltpu

F32 = jnp.float32
BF16 = jnp.bfloat16
I32 = jnp.int32

D_MODEL = 1024
N_MOD = 6
NORM_EPS = 1e-6
SSD_HEADS, SSD_GROUPS, SSD_STATE = 8, 2, 64
SSD_D_INNER, SSD_XBC = 512, 768
GLA_HEADS, GLA_W, GLA_GATE_RANK, GLA_TAU = 4, 256, 16, 16.0
MLA_HEADS, MLA_Q_RANK, MLA_KV_RANK = 4, 192, 128
MLA_NOPE, MLA_ROPE, MLA_V, MLA_QK = 64, 32, 64, 96
GRID_W, ROPE_BASE = 64, 10000.0
N_EXPERTS, TOP_K, EXPERT_FF = 32, 4, 1024
SWIGLU_ALPHA, SWIGLU_LIMIT = 1.702, 7.0

LANES = 128
SUBLANES = 8
ROW_TILE = 256
CHUNK = 128
MOE_SUPER = 4096
MOE_ROWS = 128
MOE_PLANE = MOE_ROWS + SUBLANES
VMEM_LIMIT = 56 * 1024 * 1024
NEG = -1e30

IN_Z, IN_XBC, IN_DT, IN_GLA, IN_GLR, IN_MLA = 0, 512, 1280, 1408, 2432, 2560
IN_PAD_WIDTH = 3200
MLA_IN_W = 640


def _dot(a, b):
    return jnp.dot(a, b, preferred_element_type=F32)


def _dot_nt(a, b):
    return lax.dot_general(a, b, (((1,), (1,)), ((), ())), preferred_element_type=F32)


def _dot_tn(a, b):
    return lax.dot_general(a, b, (((0,), (0,)), ((), ())), preferred_element_type=F32)


def _split(x):
    hi = x.astype(BF16)
    lo = (x - hi.astype(F32)).astype(BF16)
    return hi, lo


def _dot_sel_rhs(x, m):
    hi, lo = _split(x)
    return _dot(hi, m) + _dot(lo, m)


def _dot_sel_lhs(m, x):
    hi, lo = _split(x)
    return _dot(m, hi) + _dot(m, lo)


def _dot3(a, b):
    ah, al = _split(a)
    bh, bl = _split(b)
    return _dot(ah, bh) + _dot(ah, bl) + _dot(al, bh)


def _softplus(x):
    return jnp.maximum(x, 0.0) + jnp.log(1.0 + jnp.exp(-jnp.abs(x)))


def _silu(x):
    return x * jax.nn.sigmoid(x)


def _rms(x, n=None):
    ms = jnp.sum(x * x, axis=-1, keepdims=True) * (1.0 / (n or x.shape[-1]))
    return x * lax.rsqrt(ms + NORM_EPS)


def _params(*sem):
    return pltpu.CompilerParams(dimension_semantics=sem, vmem_limit_bytes=VMEM_LIMIT)


def _mod_kernel(c_ref, w_ref, b_ref, o_ref):
    cv = c_ref[...]
    o_ref[0] = _dot3(_silu(cv), w_ref[0]) + b_ref[0]


def _modulation(cvec, w_mod, b_mod):
    depth, _, width = w_mod.shape
    rows = cvec.shape[0]
    tn = 1536
    return pl.pallas_call(
        _mod_kernel,
        grid=(depth, width // tn),
        in_specs=[pl.BlockSpec((rows, D_MODEL), lambda l, n: (0, 0)),
                  pl.BlockSpec((1, D_MODEL, tn), lambda l, n: (l, 0, n)),
                  pl.BlockSpec((1, 1, tn), lambda l, n: (l, 0, n))],
        out_specs=pl.BlockSpec((1, rows, tn), lambda l, n: (l, 0, n)),
        out_shape=jax.ShapeDtypeStruct((depth, rows, width), F32),
        compiler_params=_params("parallel", "parallel"),
        name="modulation",
    )(cvec, w_mod, b_mod.reshape(depth, 1, width))


def _from_token_vreg(f8_ref, rows):
    return jnp.concatenate([f8_ref[pl.ds(j, rows, stride=SUBLANES), :] for j in range(SUBLANES)], axis=1)


def _inproj_kernel(has_moe, *refs):
    if has_moe:
        h_ref, f8_ref, pmod_ref, mod_ref, nw_ref, w_ref = refs[:6]
        outs = refs[6:]
    else:
        h_ref, mod_ref, nw_ref, w_ref = refs[:4]
        outs = refs[4:]
    x = h_ref[0]
    if has_moe:
        x = x + pmod_ref[0, 0, 5:6, :] * _from_token_vreg(f8_ref, ROW_TILE)
        outs[6][0] = x
    mod = mod_ref[0, 0]
    u = (_rms(x) * nw_ref[...]) * (1.0 + mod[1:2, :]) + mod[0:1, :]
    ub = u.astype(BF16)
    z_ref, xbc_ref, dt_ref, gla_ref, glr_ref, mla_ref = outs[:6]
    z_ref[0] = _dot(ub, w_ref[:, IN_Z:IN_XBC]).astype(BF16)
    xbc_ref[0] = _dot(ub, w_ref[:, IN_XBC:IN_DT]).astype(BF16)
    dt_ref[0] = _dot(ub, w_ref[:, IN_DT:IN_GLA])
    gla_ref[0] = _dot(ub, w_ref[:, IN_GLA:IN_GLR]).astype(BF16)
    glr_ref[0] = _dot(ub, w_ref[:, IN_GLR:IN_MLA])
    mla_ref[0] = _dot(ub, w_ref[:, IN_MLA:IN_PAD_WIDTH]).astype(BF16)


def _in_projection(h, f8, prev_mod, mod, norm_w, w_pad, n_ctx_tiles):
    b, lt, _ = h.shape
    nt = lt // ROW_TILE
    has_moe = f8 is not None
    row = lambda w: pl.BlockSpec((1, ROW_TILE, w), lambda i, t: (i, t, 0))
    mod_spec = pl.BlockSpec((1, 1, N_MOD, D_MODEL), lambda i, t: (i, jnp.where(t >= n_ctx_tiles, 1, 0), 0, 0))
    in_specs = [row(D_MODEL)]
    args = [h]
    if has_moe:
        in_specs += [pl.BlockSpec((ROW_TILE * SUBLANES, LANES), lambda i, t: (i * nt + t, 0)), mod_spec]
        args += [f8, prev_mod]
    in_specs += [mod_spec, pl.BlockSpec((1, D_MODEL), lambda i, t: (0, 0)),
                 pl.BlockSpec((D_MODEL, IN_PAD_WIDTH), lambda i, t: (0, 0))]
    args += [mod, norm_w.reshape(1, D_MODEL), w_pad]
    widths = [(SSD_D_INNER, BF16), (SSD_XBC, BF16), (LANES, F32), (4 * GLA_W, BF16), (LANES, F32), (MLA_IN_W, BF16)]
    if has_moe:
        widths.append((D_MODEL, F32))
    return pl.pallas_call(
        functools.partial(_inproj_kernel, has_moe),
        grid=(b, nt),
        in_specs=in_specs,
        out_specs=[row(w) for w, _ in widths],
        out_shape=[jax.ShapeDtypeStruct((b, lt, w), dt) for w, dt in widths],
        compiler_params=_params("parallel", "parallel"),
        name="in_projection",
    )(*args)


def _conv_kernel(n_ctx_tiles, prev_ref, cur_ref, next_ref, w_ref, b_ref, o_ref):
    t = pl.program_id(1)
    nt = pl.num_programs(1)
    cur = cur_ref[0].astype(F32)
    has_prev = jnp.logical_and(t != 0, t != n_ctx_tiles)
    has_next = jnp.logical_and(t != n_ctx_tiles - 1, t != nt - 1)
    prev_row = jnp.where(has_prev, prev_ref[0][SUBLANES - 1:SUBLANES, :].astype(F32), 0.0)
    next_row = jnp.where(has_next, next_ref[0][0:1, :].astype(F32), 0.0)
    row = lax.broadcasted_iota(I32, cur.shape, 0)
    before = jnp.where(row == 0, prev_row, pltpu.roll(cur, 1, 0))
    after = jnp.where(row == ROW_TILE - 1, next_row, pltpu.roll(cur, ROW_TILE - 1, 0))
    y = w_ref[0:1, :] * before + w_ref[1:2, :] * cur + w_ref[2:3, :] * after + b_ref[...]
    o_ref[0] = _silu(y).astype(BF16)


def _ssd_conv(xbc, conv_w, conv_b, n_ctx_tiles):
    b, lt, w = xbc.shape
    nt = lt // ROW_TILE
    per = ROW_TILE // SUBLANES
    last = lt // SUBLANES - 1
    return pl.pallas_call(
        functools.partial(_conv_kernel, n_ctx_tiles),
        grid=(b, nt),
        in_specs=[pl.BlockSpec((1, SUBLANES, w), lambda i, t: (i, jnp.maximum(t * per - 1, 0), 0)),
                  pl.BlockSpec((1, ROW_TILE, w), lambda i, t: (i, t, 0)),
                  pl.BlockSpec((1, SUBLANES, w), lambda i, t: (i, jnp.minimum((t + 1) * per, last), 0)),
                  pl.BlockSpec((3, w), lambda i, t: (0, 0)),
                  pl.BlockSpec((1, w), lambda i, t: (0, 0))],
        out_specs=pl.BlockSpec((1, ROW_TILE, w), lambda i, t: (i, t, 0)),
        out_shape=jax.ShapeDtypeStruct((b, lt, w), BF16),
        compiler_params=_params("parallel", "parallel"),
        name="ssd_conv",
    )(xbc, xbc, xbc, conv_w, conv_b.reshape(1, w))


def _scan_chunk(reverse, n_ctx_chunks, n_chunks, j):
    if not reverse:
        return j
    return jnp.where(j < n_ctx_chunks, n_ctx_chunks - 1 - j, n_chunks - 1 - (j - n_ctx_chunks))


def _ssd_kernel(reverse, group, *refs):
    if reverse:
        (xbc_ref, dtc_ref, dtr_ref, acr_ref, acc_ref, dbr_ref, dbc_ref,
         yf_ref, z_ref, dsk_ref, nw_ref, out_ref, st_ref) = refs
    else:
        xbc_ref, dtc_ref, dtr_ref, acr_ref, acc_ref, dbr_ref, dbc_ref, out_ref, st_ref = refs
    q = CHUNK
    d = 1 if reverse else 0

    @pl.when(pl.program_id(1) == 0)
    def _():
        st_ref[...] = jnp.zeros_like(st_ref)

    row = lax.broadcasted_iota(I32, (q, q), 0)
    col = lax.broadcasted_iota(I32, (q, q), 1)
    if not reverse:
        lmat = jnp.where(row >= col, 1.0, 0.0).astype(BF16)
        umat = jnp.where(row <= col, 1.0, 0.0).astype(BF16)
        valid = row >= col
    else:
        lmat = jnp.where(row > col, 1.0, 0.0).astype(BF16)
        umat = jnp.where(row < col, 1.0, 0.0).astype(BF16)
        valid = col >= row
    hrow = lax.broadcasted_iota(I32, (SSD_HEADS, SSD_D_INNER), 0)
    hlane = lax.broadcasted_iota(I32, (SSD_HEADS, SSD_D_INNER), 1)
    expand = jnp.where(jnp.right_shift(hlane, 6) == hrow, 1.0, 0.0).astype(BF16)
    half = SSD_D_INNER // SSD_GROUPS
    lane_h = jnp.right_shift(lax.broadcasted_iota(I32, (q, half), 1), 6)
    lane_g = jnp.right_shift(lax.broadcasted_iota(I32, (q, LANES), 1), 6)

    for bi in range(group):
        xbc = xbc_ref[bi]
        xs = xbc[:, :SSD_D_INNER]
        bm = xbc[:, SSD_D_INNER:SSD_D_INNER + LANES]
        cm = xbc[:, SSD_D_INNER + LANES:]
        dt_c = _softplus(dtc_ref[bi][:, SSD_HEADS * d:SSD_HEADS * (d + 1)] + dbr_ref[0])
        dt_r = _softplus(dtr_ref[bi, 0] + dbc_ref[0])
        a_c = dt_c * acr_ref[0]
        a_r = dt_r * acc_ref[0]
        p_c = _dot_sel_lhs(lmat, a_c)
        p_r = _dot_sel_rhs(a_r, umat)
        if not reverse:
            tot = p_c[q - 1:q, :]
            inter_c = jnp.exp(p_c)
            w_c = jnp.exp(tot - p_c) * dt_c
        else:
            tot = p_c[q - 1:q, :] + a_c[q - 1:q, :]
            inter_c = jnp.exp(tot - p_c)
            w_c = jnp.exp(p_c) * dt_c
        inter_f = _dot(inter_c.astype(BF16), expand)
        w_f = _dot(w_c.astype(BF16), expand)
        dec_f = _dot_sel_rhs(jnp.broadcast_to(jnp.exp(tot), (SUBLANES, SSD_HEADS)), expand)[0:1]
        xs_f = xs.astype(F32)
        xw = (xs_f * w_f).astype(BF16)
        for g in range(SSD_GROUPS):
            sl = slice(g * half, (g + 1) * half)
            cg = jnp.where(lane_g == g, cm, jnp.zeros_like(cm))
            bg = jnp.where(lane_g == g, bm, jnp.zeros_like(bm))
            cb = _dot_nt(cg, bm)
            state = st_ref[bi, g]
            y = _dot(cg, state.astype(BF16)) * inter_f[:, sl]
            xs_g = xs[:, sl]
            scores, x_heads = [], []
            for hh in range(SSD_HEADS // SSD_GROUPS):
                h = g * (SSD_HEADS // SSD_GROUPS) + hh
                if not reverse:
                    seg = p_c[:, h:h + 1] - p_r[h:h + 1, :]
                else:
                    seg = p_r[h:h + 1, :] - p_c[:, h:h + 1]
                decay = jnp.exp(jnp.where(valid, seg, NEG))
                scores.append((cb * decay * dt_r[h:h + 1, :]).astype(BF16))
                x_heads.append(jnp.where(lane_h == hh, xs_g, jnp.zeros_like(xs_g)))
            y = y + _dot(jnp.concatenate(scores, axis=1), jnp.concatenate(x_heads, axis=0))
            st_ref[bi, g] = state * dec_f[:, sl] + _dot_tn(bg, xw[:, sl])
            if reverse:
                ytot = y + yf_ref[bi][:, sl].astype(F32) + dsk_ref[:, sl] * xs_f[:, sl]
                gated = ytot * _silu(z_ref[bi][:, sl].astype(F32))
                out_ref[bi, :, sl] = (_rms(gated) * nw_ref[:, sl]).astype(BF16)
            else:
                out_ref[bi, :, sl] = y.astype(BF16)


def _scan_group(b):
    return next(g for g in (4, 2, 1) if b % g == 0)


def _ssd_scan(reverse, xbc_act, dtc, dtr, a_coef, dt_bias, n_ctx_chunks, extra=None):
    b, lt, _ = xbc_act.shape
    nc = lt // CHUNK
    d = 1 if reverse else 0
    grp = _scan_group(b)
    cidx = functools.partial(_scan_chunk, reverse, n_ctx_chunks, nc)
    row = lambda w: pl.BlockSpec((grp, CHUNK, w), lambda i, j: (i, cidx(j), 0))
    small_r = pl.BlockSpec((1, 1, SSD_HEADS), lambda i, j: (d, 0, 0))
    small_c = pl.BlockSpec((1, SSD_HEADS, 1), lambda i, j: (d, 0, 0))
    in_specs = [row(SSD_XBC), row(LANES),
                pl.BlockSpec((grp, 1, SSD_HEADS, CHUNK), lambda i, j: (i, d, 0, cidx(j))),
                small_r, small_c, small_r, small_c]
    args = [xbc_act, dtc, dtr, a_coef.reshape(2, 1, SSD_HEADS), a_coef.reshape(2, SSD_HEADS, 1),
            dt_bias.reshape(2, 1, SSD_HEADS), dt_bias.reshape(2, SSD_HEADS, 1)]
    if reverse:
        yf, z, dsk, nw = extra
        vec = pl.BlockSpec((1, SSD_D_INNER), lambda i, j: (0, 0))
        in_specs += [row(SSD_D_INNER), row(SSD_D_INNER), vec, vec]
        args += [yf, z, dsk, nw]
    return pl.pallas_call(
        functools.partial(_ssd_kernel, reverse, grp),
        grid=(b // grp, nc),
        in_specs=in_specs,
        out_specs=row(SSD_D_INNER),
        out_shape=jax.ShapeDtypeStruct((b, lt, SSD_D_INNER), BF16),
        scratch_shapes=[pltpu.VMEM((grp, SSD_GROUPS, LANES, SSD_D_INNER // SSD_GROUPS), F32)],
        compiler_params=_params("parallel", "arbitrary"),
        name="ssd_scan_bwd" if reverse else "ssd_scan_fwd",
    )(*args)


def _gla_kernel(reverse, group, *refs):
    if reverse:
        gla_ref, glr_ref, wg_ref, bg_ref, of_ref, nw_ref, out_ref, st_ref = refs
    else:
        gla_ref, glr_ref, wg_ref, bg_ref, out_ref, st_ref = refs
    q = CHUNK
    w = GLA_W

    @pl.when(pl.program_id(1) == 0)
    def _():
        st_ref[...] = jnp.zeros_like(st_ref)

    row = lax.broadcasted_iota(I32, (q, q), 0)
    col = lax.broadcasted_iota(I32, (q, q), 1)
    if not reverse:
        cmat = jnp.where(row >= col, 1.0, 0.0).astype(BF16)
        valid = row >= col
    else:
        cmat = jnp.where(row > col, 1.0, 0.0).astype(BF16)
        valid = col >= row
    lane_h = jnp.right_shift(lax.broadcasted_iota(I32, (q, w), 1), 6)
    r2 = jnp.right_shift(lax.broadcasted_iota(I32, (w, w), 0), 6)
    c2 = jnp.right_shift(lax.broadcasted_iota(I32, (w, w), 1), 6)
    same_head = r2 == c2

    for bi in range(group):
        x = gla_ref[bi]
        qq = x[:, :w].astype(F32) * (64.0 ** -0.5)
        kk = x[:, w:2 * w].astype(F32)
        vv = x[:, 2 * w:3 * w]
        gpre = _dot3(glr_ref[bi], wg_ref[0]) + bg_ref[0]
        g = (jnp.minimum(gpre, 0.0) - jnp.log(1.0 + jnp.exp(-jnp.abs(gpre)))) * (1.0 / GLA_TAU)
        gc = _dot_sel_lhs(cmat, g)
        if not reverse:
            tot = gc[q - 1:q, :]
            q_dec = qq * jnp.exp(gc)
            k_inv = kk * jnp.exp(-gc)
            k_st = kk * jnp.exp(tot - gc)
        else:
            tot = gc[q - 1:q, :] + g[q - 1:q, :]
            q_dec = qq * jnp.exp(tot - gc)
            k_inv = kk * jnp.exp(gc - tot)
            k_st = kk * jnp.exp(gc)
        state = st_ref[bi]
        qd_b = q_dec.astype(BF16)
        ki_b = k_inv.astype(BF16)
        atts, v_heads = [], []
        for h in range(GLA_HEADS):
            att = _dot_nt(jnp.where(lane_h == h, qd_b, jnp.zeros_like(qd_b)), ki_b)
            atts.append(jnp.where(valid, att, 0.0).astype(BF16))
            v_heads.append(jnp.where(lane_h == h, vv, jnp.zeros_like(vv)))
        o = (_dot_nt(qd_b, state.astype(BF16))
             + _dot(jnp.concatenate(atts, axis=1), jnp.concatenate(v_heads, axis=0)))
        st_ref[bi] = jnp.where(same_head, state * jnp.exp(tot) + _dot_tn(vv, k_st.astype(BF16)), 0.0)
        if reverse:
            ot = o + of_ref[bi].astype(F32)
            ms = _dot_sel_rhs(ot * ot, jnp.where(same_head, 1.0, 0.0).astype(BF16)) * (1.0 / 64.0)
            rr = x[:, 3 * w:].astype(F32)
            out_ref[bi] = (ot * lax.rsqrt(ms + NORM_EPS) * nw_ref[...] * _silu(rr)).astype(BF16)
        else:
            out_ref[bi] = o.astype(BF16)


def _gla_scan(reverse, gla_in, glr, wg_pad, g_up_b, n_ctx_chunks, extra=None):
    b, lt, _ = gla_in.shape
    nc = lt // CHUNK
    d = 1 if reverse else 0
    grp = _scan_group(b)
    cidx = functools.partial(_scan_chunk, reverse, n_ctx_chunks, nc)
    row = lambda w: pl.BlockSpec((grp, CHUNK, w), lambda i, j: (i, cidx(j), 0))
    in_specs = [row(4 * GLA_W), row(LANES),
                pl.BlockSpec((1, LANES, GLA_W), lambda i, j: (d, 0, 0)),
                pl.BlockSpec((1, 1, GLA_W), lambda i, j: (d, 0, 0))]
    args = [gla_in, glr, wg_pad, g_up_b.reshape(2, 1, GLA_W)]
    if reverse:
        of, nw = extra
        in_specs += [row(GLA_W), pl.BlockSpec((1, GLA_W), lambda i, j: (0, 0))]
        args += [of, nw]
    return pl.pallas_call(
        functools.partial(_gla_kernel, reverse, grp),
        grid=(b // grp, nc),
        in_specs=in_specs,
        out_specs=row(GLA_W),
        out_shape=jax.ShapeDtypeStruct((b, lt, GLA_W), BF16),
        scratch_shapes=[pltpu.VMEM((grp, GLA_W, GLA_W), F32)],
        compiler_params=_params("parallel", "arbitrary"),
        name="gla_scan_bwd" if reverse else "gla_scan_fwd",
    )(*args)


def _mla_prep_kernel(m_ref, qnw_ref, wq_ref, wqs_ref, kvnw_ref, wk_ref, wv_ref, cos_ref, sin_ref,
                     q_ref, k_ref, v_ref):
    m = m_ref[0]
    cq = m[:, :2 * LANES].astype(F32)
    ckv = m[:, 2 * LANES:3 * LANES].astype(F32)
    ka = m[:, 3 * LANES:4 * LANES].astype(F32)
    kb = m[:, 4 * LANES:].astype(F32)
    cos = cos_ref[...]
    sin = sin_ref[...]
    cqn = (_rms(cq, MLA_Q_RANK) * qnw_ref[...]).astype(BF16)
    qm = _dot(cqn, wq_ref[...])
    qs = _dot(cqn, wqs_ref[...])
    ckn = (_rms(ckv) * kvnw_ref[...]).astype(BF16)
    kn = _dot(ckn, wk_ref[...])
    k_rot = ka * cos + kb * sin
    v_ref[0] = _dot(ckn, wv_ref[...]).astype(BF16)
    for h in range(MLA_HEADS):
        sl = slice(h * LANES, (h + 1) * LANES)
        q_ref[0, h] = ((qm[:, sl] * cos + qs[:, sl] * sin) * (MLA_QK ** -0.5)).astype(BF16)
        k_ref[0, h] = (kn[:, sl] + k_rot).astype(BF16)


def _mla_prep(mla_in, qnw, wq, wqs, kvnw, wk, wv, cos_t, sin_t):
    b, lt, _ = mla_in.shape
    nt = lt // ROW_TILE
    full = lambda a: pl.BlockSpec(a.shape, lambda i, t: (0,) * a.ndim)
    tab = pl.BlockSpec((ROW_TILE, LANES), lambda i, t: (t, 0))
    head = pl.BlockSpec((1, MLA_HEADS, ROW_TILE, LANES), lambda i, t: (i, 0, t, 0))
    return pl.pallas_call(
        _mla_prep_kernel,
        grid=(b, nt),
        in_specs=[pl.BlockSpec((1, ROW_TILE, MLA_IN_W), lambda i, t: (i, t, 0)),
                  full(qnw), full(wq), full(wqs), full(kvnw), full(wk), full(wv), tab, tab],
        out_specs=[head, head, pl.BlockSpec((1, ROW_TILE, MLA_HEADS * MLA_V), lambda i, t: (i, t, 0))],
        out_shape=[jax.ShapeDtypeStruct((b, MLA_HEADS, lt, LANES), BF16),
                   jax.ShapeDtypeStruct((b, MLA_HEADS, lt, LANES), BF16),
                   jax.ShapeDtypeStruct((b, lt, MLA_HEADS * MLA_V), BF16)],
        compiler_params=_params("parallel", "parallel"),
        name="mla_prep",
    )(mla_in, qnw, wq, wqs, kvnw, wk, wv, cos_t, sin_t)


def _attn_kernel(n_ctx, q_ref, k_ref, v_ref, o_ref):
    def attend(n_keys):
        width = MLA_HEADS * MLA_V
        lane_k = lax.broadcasted_iota(I32, (n_keys, width), 1)
        lane_q = lax.broadcasted_iota(I32, (ROW_TILE, width), 1)
        vv = v_ref[0, :n_keys, :]
        acc = jnp.zeros((ROW_TILE, width), F32)
        for h in range(MLA_HEADS):
            s = _dot_nt(q_ref[0, h], k_ref[0, h, :n_keys, :])
            p = jnp.exp((s - jnp.max(s, axis=-1, keepdims=True)).astype(BF16))
            ones_at = ((h + 1) % MLA_HEADS) * MLA_V
            vh = jnp.where(jnp.right_shift(lane_k, 6) == h, vv,
                           jnp.where(lane_k == ones_at, 1.0, 0.0).astype(vv.dtype))
            pv = _dot(p, vh)
            denom = pv[:, ones_at:ones_at + 1]
            acc = acc + jnp.where(jnp.right_shift(lane_q, 6) == h, pv * (1.0 / denom), 0.0)
        o_ref[0] = acc.astype(BF16)

    is_ctx = pl.program_id(1) < n_ctx // ROW_TILE
    pl.when(is_ctx)(lambda: attend(n_ctx))
    pl.when(jnp.logical_not(is_ctx))(lambda: attend(k_ref.shape[2]))


def _attention(q, k, v, n_ctx):
    b, _, lt, _ = q.shape
    nt = lt // ROW_TILE
    return pl.pallas_call(
        functools.partial(_attn_kernel, n_ctx),
        grid=(b, nt),
        in_specs=[pl.BlockSpec((1, MLA_HEADS, ROW_TILE, LANES), lambda i, t: (i, 0, t, 0)),
                  pl.BlockSpec((1, MLA_HEADS, lt, LANES), lambda i, t: (i, 0, 0, 0)),
                  pl.BlockSpec((1, lt, MLA_HEADS * MLA_V), lambda i, t: (i, 0, 0))],
        out_specs=pl.BlockSpec((1, ROW_TILE, MLA_HEADS * MLA_V), lambda i, t: (i, t, 0)),
        out_shape=jax.ShapeDtypeStruct((b, lt, MLA_HEADS * MLA_V), BF16),
        compiler_params=_params("parallel", "arbitrary"),
        name="mla_attention",
    )(q, k, v)


def _outproj_kernel(ssd_ref, gla_ref, mla_ref, h_ref, mod_ref, w_ref, nw_ref, rw_ref, rb_ref,
                    h_out_ref, v8_ref, idx_ref, gate_ref):
    mod = mod_ref[0, 0]
    mix = (_dot(ssd_ref[0], w_ref[:SSD_D_INNER, :])
           + _dot(gla_ref[0], w_ref[SSD_D_INNER:SSD_D_INNER + GLA_W, :])
           + _dot(mla_ref[0], w_ref[SSD_D_INNER + GLA_W:, :]))
    hm = h_ref[0] + mod[2:3, :] * mix
    h_out_ref[0] = hm
    v = (_rms(hm) * nw_ref[...]) * (1.0 + mod[4:5, :]) + mod[3:4, :]
    for j in range(SUBLANES):
        v8_ref[pl.ds(j, ROW_TILE, stride=SUBLANES), :] = v[:, j * LANES:(j + 1) * LANES]
    logits = _dot(v.astype(BF16), rw_ref[...]) + rb_ref[...]
    lane = lax.broadcasted_iota(I32, logits.shape, 1).astype(F32)
    idxs, vals = [], []
    for _ in range(TOP_K):
        top = jnp.max(logits, axis=-1, keepdims=True)
        pick = jnp.min(jnp.where(logits == top, lane, float(LANES)), axis=-1, keepdims=True)
        idxs.append(pick)
        vals.append(top)
        logits = jnp.where(lane == pick, 2.0 * NEG, logits)
    exps = [jnp.exp(t - vals[0]) for t in vals]
    inv = 1.0 / (exps[0] + exps[1] + exps[2] + exps[3])
    idx_out = jnp.zeros(lane.shape, F32)
    gate_out = jnp.zeros(lane.shape, F32)
    for k in range(TOP_K):
        idx_out = jnp.where(lane == float(k), idxs[k], idx_out)
        gate_out = jnp.where(lane == float(k), exps[k] * inv, gate_out)
    idx_ref[0] = idx_out.astype(I32)
    gate_ref[0] = gate_out


def _out_projection(ssd_o, gla_o, mla_o, h, mod, w_out, norm_w, rw_pad, rb_pad, n_ctx_tiles):
    b, lt, _ = h.shape
    nt = lt // ROW_TILE
    row = lambda w: pl.BlockSpec((1, ROW_TILE, w), lambda i, t: (i, t, 0))
    full = lambda a: pl.BlockSpec(a.shape, lambda i, t: (0,) * a.ndim)
    mod_spec = pl.BlockSpec((1, 1, N_MOD, D_MODEL), lambda i, t: (i, jnp.where(t >= n_ctx_tiles, 1, 0), 0, 0))
    nw = norm_w.reshape(1, D_MODEL)
    return pl.pallas_call(
        _outproj_kernel,
        grid=(b, nt),
        in_specs=[row(SSD_D_INNER), row(GLA_W), row(MLA_HEADS * MLA_V), row(D_MODEL), mod_spec,
                  full(w_out), full(nw), full(rw_pad), full(rb_pad)],
        out_specs=[row(D_MODEL),
                   pl.BlockSpec((ROW_TILE * SUBLANES, LANES), lambda i, t: (i * nt + t, 0)),
                   row(LANES), row(LANES)],
        out_shape=[jax.ShapeDtypeStruct((b, lt, D_MODEL), F32),
                   jax.ShapeDtypeStruct((b * lt * SUBLANES, LANES), F32),
                   jax.ShapeDtypeStruct((b, lt, LANES), I32),
                   jax.ShapeDtypeStruct((b, lt, LANES), F32)],
        compiler_params=_params("parallel", "parallel"),
        name="out_projection",
    )(ssd_o, gla_o, mla_o, h, mod, w_out, nw, rw_pad, rb_pad)


def _expert_w1_kernel(w_ref, g_ref, l_ref):
    blk = 2 * LANES
    row = lax.broadcasted_iota(I32, (blk, LANES), 0)
    col = lax.broadcasted_iota(I32, (blk, LANES), 1)
    even = jnp.where(row == 2 * col, 1.0, 0.0).astype(BF16)
    odd = jnp.where(row == 2 * col + 1, 1.0, 0.0).astype(BF16)
    for c in range(2 * EXPERT_FF // blk):
        w = w_ref[0, :, c * blk:(c + 1) * blk].astype(BF16)
        g_ref[0, :, c * LANES:(c + 1) * LANES] = _dot(w, even).astype(BF16)
        l_ref[0, :, c * LANES:(c + 1) * LANES] = _dot(w, odd).astype(BF16)


def _expert_w1(w1):
    n_e = w1.shape[0]
    out = pl.BlockSpec((1, D_MODEL, EXPERT_FF), lambda e: (e, 0, 0))
    return pl.pallas_call(
        _expert_w1_kernel,
        grid=(n_e,),
        in_specs=[pl.BlockSpec((1, D_MODEL, 2 * EXPERT_FF), lambda e: (e, 0, 0))],
        out_specs=[out, out],
        out_shape=[jax.ShapeDtypeStruct((n_e, D_MODEL, EXPERT_FF), BF16)] * 2,
        compiler_params=_params("parallel"),
        name="expert_w1_split",
    )(w1)


def _moe_kernel(ts, off_ref, cnt_ref, end_ref, row_ref, gate_ref, x8_ref, w1g_ref, w1l_ref, w2_ref,
                b1g_ref, b1l_ref, b2_ref, f8_ref, xa_ref, xb_ref, ya_ref, yb_ref):
    s = pl.program_id(0)
    e = pl.program_id(1)
    r = MOE_ROWS
    batch = SUBLANES

    @pl.when(e == 0)
    def _():
        f8_ref[...] = jnp.zeros_like(f8_ref)

    base0 = off_ref[s * N_EXPERTS + e]
    end = end_ref[s * N_EXPERTS + e]
    n_tiles = cnt_ref[s * N_EXPERTS + e]

    def token_tile(row):
        return pl.multiple_of(jnp.right_shift(row, TOP_K.bit_length() - 1) * SUBLANES, SUBLANES)

    def gather(base, xg_ref):
        for rr in range(r):
            src = token_tile(row_ref[0, 0, base + rr])
            xg_ref[pl.ds(rr, SUBLANES, stride=MOE_PLANE), :] = x8_ref[pl.ds(src, SUBLANES), :]

    def ffn(xg_ref, yp_ref):
        x = jnp.concatenate([xg_ref[j * MOE_PLANE:j * MOE_PLANE + r, :] for j in range(SUBLANES)],
                            axis=1).astype(BF16)
        glu = jnp.minimum(_dot(x, w1g_ref[0]) + b1g_ref[0], SWIGLU_LIMIT)
        lin = jnp.clip(_dot(x, w1l_ref[0]) + b1l_ref[0], -SWIGLU_LIMIT, SWIGLU_LIMIT)
        sig = 0.5 * jnp.tanh((0.5 * SWIGLU_ALPHA) * glu) + 0.5
        act = (glu * sig * (lin + 1.0)).astype(BF16)
        y = _dot(act, w2_ref[0]) + b2_ref[0]
        for j in range(SUBLANES):
            yp_ref[j * MOE_PLANE:j * MOE_PLANE + r, :] = y[:, j * LANES:(j + 1) * LANES]

    def scatter(base, yp_ref):
        for r0 in range(0, r, batch):
            new = []
            for rr in range(r0, r0 + batch):
                row = row_ref[0, 0, base + rr]
                dst = token_tile(row)
                gate = jnp.where(base + rr < end, gate_ref[0, 0, row], 0.0)
                new.append((dst, f8_ref[pl.ds(dst, SUBLANES), :]
                            + gate * yp_ref[pl.ds(rr, SUBLANES, stride=MOE_PLANE), :]))
            for dst, val in reversed(new):
                f8_ref[pl.ds(dst, SUBLANES), :] = val

    def pair(i, carry):
        base = base0 + i * (2 * r)
        gather(base, xa_ref)
        gather(base + r, xb_ref)
        ffn(xa_ref, ya_ref)
        ffn(xb_ref, yb_ref)
        scatter(base, ya_ref)
        scatter(base + r, yb_ref)
        return carry

    lax.fori_loop(0, jnp.right_shift(n_tiles, 1), pair, 0)

    @pl.when(jnp.bitwise_and(n_tiles, 1) == 1)
    def _():
        base = base0 + (n_tiles - 1) * r
        gather(base, xa_ref)
        ffn(xa_ref, ya_ref)
        scatter(base, ya_ref)


def _moe(x8, plan, w1g, w1l, w2, b1g, b1l, b2, ts):
    off, cnt, end, rows, gate = plan
    n_super = x8.shape[0] // (ts * SUBLANES)
    wspec = pl.BlockSpec((1, D_MODEL, EXPERT_FF), lambda s, e, *_: (e, 0, 0))
    w2spec = pl.BlockSpec((1, EXPERT_FF, D_MODEL), lambda s, e, *_: (e, 0, 0))
    bspec = pl.BlockSpec((1, 1, EXPERT_FF), lambda s, e, *_: (e, 0, 0))
    smem = lambda a: pl.BlockSpec((1, 1, a.shape[-1]), lambda s, e, *_: (s, 0, 0), memory_space=pltpu.SMEM)
    win = pl.BlockSpec((ts * SUBLANES, LANES), lambda s, e, *_: (s, 0), pipeline_mode=pl.Buffered(1))
    grid_spec = pltpu.PrefetchScalarGridSpec(
        num_scalar_prefetch=3,
        grid=(n_super, N_EXPERTS),
        in_specs=[smem(rows), smem(gate), win, wspec, wspec, w2spec, bspec, bspec, bspec],
        out_specs=pl.BlockSpec((ts * SUBLANES, LANES), lambda s, e, *_: (s, 0), pipeline_mode=pl.Buffered(1)),
        scratch_shapes=[pltpu.VMEM((SUBLANES * MOE_PLANE, LANES), F32)] * 4,
    )
    return pl.pallas_call(
        functools.partial(_moe_kernel, ts),
        grid_spec=grid_spec,
        out_shape=jax.ShapeDtypeStruct(x8.shape, F32),
        compiler_params=_params("arbitrary", "arbitrary"),
        name="moe_experts",
    )(off, cnt, end, rows, gate, x8, w1g, w1l, w2, b1g, b1l, b2)


def _moe_plan(idx, gate, ts):
    n_tok = idx.shape[0]
    n_super = n_tok // ts
    n_rows = ts * TOP_K
    flat_e = idx.reshape(n_super, n_rows)
    key = lax.sort(flat_e * n_rows + jnp.arange(n_rows, dtype=I32)[None, :], dimension=1)
    bounds = jnp.arange(N_EXPERTS + 1, dtype=I32) * n_rows
    first = jax.vmap(lambda k: jnp.searchsorted(k, bounds))(key).astype(I32)
    off, end = first[:, :-1], first[:, 1:]
    tiles = (end - off + MOE_ROWS - 1) // MOE_ROWS
    rows = jnp.pad(key % n_rows, ((0, 0), (0, MOE_ROWS)))
    return (off.reshape(-1), tiles.reshape(-1), end.reshape(-1),
            rows.astype(I32).reshape(n_super, 1, n_rows + MOE_ROWS),
            gate.astype(F32).reshape(n_super, 1, n_rows))


def _final_kernel(h_ref, f8_ref, mod_ref, w_ref, o_ref):
    x = h_ref[0] + mod_ref[0, 0, 5:6, :] * _from_token_vreg(f8_ref, ROW_TILE)
    o_ref[0] = _rms(x) * w_ref[...]


def _final_norm(h, f8, mod, w, n_ctx_tiles):
    b, lt, _ = h.shape
    nt = lt // ROW_TILE
    nl = nt - n_ctx_tiles
    return pl.pallas_call(
        _final_kernel,
        grid=(b, nl),
        in_specs=[pl.BlockSpec((1, ROW_TILE, D_MODEL), lambda i, t: (i, t + n_ctx_tiles, 0)),
                  pl.BlockSpec((ROW_TILE * SUBLANES, LANES), lambda i, t: (i * nt + t + n_ctx_tiles, 0)),
                  pl.BlockSpec((1, 1, N_MOD, D_MODEL), lambda i, t: (i, 1, 0, 0)),
                  pl.BlockSpec((1, D_MODEL), lambda i, t: (0, 0))],
        out_specs=pl.BlockSpec((1, ROW_TILE, D_MODEL), lambda i, t: (i, t, 0)),
        out_shape=jax.ShapeDtypeStruct((b, nl * ROW_TILE, D_MODEL), F32),
        compiler_params=_params("parallel", "parallel"),
        name="final_norm",
    )(h, f8, mod, w.reshape(1, D_MODEL))


def _rope_partner():
    i = jnp.arange(MLA_ROPE)
    return jnp.where((i % 16) < 8, i + 8, i - 8)


def _pad_cols(w, width):
    return jnp.pad(w, ((0, 0), (0, width - w.shape[1])))


def _in_weight(w_in):
    sizes = (SSD_D_INNER, SSD_XBC, 2 * SSD_HEADS, GLA_W, GLA_W, GLA_W, GLA_W, 2 * GLA_GATE_RANK,
             MLA_Q_RANK, MLA_KV_RANK, MLA_ROPE)
    cols, acc = [], 0
    for s in sizes:
        cols.append(w_in[:, acc:acc + s])
        acc += s
    z, xbc, dt, gq, gk, gv, gr, glr, cq, ckv, kpe = cols
    rope_at = lambda w: jnp.pad(w, ((0, 0), (MLA_NOPE, LANES - MLA_NOPE - MLA_ROPE)))
    parts = [z, xbc, _pad_cols(dt, LANES), gq, gk, gv, gr, _pad_cols(glr, LANES), _pad_cols(cq, 2 * LANES), ckv,
             rope_at(kpe), rope_at(kpe[:, _rope_partner()])]
    return jnp.concatenate(parts, axis=1).astype(BF16)


def _mla_weights(w_uq, w_ukv):
    wq = w_uq.reshape(MLA_Q_RANK, MLA_HEADS, MLA_QK)
    nope, rope = wq[..., :MLA_NOPE], wq[..., MLA_NOPE:]
    zeros = jnp.zeros((MLA_Q_RANK, MLA_HEADS, LANES - MLA_QK), F32)
    main = jnp.concatenate([nope, rope, zeros], axis=-1)
    swap = jnp.concatenate([jnp.zeros_like(nope), rope[..., _rope_partner()], zeros], axis=-1)
    pad_rows = lambda w: jnp.pad(w.reshape(MLA_Q_RANK, MLA_HEADS * LANES), ((0, 2 * LANES - MLA_Q_RANK), (0, 0)))
    wkv = w_ukv.reshape(MLA_KV_RANK, MLA_HEADS, MLA_NOPE + MLA_V)
    wk = jnp.pad(wkv[..., :MLA_NOPE], ((0, 0), (0, 0), (0, LANES - MLA_NOPE))).reshape(MLA_KV_RANK, MLA_HEADS * LANES)
    wv = wkv[..., MLA_NOPE:].reshape(MLA_KV_RANK, MLA_HEADS * MLA_V)
    return pad_rows(main).astype(BF16), pad_rows(swap).astype(BF16), wk.astype(BF16), wv.astype(BF16)


def _rope_tables(n_ctx, n_lat):
    pos = jnp.arange(n_lat, dtype=F32)
    rowp = jnp.floor(pos / GRID_W)
    colp = pos - rowp * GRID_W
    half = MLA_ROPE // 2
    inv_freq = 1.0 / (ROPE_BASE ** (jnp.arange(0, half, 2, dtype=F32) / half))
    ang = jnp.stack([rowp[:, None] * inv_freq, colp[:, None] * inv_freq], axis=1)
    cos, sin = jnp.cos(ang), jnp.sin(ang)
    cos32 = jnp.concatenate([cos, cos], axis=2).reshape(n_lat, MLA_ROPE)
    sin32 = jnp.concatenate([-sin, sin], axis=2).reshape(n_lat, MLA_ROPE)
    cos32 = jnp.concatenate([jnp.ones((n_ctx, MLA_ROPE), F32), cos32], axis=0)
    sin32 = jnp.concatenate([jnp.zeros((n_ctx, MLA_ROPE), F32), sin32], axis=0)
    n = n_ctx + n_lat
    cos_t = jnp.concatenate([jnp.ones((n, MLA_NOPE), F32), cos32, jnp.zeros((n, LANES - MLA_QK), F32)], axis=1)
    sin_t = jnp.concatenate([jnp.zeros((n, MLA_NOPE), F32), sin32, jnp.zeros((n, LANES - MLA_QK), F32)], axis=1)
    return cos_t, sin_t


def kernel(x, c, ctx, c_ctx, w_mod, b_mod, norm1_w, w_in, ssd_conv_w, ssd_conv_b, ssd_a_log, ssd_dt_bias, ssd_d, ssd_norm_w, gla_g_up_w, gla_g_up_b, gla_norm_w, mla_q_norm_w, mla_w_uq, mla_kv_norm_w, mla_w_ukv, w_out, norm2_w, router_w, router_b, expert_w1, expert_b1, expert_w2, expert_b2, final_norm_w):
    b, n_lat, _ = x.shape
    n_ctx = ctx.shape[1]
    lt = n_ctx + n_lat
    depth = w_mod.shape[0]
    assert n_ctx % ROW_TILE == 0 and n_lat % ROW_TILE == 0 and n_lat % GRID_W == 0
    n_ctx_tiles = n_ctx // ROW_TILE
    n_ctx_chunks = n_ctx // CHUNK
    n_tok = b * lt
    ts = MOE_SUPER if n_tok % MOE_SUPER == 0 else n_tok
    assert b <= 16

    h = jnp.concatenate([ctx, x], axis=1)
    cvec = jnp.concatenate([c, c_ctx[None, :], jnp.zeros((24 - b - 1, D_MODEL), F32)], axis=0)
    mod_all = _modulation(cvec, w_mod, b_mod).reshape(depth, 24, N_MOD, D_MODEL)
    mods = [jnp.stack([jnp.broadcast_to(mod_all[l, b], (b, N_MOD, D_MODEL)), mod_all[l, :b]], axis=1)
            for l in range(depth)]
    cos_t, sin_t = _rope_tables(n_ctx, n_lat)

    f8 = None
    for l in range(depth):
        outs = _in_projection(h, f8, mods[l - 1] if l else None, mods[l], norm1_w[l], _in_weight(w_in[l]),
                              n_ctx_tiles)
        z, xbc, dt, gla_in, glr, mla_in = outs[:6]
        if f8 is not None:
            h = outs[6]

        a_coef = -jnp.exp(ssd_a_log[l].astype(F32))
        xbc_act = _ssd_conv(xbc, ssd_conv_w[l], ssd_conv_b[l], n_ctx_tiles)
        dtr = jnp.transpose(dt[:, :, :2 * SSD_HEADS], (0, 2, 1)).reshape(b, 2, SSD_HEADS, lt)
        yf = _ssd_scan(False, xbc_act, dt, dtr, a_coef, ssd_dt_bias[l], n_ctx_chunks)
        dsk = jnp.repeat(ssd_d[l].astype(F32), SSD_D_INNER // SSD_HEADS).reshape(1, SSD_D_INNER)
        ssd_o = _ssd_scan(True, xbc_act, dt, dtr, a_coef, ssd_dt_bias[l], n_ctx_chunks,
                          extra=(yf, z, dsk, ssd_norm_w[l].reshape(1, SSD_D_INNER)))

        wg = jnp.zeros((2, LANES, GLA_W), F32)
        for d in range(2):
            wg = wg.at[d, d * GLA_GATE_RANK:(d + 1) * GLA_GATE_RANK].set(gla_g_up_w[l, d])
        of = _gla_scan(False, gla_in, glr, wg, gla_g_up_b[l], n_ctx_chunks)
        gnw = jnp.tile(gla_norm_w[l].astype(F32), GLA_HEADS).reshape(1, GLA_W)
        gla_o = _gla_scan(True, gla_in, glr, wg, gla_g_up_b[l], n_ctx_chunks, extra=(of, gnw))

        wq, wqs, wk, wv = _mla_weights(mla_w_uq[l], mla_w_ukv[l])
        qnw = jnp.pad(mla_q_norm_w[l], (0, 2 * LANES - MLA_Q_RANK)).reshape(1, 2 * LANES)
        qh, kh, vh = _mla_prep(mla_in, qnw, wq, wqs, mla_kv_norm_w[l].reshape(1, MLA_KV_RANK), wk, wv, cos_t, sin_t)
        mla_o = _attention(qh, kh, vh, n_ctx)

        rw = _pad_cols(router_w[l], LANES).astype(BF16)
        rb = jnp.concatenate([router_b[l].astype(F32), jnp.full((LANES - N_EXPERTS,), NEG, F32)]).reshape(1, LANES)
        h, v8, idx, gate = _out_projection(ssd_o, gla_o, mla_o, h, mods[l], w_out[l].astype(BF16), norm2_w[l],
                                           rw, rb, n_ctx_tiles)

        if l == depth - 1:
            idx = jnp.where((jnp.arange(lt) < n_ctx)[None, :, None], N_EXPERTS, idx)
        plan = _moe_plan(idx.reshape(n_tok, LANES)[:, :TOP_K], gate.reshape(n_tok, LANES)[:, :TOP_K], ts)
        w1g, w1l = _expert_w1(expert_w1[l])
        f8 = _moe(v8, plan, w1g, w1l, expert_w2[l].astype(BF16),
                  expert_b1[l][:, None, 0::2], expert_b1[l][:, None, 1::2], expert_b2[l][:, None, :], ts)

    return _final_norm(h, f8, mods[depth - 1], final_norm_w, n_ctx_tiles)
```

```python
import functools

import jax
import jax.numpy as jnp
from jax import lax
from jax.experimental import pallas as pl
from jax.experimental.pallas import tpu as pltpu

F32 = jnp.float32
BF16 = jnp.bfloat16
I32 = jnp.int32

D_MODEL = 1024
N_MOD = 6
NORM_EPS = 1e-6
SSD_HEADS, SSD_GROUPS, SSD_STATE = 8, 2, 64
SSD_D_INNER, SSD_XBC = 512, 768
GLA_HEADS, GLA_W, GLA_GATE_RANK, GLA_TAU = 4, 256, 16, 16.0
MLA_HEADS, MLA_Q_RANK, MLA_KV_RANK = 4, 192, 128
MLA_NOPE, MLA_ROPE, MLA_V, MLA_QK = 64, 32, 64, 96
GRID_W, ROPE_BASE = 64, 10000.0
N_EXPERTS, TOP_K, EXPERT_FF = 32, 4, 1024
SWIGLU_ALPHA, SWIGLU_LIMIT = 1.702, 7.0

LANES = 128
SUBLANES = 8
ROW_TILE = 256
CHUNK = 128
MOE_SUPER = 4096
MOE_ROWS = 128
MOE_PLANE = MOE_ROWS + SUBLANES
VMEM_LIMIT = 56 * 1024 * 1024
NEG = -1e30

IN_Z, IN_XBC, IN_DT, IN_GLA, IN_GLR, IN_MLA = 0, 512, 1280, 1408, 2432, 2560
IN_PAD_WIDTH = 3200
MLA_IN_W = 640


def _dot(a, b):
    return jnp.dot(a, b, preferred_element_type=F32)


def _dot_nt(a, b):
    return lax.dot_general(a, b, (((1,), (1,)), ((), ())), preferred_element_type=F32)


def _dot_tn(a, b):
    return lax.dot_general(a, b, (((0,), (0,)), ((), ())), preferred_element_type=F32)


def _split(x):
    hi = x.astype(BF16)
    lo = (x - hi.astype(F32)).astype(BF16)
    return hi, lo


def _dot_sel_rhs(x, m):
    hi, lo = _split(x)
    return _dot(hi, m) + _dot(lo, m)


def _dot_sel_lhs(m, x):
    hi, lo = _split(x)
    return _dot(m, hi) + _dot(m, lo)


def _dot3(a, b):
    ah, al = _split(a)
    bh, bl = _split(b)
    return _dot(ah, bh) + _dot(ah, bl) + _dot(al, bh)


def _softplus(x):
    return jnp.maximum(x, 0.0) + jnp.log(1.0 + jnp.exp(-jnp.abs(x)))


def _silu(x):
    return x * jax.nn.sigmoid(x)


def _rms(x, n=None):
    ms = jnp.sum(x * x, axis=-1, keepdims=True) * (1.0 / (n or x.shape[-1]))
    return x * lax.rsqrt(ms + NORM_EPS)


def _params(*sem):
    return pltpu.CompilerParams(dimension_semantics=sem, vmem_limit_bytes=VMEM_LIMIT)


def _mod_kernel(c_ref, w_ref, b_ref, o_ref):
    cv = c_ref[...]
    o_ref[0] = _dot3(_silu(cv), w_ref[0]) + b_ref[0]


def _modulation(cvec, w_mod, b_mod):
    depth, _, width = w_mod.shape
    rows = cvec.shape[0]
    tn = 1536
    return pl.pallas_call(
        _mod_kernel,
        grid=(depth, width // tn),
        in_specs=[pl.BlockSpec((rows, D_MODEL), lambda l, n: (0, 0)),
                  pl.BlockSpec((1, D_MODEL, tn), lambda l, n: (l, 0, n)),
                  pl.BlockSpec((1, 1, tn), lambda l, n: (l, 0, n))],
        out_specs=pl.BlockSpec((1, rows, tn), lambda l, n: (l, 0, n)),
        out_shape=jax.ShapeDtypeStruct((depth, rows, width), F32),
        compiler_params=_params("parallel", "parallel"),
        name="modulation",
    )(cvec, w_mod, b_mod.reshape(depth, 1, width))


def _from_token_vreg(f8_ref, rows):
    return jnp.concatenate([f8_ref[pl.ds(j, rows, stride=SUBLANES), :] for j in range(SUBLANES)], axis=1)


def _stream_specs(h, n_ctx_tiles):
    if isinstance(h, tuple):
        spec = lambda f: pl.BlockSpec((1, ROW_TILE, D_MODEL), lambda i, t: (i, f(t), 0))
        return [spec(lambda t: jnp.minimum(t, n_ctx_tiles - 1)), spec(lambda t: jnp.maximum(t - n_ctx_tiles, 0))], list(h)
    return [pl.BlockSpec((1, ROW_TILE, D_MODEL), lambda i, t: (i, t, 0))], [h]


def _stream_tile(refs, n_ctx_tiles):
    if len(refs) == 2:
        return jnp.where(pl.program_id(1) < n_ctx_tiles, refs[0][0], refs[1][0])
    return refs[0][0]


def _inproj_kernel(has_moe, n_stream, n_ctx_tiles, *refs):
    stream, refs = refs[:n_stream], refs[n_stream:]
    if has_moe:
        f8_ref, pmod_ref, mod_ref, nw_ref, w_ref = refs[:5]
        outs = refs[5:]
    else:
        mod_ref, nw_ref, w_ref = refs[:3]
        outs = refs[3:]
    x = _stream_tile(stream, n_ctx_tiles)
    if has_moe:
        x = x + pmod_ref[0, 0, 5:6, :] * _from_token_vreg(f8_ref, ROW_TILE)
        outs[6][0] = x
    mod = mod_ref[0, 0]
    u = (_rms(x) * nw_ref[...]) * (1.0 + mod[1:2, :]) + mod[0:1, :]
    ub = u.astype(BF16)
    z_ref, xbc_ref, dt_ref, gla_ref, glr_ref, mla_ref = outs[:6]
    z_ref[0] = _dot(ub, w_ref[:, IN_Z:IN_XBC]).astype(BF16)
    xbc_ref[0] = _dot(ub, w_ref[:, IN_XBC:IN_DT]).astype(BF16)
    dt_ref[0] = _dot(ub, w_ref[:, IN_DT:IN_GLA])
    gla_ref[0] = _dot(ub, w_ref[:, IN_GLA:IN_GLR]).astype(BF16)
    glr_ref[0] = _dot(ub, w_ref[:, IN_GLR:IN_MLA])
    mla_ref[0] = _dot(ub, w_ref[:, IN_MLA:IN_PAD_WIDTH]).astype(BF16)


def _in_projection(h, f8, prev_mod, mod, norm_w, w_pad, n_ctx_tiles):
    in_specs, args = _stream_specs(h, n_ctx_tiles)
    n_stream = len(args)
    b = args[0].shape[0]
    lt = sum(a.shape[1] for a in args)
    nt = lt // ROW_TILE
    has_moe = f8 is not None
    row = lambda w: pl.BlockSpec((1, ROW_TILE, w), lambda i, t: (i, t, 0))
    mod_spec = pl.BlockSpec((1, 1, N_MOD, D_MODEL), lambda i, t: (i, jnp.where(t >= n_ctx_tiles, 1, 0), 0, 0))
    if has_moe:
        in_specs += [pl.BlockSpec((ROW_TILE * SUBLANES, LANES), lambda i, t: (i * nt + t, 0)), mod_spec]
        args += [f8, prev_mod]
    in_specs += [mod_spec, pl.BlockSpec((1, D_MODEL), lambda i, t: (0, 0)),
                 pl.BlockSpec((D_MODEL, IN_PAD_WIDTH), lambda i, t: (0, 0))]
    args += [mod, norm_w.reshape(1, D_MODEL), w_pad]
    widths = [(SSD_D_INNER, BF16), (SSD_XBC, BF16), (LANES, F32), (4 * GLA_W, BF16), (LANES, F32), (MLA_IN_W, BF16)]
    if has_moe:
        widths.append((D_MODEL, F32))
    return pl.pallas_call(
        functools.partial(_inproj_kernel, has_moe, n_stream, n_ctx_tiles),
        grid=(b, nt),
        in_specs=in_specs,
        out_specs=[row(w) for w, _ in widths],
        out_shape=[jax.ShapeDtypeStruct((b, lt, w), dt) for w, dt in widths],
        compiler_params=_params("parallel", "parallel"),
        name="in_projection",
    )(*args)


def _conv_kernel(n_ctx_tiles, prev_ref, cur_ref, next_ref, w_ref, b_ref, o_ref):
    t = pl.program_id(1)
    nt = pl.num_programs(1)
    cur = cur_ref[0].astype(F32)
    has_prev = jnp.logical_and(t != 0, t != n_ctx_tiles)
    has_next = jnp.logical_and(t != n_ctx_tiles - 1, t != nt - 1)
    prev_row = jnp.where(has_prev, prev_ref[0][SUBLANES - 1:SUBLANES, :].astype(F32), 0.0)
    next_row = jnp.where(has_next, next_ref[0][0:1, :].astype(F32), 0.0)
    row = lax.broadcasted_iota(I32, cur.shape, 0)
    before = jnp.where(row == 0, prev_row, pltpu.roll(cur, 1, 0))
    after = jnp.where(row == ROW_TILE - 1, next_row, pltpu.roll(cur, ROW_TILE - 1, 0))
    y = w_ref[0:1, :] * before + w_ref[1:2, :] * cur + w_ref[2:3, :] * after + b_ref[...]
    o_ref[0] = _silu(y).astype(BF16)


def _ssd_conv(xbc, conv_w, conv_b, n_ctx_tiles):
    b, lt, w = xbc.shape
    nt = lt // ROW_TILE
    per = ROW_TILE // SUBLANES
    last = lt // SUBLANES - 1
    return pl.pallas_call(
        functools.partial(_conv_kernel, n_ctx_tiles),
        grid=(b, nt),
        in_specs=[pl.BlockSpec((1, SUBLANES, w), lambda i, t: (i, jnp.maximum(t * per - 1, 0), 0)),
                  pl.BlockSpec((1, ROW_TILE, w), lambda i, t: (i, t, 0)),
                  pl.BlockSpec((1, SUBLANES, w), lambda i, t: (i, jnp.minimum((t + 1) * per, last), 0)),
                  pl.BlockSpec((3, w), lambda i, t: (0, 0)),
                  pl.BlockSpec((1, w), lambda i, t: (0, 0))],
        out_specs=pl.BlockSpec((1, ROW_TILE, w), lambda i, t: (i, t, 0)),
        out_shape=jax.ShapeDtypeStruct((b, lt, w), BF16),
        compiler_params=_params("parallel", "parallel"),
        name="ssd_conv",
    )(xbc, xbc, xbc, conv_w, conv_b.reshape(1, w))


def _scan_chunk(reverse, n_ctx_chunks, n_chunks, j):
    if not reverse:
        return j
    return jnp.where(j < n_ctx_chunks, n_ctx_chunks - 1 - j, n_chunks - 1 - (j - n_ctx_chunks))


def _ssd_kernel(reverse, group, *refs):
    if reverse:
        (xbc_ref, dtc_ref, dtr_ref, acr_ref, acc_ref, dbr_ref, dbc_ref,
         yf_ref, z_ref, dsk_ref, nw_ref, out_ref, st_ref) = refs
    else:
        xbc_ref, dtc_ref, dtr_ref, acr_ref, acc_ref, dbr_ref, dbc_ref, out_ref, st_ref = refs
    q = CHUNK
    d = 1 if reverse else 0

    @pl.when(pl.program_id(1) == 0)
    def _():
        st_ref[...] = jnp.zeros_like(st_ref)

    row = lax.broadcasted_iota(I32, (q, q), 0)
    col = lax.broadcasted_iota(I32, (q, q), 1)
    if not reverse:
        lmat = jnp.where(row >= col, 1.0, 0.0).astype(BF16)
        umat = jnp.where(row <= col, 1.0, 0.0).astype(BF16)
        valid = row >= col
    else:
        lmat = jnp.where(row > col, 1.0, 0.0).astype(BF16)
        umat = jnp.where(row < col, 1.0, 0.0).astype(BF16)
        valid = col >= row
    hrow = lax.broadcasted_iota(I32, (SSD_HEADS, SSD_D_INNER), 0)
    hlane = lax.broadcasted_iota(I32, (SSD_HEADS, SSD_D_INNER), 1)
    expand = jnp.where(jnp.right_shift(hlane, 6) == hrow, 1.0, 0.0).astype(BF16)
    half = SSD_D_INNER // SSD_GROUPS
    lane_h = jnp.right_shift(lax.broadcasted_iota(I32, (q, half), 1), 6)
    lane_g = jnp.right_shift(lax.broadcasted_iota(I32, (q, LANES), 1), 6)

    for bi in range(group):
        xbc = xbc_ref[bi]
        xs = xbc[:, :SSD_D_INNER]
        bm = xbc[:, SSD_D_INNER:SSD_D_INNER + LANES]
        cm = xbc[:, SSD_D_INNER + LANES:]
        dt_c = _softplus(dtc_ref[bi][:, SSD_HEADS * d:SSD_HEADS * (d + 1)] + dbr_ref[0])
        dt_r = _softplus(dtr_ref[bi, 0] + dbc_ref[0])
        a_c = dt_c * acr_ref[0]
        a_r = dt_r * acc_ref[0]
        p_c = _dot_sel_lhs(lmat, a_c)
        p_r = _dot_sel_rhs(a_r, umat)
        if not reverse:
            tot = p_c[q - 1:q, :]
            inter_c = jnp.exp(p_c)
            w_c = jnp.exp(tot - p_c) * dt_c
        else:
            tot = p_c[q - 1:q, :] + a_c[q - 1:q, :]
            inter_c = jnp.exp(tot - p_c)
            w_c = jnp.exp(p_c) * dt_c
        inter_f = _dot(inter_c.astype(BF16), expand)
        w_f = _dot(w_c.astype(BF16), expand)
        dec_f = _dot_sel_rhs(jnp.broadcast_to(jnp.exp(tot), (SUBLANES, SSD_HEADS)), expand)[0:1]
        xs_f = xs.astype(F32)
        xw = (xs_f * w_f).astype(BF16)
        for g in range(SSD_GROUPS):
            sl = slice(g * half, (g + 1) * half)
            cg = jnp.where(lane_g == g, cm, jnp.zeros_like(cm))
            bg = jnp.where(lane_g == g, bm, jnp.zeros_like(bm))
            cb = _dot_nt(cg, bm)
            state = st_ref[bi, g]
            y = _dot(cg, state.astype(BF16)) * inter_f[:, sl]
            xs_g = xs[:, sl]
            scores, x_heads = [], []
            for hh in range(SSD_HEADS // SSD_GROUPS):
                h = g * (SSD_HEADS // SSD_GROUPS) + hh
                if not reverse:
                    seg = p_c[:, h:h + 1] - p_r[h:h + 1, :]
                else:
                    seg = p_r[h:h + 1, :] - p_c[:, h:h + 1]
                decay = jnp.exp(jnp.where(valid, seg, NEG))
                scores.append((cb * decay * dt_r[h:h + 1, :]).astype(BF16))
                x_heads.append(jnp.where(lane_h == hh, xs_g, jnp.zeros_like(xs_g)))
            y = y + _dot(jnp.concatenate(scores, axis=1), jnp.concatenate(x_heads, axis=0))
            st_ref[bi, g] = state * dec_f[:, sl] + _dot_tn(bg, xw[:, sl])
            if reverse:
                ytot = y + yf_ref[bi][:, sl].astype(F32) + dsk_ref[:, sl] * xs_f[:, sl]
                gated = ytot * _silu(z_ref[bi][:, sl].astype(F32))
                out_ref[bi, :, sl] = (_rms(gated) * nw_ref[:, sl]).astype(BF16)
            else:
                out_ref[bi, :, sl] = y.astype(BF16)


def _scan_group(b):
    return next(g for g in (4, 2, 1) if b % g == 0)


def _ssd_scan(reverse, xbc_act, dtc, dtr, a_coef, dt_bias, n_ctx_chunks, extra=None):
    b, lt, _ = xbc_act.shape
    nc = lt // CHUNK
    d = 1 if reverse else 0
    grp = _scan_group(b)
    cidx = functools.partial(_scan_chunk, reverse, n_ctx_chunks, nc)
    row = lambda w: pl.BlockSpec((grp, CHUNK, w), lambda i, j: (i, cidx(j), 0))
    small_r = pl.BlockSpec((1, 1, SSD_HEADS), lambda i, j: (d, 0, 0))
    small_c = pl.BlockSpec((1, SSD_HEADS, 1), lambda i, j: (d, 0, 0))
    in_specs = [row(SSD_XBC), row(LANES),
                pl.BlockSpec((grp, 1, SSD_HEADS, CHUNK), lambda i, j: (i, d, 0, cidx(j))),
                small_r, small_c, small_r, small_c]
    args = [xbc_act, dtc, dtr, a_coef.reshape(2, 1, SSD_HEADS), a_coef.reshape(2, SSD_HEADS, 1),
            dt_bias.reshape(2, 1, SSD_HEADS), dt_bias.reshape(2, SSD_HEADS, 1)]
    if reverse:
        yf, z, dsk, nw = extra
        vec = pl.BlockSpec((1, SSD_D_INNER), lambda i, j: (0, 0))
        in_specs += [row(SSD_D_INNER), row(SSD_D_INNER), vec, vec]
        args += [yf, z, dsk, nw]
    return pl.pallas_call(
        functools.partial(_ssd_kernel, reverse, grp),
        grid=(b // grp, nc),
        in_specs=in_specs,
        out_specs=row(SSD_D_INNER),
        out_shape=jax.ShapeDtypeStruct((b, lt, SSD_D_INNER), BF16),
        scratch_shapes=[pltpu.VMEM((grp, SSD_GROUPS, LANES, SSD_D_INNER // SSD_GROUPS), F32)],
        compiler_params=_params("parallel", "arbitrary"),
        name="ssd_scan_bwd" if reverse else "ssd_scan_fwd",
    )(*args)


def _gla_kernel(reverse, group, *refs):
    if reverse:
        gla_ref, glr_ref, wg_ref, bg_ref, of_ref, nw_ref, out_ref, st_ref = refs
    else:
        gla_ref, glr_ref, wg_ref, bg_ref, out_ref, st_ref = refs
    q = CHUNK
    w = GLA_W

    @pl.when(pl.program_id(1) == 0)
    def _():
        st_ref[...] = jnp.zeros_like(st_ref)

    row = lax.broadcasted_iota(I32, (q, q), 0)
    col = lax.broadcasted_iota(I32, (q, q), 1)
    if not reverse:
        cmat = jnp.where(row >= col, 1.0, 0.0).astype(BF16)
        valid = row >= col
    else:
        cmat = jnp.where(row > col, 1.0, 0.0).astype(BF16)
        valid = col >= row
    lane_h = jnp.right_shift(lax.broadcasted_iota(I32, (q, w), 1), 6)
    r2 = jnp.right_shift(lax.broadcasted_iota(I32, (w, w), 0), 6)
    c2 = jnp.right_shift(lax.broadcasted_iota(I32, (w, w), 1), 6)
    same_head = r2 == c2

    for bi in range(group):
        x = gla_ref[bi]
        qq = x[:, :w].astype(F32) * (64.0 ** -0.5)
        kk = x[:, w:2 * w].astype(F32)
        vv = x[:, 2 * w:3 * w]
        gpre = _dot3(glr_ref[bi], wg_ref[0]) + bg_ref[0]
        g = (jnp.minimum(gpre, 0.0) - jnp.log(1.0 + jnp.exp(-jnp.abs(gpre)))) * (1.0 / GLA_TAU)
        gc = _dot_sel_lhs(cmat, g)
        if not reverse:
            tot = gc[q - 1:q, :]
            q_dec = qq * jnp.exp(gc)
            k_inv = kk * jnp.exp(-gc)
            k_st = kk * jnp.exp(tot - gc)
        else:
            tot = gc[q - 1:q, :] + g[q - 1:q, :]
            q_dec = qq * jnp.exp(tot - gc)
            k_inv = kk * jnp.exp(gc - tot)
            k_st = kk * jnp.exp(gc)
        state = st_ref[bi]
        qd_b = q_dec.astype(BF16)
        ki_b = k_inv.astype(BF16)
        atts, v_heads = [], []
        for h in range(GLA_HEADS):
            att = _dot_nt(jnp.where(lane_h == h, qd_b, jnp.zeros_like(qd_b)), ki_b)
            atts.append(jnp.where(valid, att, 0.0).astype(BF16))
            v_heads.append(jnp.where(lane_h == h, vv, jnp.zeros_like(vv)))
        o = (_dot_nt(qd_b, state.astype(BF16))
             + _dot(jnp.concatenate(atts, axis=1), jnp.concatenate(v_heads, axis=0)))
        st_ref[bi] = jnp.where(same_head, state * jnp.exp(tot) + _dot_tn(vv, k_st.astype(BF16)), 0.0)
        if reverse:
            ot = o + of_ref[bi].astype(F32)
            ms = _dot_sel_rhs(ot * ot, jnp.where(same_head, 1.0, 0.0).astype(BF16)) * (1.0 / 64.0)
            rr = x[:, 3 * w:].astype(F32)
            out_ref[bi] = (ot * lax.rsqrt(ms + NORM_EPS) * nw_ref[...] * _silu(rr)).astype(BF16)
        else:
            out_ref[bi] = o.astype(BF16)


def _gla_scan(reverse, gla_in, glr, wg_pad, g_up_b, n_ctx_chunks, extra=None):
    b, lt, _ = gla_in.shape
    nc = lt // CHUNK
    d = 1 if reverse else 0
    grp = _scan_group(b)
    cidx = functools.partial(_scan_chunk, reverse, n_ctx_chunks, nc)
    row = lambda w: pl.BlockSpec((grp, CHUNK, w), lambda i, j: (i, cidx(j), 0))
    in_specs = [row(4 * GLA_W), row(LANES),
                pl.BlockSpec((1, LANES, GLA_W), lambda i, j: (d, 0, 0)),
                pl.BlockSpec((1, 1, GLA_W), lambda i, j: (d, 0, 0))]
    args = [gla_in, glr, wg_pad, g_up_b.reshape(2, 1, GLA_W)]
    if reverse:
        of, nw = extra
        in_specs += [row(GLA_W), pl.BlockSpec((1, GLA_W), lambda i, j: (0, 0))]
        args += [of, nw]
    return pl.pallas_call(
        functools.partial(_gla_kernel, reverse, grp),
        grid=(b // grp, nc),
        in_specs=in_specs,
        out_specs=row(GLA_W),
        out_shape=jax.ShapeDtypeStruct((b, lt, GLA_W), BF16),
        scratch_shapes=[pltpu.VMEM((grp, GLA_W, GLA_W), F32)],
        compiler_params=_params("parallel", "arbitrary"),
        name="gla_scan_bwd" if reverse else "gla_scan_fwd",
    )(*args)


def _mla_prep_kernel(m_ref, qnw_ref, wq_ref, wqs_ref, kvnw_ref, wk_ref, wv_ref, cos_ref, sin_ref,
                     q_ref, k_ref, v_ref):
    m = m_ref[0]
    cq = m[:, :2 * LANES].astype(F32)
    ckv = m[:, 2 * LANES:3 * LANES].astype(F32)
    ka = m[:, 3 * LANES:4 * LANES].astype(F32)
    kb = m[:, 4 * LANES:].astype(F32)
    cos = cos_ref[...]
    sin = sin_ref[...]
    cqn = (_rms(cq, MLA_Q_RANK) * qnw_ref[...]).astype(BF16)
    qm = _dot(cqn, wq_ref[...])
    qs = _dot(cqn, wqs_ref[...])
    ckn = (_rms(ckv) * kvnw_ref[...]).astype(BF16)
    kn = _dot(ckn, wk_ref[...])
    k_rot = ka * cos + kb * sin
    v_ref[0] = _dot(ckn, wv_ref[...]).astype(BF16)
    for h in range(MLA_HEADS):
        sl = slice(h * LANES, (h + 1) * LANES)
        q_ref[0, h] = ((qm[:, sl] * cos + qs[:, sl] * sin) * (MLA_QK ** -0.5)).astype(BF16)
        k_ref[0, h] = (kn[:, sl] + k_rot).astype(BF16)


def _mla_prep(mla_in, qnw, wq, wqs, kvnw, wk, wv, cos_t, sin_t):
    b, lt, _ = mla_in.shape
    nt = lt // ROW_TILE
    full = lambda a: pl.BlockSpec(a.shape, lambda i, t: (0,) * a.ndim)
    tab = pl.BlockSpec((ROW_TILE, LANES), lambda i, t: (t, 0))
    head = pl.BlockSpec((1, MLA_HEADS, ROW_TILE, LANES), lambda i, t: (i, 0, t, 0))
    return pl.pallas_call(
        _mla_prep_kernel,
        grid=(b, nt),
        in_specs=[pl.BlockSpec((1, ROW_TILE, MLA_IN_W), lambda i, t: (i, t, 0)),
                  full(qnw), full(wq), full(wqs), full(kvnw), full(wk), full(wv), tab, tab],
        out_specs=[head, head, pl.BlockSpec((1, ROW_TILE, MLA_HEADS * MLA_V), lambda i, t: (i, t, 0))],
        out_shape=[jax.ShapeDtypeStruct((b, MLA_HEADS, lt, LANES), BF16),
                   jax.ShapeDtypeStruct((b, MLA_HEADS, lt, LANES), BF16),
                   jax.ShapeDtypeStruct((b, lt, MLA_HEADS * MLA_V), BF16)],
        compiler_params=_params("parallel", "parallel"),
        name="mla_prep",
    )(mla_in, qnw, wq, wqs, kvnw, wk, wv, cos_t, sin_t)


def _attn_kernel(n_ctx, q_ref, k_ref, v_ref, o_ref):
    def attend(n_keys):
        width = MLA_HEADS * MLA_V
        lane_k = lax.broadcasted_iota(I32, (n_keys, width), 1)
        lane_q = lax.broadcasted_iota(I32, (ROW_TILE, width), 1)
        vv = v_ref[0, :n_keys, :]
        acc = jnp.zeros((ROW_TILE, width), F32)
        for h in range(MLA_HEADS):
            s = _dot_nt(q_ref[0, h], k_ref[0, h, :n_keys, :])
            p = jnp.exp((s - jnp.max(s, axis=-1, keepdims=True)).astype(BF16))
            ones_at = ((h + 1) % MLA_HEADS) * MLA_V
            vh = jnp.where(jnp.right_shift(lane_k, 6) == h, vv,
                           jnp.where(lane_k == ones_at, 1.0, 0.0).astype(vv.dtype))
            pv = _dot(p, vh)
            denom = pv[:, ones_at:ones_at + 1]
            acc = acc + jnp.where(jnp.right_shift(lane_q, 6) == h, pv * (1.0 / denom), 0.0)
        o_ref[0] = acc.astype(BF16)

    is_ctx = pl.program_id(1) < n_ctx // ROW_TILE
    pl.when(is_ctx)(lambda: attend(n_ctx))
    pl.when(jnp.logical_not(is_ctx))(lambda: attend(k_ref.shape[2]))


def _attention(q, k, v, n_ctx):
    b, _, lt, _ = q.shape
    nt = lt // ROW_TILE
    return pl.pallas_call(
        functools.partial(_attn_kernel, n_ctx),
        grid=(b, nt),
        in_specs=[pl.BlockSpec((1, MLA_HEADS, ROW_TILE, LANES), lambda i, t: (i, 0, t, 0)),
                  pl.BlockSpec((1, MLA_HEADS, lt, LANES), lambda i, t: (i, 0, 0, 0)),
                  pl.BlockSpec((1, lt, MLA_HEADS * MLA_V), lambda i, t: (i, 0, 0))],
        out_specs=pl.BlockSpec((1, ROW_TILE, MLA_HEADS * MLA_V), lambda i, t: (i, t, 0)),
        out_shape=jax.ShapeDtypeStruct((b, lt, MLA_HEADS * MLA_V), BF16),
        compiler_params=_params("parallel", "arbitrary"),
        name="mla_attention",
    )(q, k, v)


def _outproj_kernel(n_stream, n_ctx_tiles, *refs):
    stream, refs = refs[:n_stream], refs[n_stream:]
    (ssd_ref, gla_ref, mla_ref, mod_ref, w_ref, nw_ref, rw_ref, rb_ref,
     h_out_ref, v8_ref, idx_ref, gate_ref) = refs
    mod = mod_ref[0, 0]
    mix = (_dot(ssd_ref[0], w_ref[:SSD_D_INNER, :])
           + _dot(gla_ref[0], w_ref[SSD_D_INNER:SSD_D_INNER + GLA_W, :])
           + _dot(mla_ref[0], w_ref[SSD_D_INNER + GLA_W:, :]))
    hm = _stream_tile(stream, n_ctx_tiles) + mod[2:3, :] * mix
    h_out_ref[0] = hm
    v = (_rms(hm) * nw_ref[...]) * (1.0 + mod[4:5, :]) + mod[3:4, :]
    for j in range(SUBLANES):
        v8_ref[pl.ds(j, ROW_TILE, stride=SUBLANES), :] = v[:, j * LANES:(j + 1) * LANES]
    logits = _dot(v.astype(BF16), rw_ref[...]) + rb_ref[...]
    lane = lax.broadcasted_iota(I32, logits.shape, 1).astype(F32)
    idxs, vals = [], []
    for _ in range(TOP_K):
        top = jnp.max(logits, axis=-1, keepdims=True)
        pick = jnp.min(jnp.where(logits == top, lane, float(LANES)), axis=-1, keepdims=True)
        idxs.append(pick)
        vals.append(top)
        logits = jnp.where(lane == pick, 2.0 * NEG, logits)
    exps = [jnp.exp(t - vals[0]) for t in vals]
    inv = 1.0 / (exps[0] + exps[1] + exps[2] + exps[3])
    idx_out = jnp.zeros(lane.shape, F32)
    gate_out = jnp.zeros(lane.shape, F32)
    for k in range(TOP_K):
        idx_out = jnp.where(lane == float(k), idxs[k], idx_out)
        gate_out = jnp.where(lane == float(k), exps[k] * inv, gate_out)
    pick_r = lax.broadcasted_iota(I32, (SUBLANES, LANES), 0)
    pick_c = lax.broadcasted_iota(I32, (SUBLANES, LANES), 1)
    pick = jnp.where(pick_r == pick_c, 1.0, 0.0).astype(BF16)
    idx_ref[0, 0] = _dot_nt(pick, idx_out.astype(BF16)).astype(I32)
    g_hi, g_lo = _split(gate_out)
    gate_ref[0, 0] = _dot_nt(pick, g_hi) + _dot_nt(pick, g_lo)


def _out_projection(ssd_o, gla_o, mla_o, h, mod, w_out, norm_w, rw_pad, rb_pad, n_ctx_tiles):
    h_specs, h_args = _stream_specs(h, n_ctx_tiles)
    b, lt, _ = ssd_o.shape
    nt = lt // ROW_TILE
    row = lambda w: pl.BlockSpec((1, ROW_TILE, w), lambda i, t: (i, t, 0))
    full = lambda a: pl.BlockSpec(a.shape, lambda i, t: (0,) * a.ndim)
    mod_spec = pl.BlockSpec((1, 1, N_MOD, D_MODEL), lambda i, t: (i, jnp.where(t >= n_ctx_tiles, 1, 0), 0, 0))
    compact = pl.BlockSpec((1, 1, SUBLANES, ROW_TILE), lambda i, t: (i, t, 0, 0))
    nw = norm_w.reshape(1, D_MODEL)
    return pl.pallas_call(
        functools.partial(_outproj_kernel, len(h_args), n_ctx_tiles),
        grid=(b, nt),
        in_specs=h_specs + [row(SSD_D_INNER), row(GLA_W), row(MLA_HEADS * MLA_V), mod_spec,
                            full(w_out), full(nw), full(rw_pad), full(rb_pad)],
        out_specs=[row(D_MODEL),
                   pl.BlockSpec((ROW_TILE * SUBLANES, LANES), lambda i, t: (i * nt + t, 0)),
                   compact, compact],
        out_shape=[jax.ShapeDtypeStruct((b, lt, D_MODEL), F32),
                   jax.ShapeDtypeStruct((b * lt * SUBLANES, LANES), F32),
                   jax.ShapeDtypeStruct((b, nt, SUBLANES, ROW_TILE), I32),
                   jax.ShapeDtypeStruct((b, nt, SUBLANES, ROW_TILE), F32)],
        compiler_params=_params("parallel", "parallel"),
        name="out_projection",
    )(*h_args, ssd_o, gla_o, mla_o, mod, w_out, nw, rw_pad, rb_pad)


def _expert_w1_kernel(w_ref, g_ref, l_ref):
    blk = 2 * LANES
    row = lax.broadcasted_iota(I32, (blk, LANES), 0)
    col = lax.broadcasted_iota(I32, (blk, LANES), 1)
    even = jnp.where(row == 2 * col, 1.0, 0.0).astype(BF16)
    odd = jnp.where(row == 2 * col + 1, 1.0, 0.0).astype(BF16)
    for c in range(2 * EXPERT_FF // blk):
        w = w_ref[0, :, c * blk:(c + 1) * blk].astype(BF16)
        g_ref[0, :, c * LANES:(c + 1) * LANES] = _dot(w, even).astype(BF16)
        l_ref[0, :, c * LANES:(c + 1) * LANES] = _dot(w, odd).astype(BF16)


def _expert_w1(w1):
    n_e = w1.shape[0]
    out = pl.BlockSpec((1, D_MODEL, EXPERT_FF), lambda e: (e, 0, 0))
    return pl.pallas_call(
        _expert_w1_kernel,
        grid=(n_e,),
        in_specs=[pl.BlockSpec((1, D_MODEL, 2 * EXPERT_FF), lambda e: (e, 0, 0))],
        out_specs=[out, out],
        out_shape=[jax.ShapeDtypeStruct((n_e, D_MODEL, EXPERT_FF), BF16)] * 2,
        compiler_params=_params("parallel"),
        name="expert_w1_split",
    )(w1)


def _moe_kernel(ts, off_ref, cnt_ref, end_ref, row_ref, gate_ref, x8_ref, w1g_ref, w1l_ref, w2_ref,
                b1g_ref, b1l_ref, b2_ref, f8_ref, xa_ref, xb_ref, ya_ref, yb_ref):
    s = pl.program_id(0)
    e = pl.program_id(1)
    r = MOE_ROWS
    batch = SUBLANES

    @pl.when(e == 0)
    def _():
        f8_ref[...] = jnp.zeros_like(f8_ref)

    base0 = off_ref[s * N_EXPERTS + e]
    end = end_ref[s * N_EXPERTS + e]
    n_tiles = cnt_ref[s * N_EXPERTS + e]

    def token_tile(row):
        return pl.multiple_of(jnp.right_shift(row, TOP_K.bit_length() - 1) * SUBLANES, SUBLANES)

    def gather(base, xg_ref):
        for rr in range(r):
            src = token_tile(row_ref[0, 0, base + rr])
            xg_ref[pl.ds(rr, SUBLANES, stride=MOE_PLANE), :] = x8_ref[pl.ds(src, SUBLANES), :]

    def ffn(xg_ref, yp_ref):
        x = jnp.concatenate([xg_ref[j * MOE_PLANE:j * MOE_PLANE + r, :] for j in range(SUBLANES)],
                            axis=1).astype(BF16)
        glu = jnp.minimum(_dot(x, w1g_ref[0]) + b1g_ref[0], SWIGLU_LIMIT)
        lin = jnp.clip(_dot(x, w1l_ref[0]) + b1l_ref[0], -SWIGLU_LIMIT, SWIGLU_LIMIT)
        sig = 0.5 * jnp.tanh((0.5 * SWIGLU_ALPHA) * glu) + 0.5
        act = (glu * sig * (lin + 1.0)).astype(BF16)
        y = _dot(act, w2_ref[0]) + b2_ref[0]
        for j in range(SUBLANES):
            yp_ref[j * MOE_PLANE:j * MOE_PLANE + r, :] = y[:, j * LANES:(j + 1) * LANES]

    def scatter(base, yp_ref):
        for r0 in range(0, r, batch):
            new = []
            for rr in range(r0, r0 + batch):
                row = row_ref[0, 0, base + rr]
                dst = token_tile(row)
                gate = jnp.where(base + rr < end, gate_ref[0, 0, row], 0.0)
                new.append((dst, f8_ref[pl.ds(dst, SUBLANES), :]
                            + gate * yp_ref[pl.ds(rr, SUBLANES, stride=MOE_PLANE), :]))
            for dst, val in reversed(new):
                f8_ref[pl.ds(dst, SUBLANES), :] = val

    def pair(i, carry):
        base = base0 + i * (2 * r)
        gather(base, xa_ref)
        gather(base + r, xb_ref)
        ffn(xa_ref, ya_ref)
        ffn(xb_ref, yb_ref)
        scatter(base, ya_ref)
        scatter(base + r, yb_ref)
        return carry

    lax.fori_loop(0, jnp.right_shift(n_tiles, 1), pair, 0)

    @pl.when(jnp.bitwise_and(n_tiles, 1) == 1)
    def _():
        base = base0 + (n_tiles - 1) * r
        gather(base, xa_ref)
        ffn(xa_ref, ya_ref)
        scatter(base, ya_ref)


def _moe(x8, plan, w1g, w1l, w2, b1g, b1l, b2, ts):
    off, cnt, end, rows, gate = plan
    n_super = x8.shape[0] // (ts * SUBLANES)
    wspec = pl.BlockSpec((1, D_MODEL, EXPERT_FF), lambda s, e, *_: (e, 0, 0))
    w2spec = pl.BlockSpec((1, EXPERT_FF, D_MODEL), lambda s, e, *_: (e, 0, 0))
    bspec = pl.BlockSpec((1, 1, EXPERT_FF), lambda s, e, *_: (e, 0, 0))
    smem = lambda a: pl.BlockSpec((1, 1, a.shape[-1]), lambda s, e, *_: (s, 0, 0), memory_space=pltpu.SMEM)
    win = pl.BlockSpec((ts * SUBLANES, LANES), lambda s, e, *_: (s, 0), pipeline_mode=pl.Buffered(1))
    grid_spec = pltpu.PrefetchScalarGridSpec(
        num_scalar_prefetch=3,
        grid=(n_super, N_EXPERTS),
        in_specs=[smem(rows), smem(gate), win, wspec, wspec, w2spec, bspec, bspec, bspec],
        out_specs=pl.BlockSpec((ts * SUBLANES, LANES), lambda s, e, *_: (s, 0), pipeline_mode=pl.Buffered(1)),
        scratch_shapes=[pltpu.VMEM((SUBLANES * MOE_PLANE, LANES), F32)] * 4,
    )
    return pl.pallas_call(
        functools.partial(_moe_kernel, ts),
        grid_spec=grid_spec,
        out_shape=jax.ShapeDtypeStruct(x8.shape, F32),
        compiler_params=_params("arbitrary", "arbitrary"),
        name="moe_experts",
    )(off, cnt, end, rows, gate, x8, w1g, w1l, w2, b1g, b1l, b2)


def _moe_plan(idx, gate, ts):
    n_tok = idx.shape[0]
    n_super = n_tok // ts
    n_rows = ts * TOP_K
    flat_e = idx.reshape(n_super, n_rows)
    key = lax.sort(flat_e * n_rows + jnp.arange(n_rows, dtype=I32)[None, :], dimension=1)
    counts = jnp.sum(flat_e[:, :, None] == jnp.arange(N_EXPERTS, dtype=I32)[None, None, :], axis=1).astype(I32)
    end = jnp.cumsum(counts, axis=1)
    off = end - counts
    tiles = (counts + MOE_ROWS - 1) // MOE_ROWS
    rows = jnp.pad(key % n_rows, ((0, 0), (0, MOE_ROWS)))
    return (off.reshape(-1), tiles.reshape(-1), end.reshape(-1),
            rows.astype(I32).reshape(n_super, 1, n_rows + MOE_ROWS),
            gate.astype(F32).reshape(n_super, 1, n_rows))


def _final_kernel(h_ref, f8_ref, mod_ref, w_ref, o_ref):
    x = h_ref[0] + mod_ref[0, 0, 5:6, :] * _from_token_vreg(f8_ref, ROW_TILE)
    o_ref[0] = _rms(x) * w_ref[...]


def _final_norm(h, f8, mod, w, n_ctx_tiles):
    b, lt, _ = h.shape
    nt = lt // ROW_TILE
    nl = nt - n_ctx_tiles
    return pl.pallas_call(
        _final_kernel,
        grid=(b, nl),
        in_specs=[pl.BlockSpec((1, ROW_TILE, D_MODEL), lambda i, t: (i, t + n_ctx_tiles, 0)),
                  pl.BlockSpec((ROW_TILE * SUBLANES, LANES), lambda i, t: (i * nt + t + n_ctx_tiles, 0)),
                  pl.BlockSpec((1, 1, N_MOD, D_MODEL), lambda i, t: (i, 1, 0, 0)),
                  pl.BlockSpec((1, D_MODEL), lambda i, t: (0, 0))],
        out_specs=pl.BlockSpec((1, ROW_TILE, D_MODEL), lambda i, t: (i, t, 0)),
        out_shape=jax.ShapeDtypeStruct((b, nl * ROW_TILE, D_MODEL), F32),
        compiler_params=_params("parallel", "parallel"),
        name="final_norm",
    )(h, f8, mod, w.reshape(1, D_MODEL))


def _rope_partner():
    i = jnp.arange(MLA_ROPE)
    return jnp.where((i % 16) < 8, i + 8, i - 8)


def _pad_cols(w, width):
    return jnp.pad(w, ((0, 0), (0, width - w.shape[1])))


def _in_weight(w_in):
    sizes = (SSD_D_INNER, SSD_XBC, 2 * SSD_HEADS, GLA_W, GLA_W, GLA_W, GLA_W, 2 * GLA_GATE_RANK,
             MLA_Q_RANK, MLA_KV_RANK, MLA_ROPE)
    cols, acc = [], 0
    for s in sizes:
        cols.append(w_in[:, acc:acc + s])
        acc += s
    z, xbc, dt, gq, gk, gv, gr, glr, cq, ckv, kpe = cols
    rope_at = lambda w: jnp.pad(w, ((0, 0), (MLA_NOPE, LANES - MLA_NOPE - MLA_ROPE)))
    parts = [z, xbc, _pad_cols(dt, LANES), gq, gk, gv, gr, _pad_cols(glr, LANES), _pad_cols(cq, 2 * LANES), ckv,
             rope_at(kpe), rope_at(kpe[:, _rope_partner()])]
    return jnp.concatenate(parts, axis=1).astype(BF16)


def _mla_weights(w_uq, w_ukv):
    wq = w_uq.reshape(MLA_Q_RANK, MLA_HEADS, MLA_QK)
    nope, rope = wq[..., :MLA_NOPE], wq[..., MLA_NOPE:]
    zeros = jnp.zeros((MLA_Q_RANK, MLA_HEADS, LANES - MLA_QK), F32)
    main = jnp.concatenate([nope, rope, zeros], axis=-1)
    swap = jnp.concatenate([jnp.zeros_like(nope), rope[..., _rope_partner()], zeros], axis=-1)
    pad_rows = lambda w: jnp.pad(w.reshape(MLA_Q_RANK, MLA_HEADS * LANES), ((0, 2 * LANES - MLA_Q_RANK), (0, 0)))
    wkv = w_ukv.reshape(MLA_KV_RANK, MLA_HEADS, MLA_NOPE + MLA_V)
    wk = jnp.pad(wkv[..., :MLA_NOPE], ((0, 0), (0, 0), (0, LANES - MLA_NOPE))).reshape(MLA_KV_RANK, MLA_HEADS * LANES)
    wv = wkv[..., MLA_NOPE:].reshape(MLA_KV_RANK, MLA_HEADS * MLA_V)
    return pad_rows(main).astype(BF16), pad_rows(swap).astype(BF16), wk.astype(BF16), wv.astype(BF16)


def _rope_tables(n_ctx, n_lat):
    pos = jnp.arange(n_lat, dtype=F32)
    rowp = jnp.floor(pos / GRID_W)
    colp = pos - rowp * GRID_W
    half = MLA_ROPE // 2
    inv_freq = 1.0 / (ROPE_BASE ** (jnp.arange(0, half, 2, dtype=F32) / half))
    ang = jnp.stack([rowp[:, None] * inv_freq, colp[:, None] * inv_freq], axis=1)
    cos, sin = jnp.cos(ang), jnp.sin(ang)
    cos32 = jnp.concatenate([cos, cos], axis=2).reshape(n_lat, MLA_ROPE)
    sin32 = jnp.concatenate([-sin, sin], axis=2).reshape(n_lat, MLA_ROPE)
    cos32 = jnp.concatenate([jnp.ones((n_ctx, MLA_ROPE), F32), cos32], axis=0)
    sin32 = jnp.concatenate([jnp.zeros((n_ctx, MLA_ROPE), F32), sin32], axis=0)
    n = n_ctx + n_lat
    cos_t = jnp.concatenate([jnp.ones((n, MLA_NOPE), F32), cos32, jnp.zeros((n, LANES - MLA_QK), F32)], axis=1)
    sin_t = jnp.concatenate([jnp.zeros((n, MLA_NOPE), F32), sin32, jnp.zeros((n, LANES - MLA_QK), F32)], axis=1)
    return cos_t, sin_t


def kernel(x, c, ctx, c_ctx, w_mod, b_mod, norm1_w, w_in, ssd_conv_w, ssd_conv_b, ssd_a_log, ssd_dt_bias, ssd_d, ssd_norm_w, gla_g_up_w, gla_g_up_b, gla_norm_w, mla_q_norm_w, mla_w_uq, mla_kv_norm_w, mla_w_ukv, w_out, norm2_w, router_w, router_b, expert_w1, expert_b1, expert_w2, expert_b2, final_norm_w):
    b, n_lat, _ = x.shape
    n_ctx = ctx.shape[1]
    lt = n_ctx + n_lat
    depth = w_mod.shape[0]
    assert n_ctx % ROW_TILE == 0 and n_lat % ROW_TILE == 0 and n_lat % GRID_W == 0
    n_ctx_tiles = n_ctx // ROW_TILE
    n_ctx_chunks = n_ctx // CHUNK
    n_tok = b * lt
    ts = MOE_SUPER if n_tok % MOE_SUPER == 0 else n_tok
    assert b <= 16

    h = (ctx, x)
    cvec = jnp.concatenate([c, c_ctx[None, :], jnp.zeros((24 - b - 1, D_MODEL), F32)], axis=0)
    mod_all = _modulation(cvec, w_mod, b_mod).reshape(depth, 24, N_MOD, D_MODEL)
    mods = [jnp.stack([jnp.broadcast_to(mod_all[l, b], (b, N_MOD, D_MODEL)), mod_all[l, :b]], axis=1)
            for l in range(depth)]
    cos_t, sin_t = _rope_tables(n_ctx, n_lat)

    f8 = None
    for l in range(depth):
        outs = _in_projection(h, f8, mods[l - 1] if l else None, mods[l], norm1_w[l], _in_weight(w_in[l]),
                              n_ctx_tiles)
        z, xbc, dt, gla_in, glr, mla_in = outs[:6]
        if f8 is not None:
            h = outs[6]

        a_coef = -jnp.exp(ssd_a_log[l].astype(F32))
        xbc_act = _ssd_conv(xbc, ssd_conv_w[l], ssd_conv_b[l], n_ctx_tiles)
        dtr = jnp.transpose(dt[:, :, :2 * SSD_HEADS], (0, 2, 1)).reshape(b, 2, SSD_HEADS, lt)
        yf = _ssd_scan(False, xbc_act, dt, dtr, a_coef, ssd_dt_bias[l], n_ctx_chunks)
        dsk = jnp.repeat(ssd_d[l].astype(F32), SSD_D_INNER // SSD_HEADS).reshape(1, SSD_D_INNER)
        ssd_o = _ssd_scan(True, xbc_act, dt, dtr, a_coef, ssd_dt_bias[l], n_ctx_chunks,
                          extra=(yf, z, dsk, ssd_norm_w[l].reshape(1, SSD_D_INNER)))

        wg = jnp.zeros((2, LANES, GLA_W), F32)
        for d in range(2):
            wg = wg.at[d, d * GLA_GATE_RANK:(d + 1) * GLA_GATE_RANK].set(gla_g_up_w[l, d])
        of = _gla_scan(False, gla_in, glr, wg, gla_g_up_b[l], n_ctx_chunks)
        gnw = jnp.tile(gla_norm_w[l].astype(F32), GLA_HEADS).reshape(1, GLA_W)
        gla_o = _gla_scan(True, gla_in, glr, wg, gla_g_up_b[l], n_ctx_chunks, extra=(of, gnw))

        wq, wqs, wk, wv = _mla_weights(mla_w_uq[l], mla_w_ukv[l])
        qnw = jnp.pad(mla_q_norm_w[l], (0, 2 * LANES - MLA_Q_RANK)).reshape(1, 2 * LANES)
        qh, kh, vh = _mla_prep(mla_in, qnw, wq, wqs, mla_kv_norm_w[l].reshape(1, MLA_KV_RANK), wk, wv, cos_t, sin_t)
        mla_o = _attention(qh, kh, vh, n_ctx)

        rw = _pad_cols(router_w[l], LANES).astype(BF16)
        rb = jnp.concatenate([router_b[l].astype(F32), jnp.full((LANES - N_EXPERTS,), NEG, F32)]).reshape(1, LANES)
        h, v8, idx, gate = _out_projection(ssd_o, gla_o, mla_o, h, mods[l], w_out[l].astype(BF16), norm2_w[l],
                                           rw, rb, n_ctx_tiles)

        per_token = lambda a: jnp.swapaxes(a[:, :, :TOP_K, :], 2, 3).reshape(b, lt, TOP_K)
        idx, gate = per_token(idx), per_token(gate)
        if l == depth - 1:
            idx = jnp.where((jnp.arange(lt) < n_ctx)[None, :, None], N_EXPERTS, idx)
        plan = _moe_plan(idx.reshape(n_tok, TOP_K), gate.reshape(n_tok, TOP_K), ts)
        w1g, w1l = _expert_w1(expert_w1[l])
        f8 = _moe(v8, plan, w1g, w1l, expert_w2[l].astype(BF16),
                  expert_b1[l][:, None, 0::2], expert_b1[l][:, None, 1::2], expert_b2[l][:, None, :], ts)

    return _final_norm(h, f8, mods[depth - 1], final_norm_w, n_ctx_tiles)
```

```python
import functools

import jax
import jax.numpy as jnp
from jax import lax
from jax.experimental import pallas as pl
from jax.experimental.pallas import tpu as pltpu

F32 = jnp.float32
BF16 = jnp.bfloat16
I32 = jnp.int32

D_MODEL = 1024
N_MOD = 6
NORM_EPS = 1e-6
SSD_HEADS, SSD_GROUPS, SSD_STATE = 8, 2, 64
SSD_D_INNER, SSD_XBC = 512, 768
GLA_HEADS, GLA_W, GLA_GATE_RANK, GLA_TAU = 4, 256, 16, 16.0
MLA_HEADS, MLA_Q_RANK, MLA_KV_RANK = 4, 192, 128
MLA_NOPE, MLA_ROPE, MLA_V, MLA_QK = 64, 32, 64, 96
GRID_W, ROPE_BASE = 64, 10000.0
N_EXPERTS, TOP_K, EXPERT_FF = 32, 4, 1024
SWIGLU_ALPHA, SWIGLU_LIMIT = 1.702, 7.0

LANES = 128
SUBLANES = 8
ROW_TILE = 256
CHUNK = 128
MOE_SUPER = 4096
MOE_ROWS = 128
MOE_PLANE = MOE_ROWS + SUBLANES
VMEM_LIMIT = 56 * 1024 * 1024
NEG = -1e30

IN_Z, IN_XBC, IN_DT, IN_GLA, IN_GLR, IN_MLA = 0, 512, 1280, 1408, 2432, 2560
IN_PAD_WIDTH = 3200
MLA_IN_W = 640


def _dot(a, b):
    return jnp.dot(a, b, preferred_element_type=F32)


def _dot_nt(a, b):
    return lax.dot_general(a, b, (((1,), (1,)), ((), ())), preferred_element_type=F32)


def _dot_tn(a, b):
    return lax.dot_general(a, b, (((0,), (0,)), ((), ())), preferred_element_type=F32)


def _split(x):
    hi = x.astype(BF16)
    lo = (x - hi.astype(F32)).astype(BF16)
    return hi, lo


def _dot_sel_rhs(x, m):
    hi, lo = _split(x)
    return _dot(hi, m) + _dot(lo, m)


def _dot_sel_lhs(m, x):
    hi, lo = _split(x)
    return _dot(m, hi) + _dot(m, lo)


def _dot3(a, b):
    ah, al = _split(a)
    bh, bl = _split(b)
    return _dot(ah, bh) + _dot(ah, bl) + _dot(al, bh)


def _softplus(x):
    return jnp.maximum(x, 0.0) + jnp.log(1.0 + jnp.exp(-jnp.abs(x)))


def _silu(x):
    return x * jax.nn.sigmoid(x)


def _rms(x, n=None):
    ms = jnp.sum(x * x, axis=-1, keepdims=True) * (1.0 / (n or x.shape[-1]))
    return x * lax.rsqrt(ms + NORM_EPS)


def _params(*sem):
    return pltpu.CompilerParams(dimension_semantics=sem, vmem_limit_bytes=VMEM_LIMIT)


def _mod_kernel(c_ref, w_ref, b_ref, o_ref):
    cv = c_ref[...]
    o_ref[0] = _dot3(_silu(cv), w_ref[0]) + b_ref[0]


def _modulation(cvec, w_mod, b_mod):
    depth, _, width = w_mod.shape
    rows = cvec.shape[0]
    tn = 1536
    return pl.pallas_call(
        _mod_kernel,
        grid=(depth, width // tn),
        in_specs=[pl.BlockSpec((rows, D_MODEL), lambda l, n: (0, 0)),
                  pl.BlockSpec((1, D_MODEL, tn), lambda l, n: (l, 0, n)),
                  pl.BlockSpec((1, 1, tn), lambda l, n: (l, 0, n))],
        out_specs=pl.BlockSpec((1, rows, tn), lambda l, n: (l, 0, n)),
        out_shape=jax.ShapeDtypeStruct((depth, rows, width), F32),
        compiler_params=_params("parallel", "parallel"),
        name="modulation",
    )(cvec, w_mod, b_mod.reshape(depth, 1, width))


def _from_token_vreg(f8_ref, rows):
    return jnp.concatenate([f8_ref[pl.ds(j, rows, stride=SUBLANES), :] for j in range(SUBLANES)], axis=1)


def _stream_specs(h, n_ctx_tiles):
    if isinstance(h, tuple):
        spec = lambda f: pl.BlockSpec((1, ROW_TILE, D_MODEL), lambda i, t: (i, f(t), 0))
        return [spec(lambda t: jnp.minimum(t, n_ctx_tiles - 1)), spec(lambda t: jnp.maximum(t - n_ctx_tiles, 0))], list(h)
    return [pl.BlockSpec((1, ROW_TILE, D_MODEL), lambda i, t: (i, t, 0))], [h]


def _stream_tile(refs, n_ctx_tiles):
    if len(refs) == 2:
        return jnp.where(pl.program_id(1) < n_ctx_tiles, refs[0][0], refs[1][0])
    return refs[0][0]


def _inproj_kernel(has_moe, n_stream, n_ctx_tiles, *refs):
    stream, refs = refs[:n_stream], refs[n_stream:]
    if has_moe:
        f8_ref, pmod_ref, mod_ref, nw_ref, w_ref = refs[:5]
        outs = refs[5:]
    else:
        mod_ref, nw_ref, w_ref = refs[:3]
        outs = refs[3:]
    x = _stream_tile(stream, n_ctx_tiles)
    if has_moe:
        x = x + pmod_ref[0, 0, 5:6, :] * _from_token_vreg(f8_ref, ROW_TILE)
        outs[6][0] = x
    mod = mod_ref[0, 0]
    u = (_rms(x) * nw_ref[...]) * (1.0 + mod[1:2, :]) + mod[0:1, :]
    ub = u.astype(BF16)
    z_ref, xbc_ref, dt_ref, gla_ref, glr_ref, mla_ref = outs[:6]
    z_ref[0] = _dot(ub, w_ref[:, IN_Z:IN_XBC]).astype(BF16)
    xbc_ref[0] = _dot(ub, w_ref[:, IN_XBC:IN_DT]).astype(BF16)
    dt_ref[0] = _dot(ub, w_ref[:, IN_DT:IN_GLA])
    gla_ref[0] = _dot(ub, w_ref[:, IN_GLA:IN_GLR]).astype(BF16)
    glr_ref[0] = _dot(ub, w_ref[:, IN_GLR:IN_MLA])
    mla_ref[0] = _dot(ub, w_ref[:, IN_MLA:IN_PAD_WIDTH]).astype(BF16)


def _in_projection(h, f8, prev_mod, mod, norm_w, w_pad, n_ctx_tiles):
    in_specs, args = _stream_specs(h, n_ctx_tiles)
    n_stream = len(args)
    b = args[0].shape[0]
    lt = sum(a.shape[1] for a in args)
    nt = lt // ROW_TILE
    has_moe = f8 is not None
    row = lambda w: pl.BlockSpec((1, ROW_TILE, w), lambda i, t: (i, t, 0))
    mod_spec = pl.BlockSpec((1, 1, N_MOD, D_MODEL), lambda i, t: (i, jnp.where(t >= n_ctx_tiles, 1, 0), 0, 0))
    if has_moe:
        in_specs += [pl.BlockSpec((ROW_TILE * SUBLANES, LANES), lambda i, t: (i * nt + t, 0)), mod_spec]
        args += [f8, prev_mod]
    in_specs += [mod_spec, pl.BlockSpec((1, D_MODEL), lambda i, t: (0, 0)),
                 pl.BlockSpec((D_MODEL, IN_PAD_WIDTH), lambda i, t: (0, 0))]
    args += [mod, norm_w.reshape(1, D_MODEL), w_pad]
    widths = [(SSD_D_INNER, BF16), (SSD_XBC, BF16), (LANES, F32), (4 * GLA_W, BF16), (LANES, F32), (MLA_IN_W, BF16)]
    if has_moe:
        widths.append((D_MODEL, F32))
    return pl.pallas_call(
        functools.partial(_inproj_kernel, has_moe, n_stream, n_ctx_tiles),
        grid=(b, nt),
        in_specs=in_specs,
        out_specs=[row(w) for w, _ in widths],
        out_shape=[jax.ShapeDtypeStruct((b, lt, w), dt) for w, dt in widths],
        compiler_params=_params("parallel", "parallel"),
        name="in_projection",
    )(*args)


def _conv_kernel(n_ctx_tiles, prev_ref, cur_ref, next_ref, w_ref, b_ref, o_ref):
    t = pl.program_id(1)
    nt = pl.num_programs(1)
    cur = cur_ref[0].astype(F32)
    has_prev = jnp.logical_and(t != 0, t != n_ctx_tiles)
    has_next = jnp.logical_and(t != n_ctx_tiles - 1, t != nt - 1)
    prev_row = jnp.where(has_prev, prev_ref[0][SUBLANES - 1:SUBLANES, :].astype(F32), 0.0)
    next_row = jnp.where(has_next, next_ref[0][0:1, :].astype(F32), 0.0)
    row = lax.broadcasted_iota(I32, cur.shape, 0)
    before = jnp.where(row == 0, prev_row, pltpu.roll(cur, 1, 0))
    after = jnp.where(row == ROW_TILE - 1, next_row, pltpu.roll(cur, ROW_TILE - 1, 0))
    y = w_ref[0:1, :] * before + w_ref[1:2, :] * cur + w_ref[2:3, :] * after + b_ref[...]
    o_ref[0] = _silu(y).astype(BF16)


def _ssd_conv(xbc, conv_w, conv_b, n_ctx_tiles):
    b, lt, w = xbc.shape
    nt = lt // ROW_TILE
    per = ROW_TILE // SUBLANES
    last = lt // SUBLANES - 1
    return pl.pallas_call(
        functools.partial(_conv_kernel, n_ctx_tiles),
        grid=(b, nt),
        in_specs=[pl.BlockSpec((1, SUBLANES, w), lambda i, t: (i, jnp.maximum(t * per - 1, 0), 0)),
                  pl.BlockSpec((1, ROW_TILE, w), lambda i, t: (i, t, 0)),
                  pl.BlockSpec((1, SUBLANES, w), lambda i, t: (i, jnp.minimum((t + 1) * per, last), 0)),
                  pl.BlockSpec((3, w), lambda i, t: (0, 0)),
                  pl.BlockSpec((1, w), lambda i, t: (0, 0))],
        out_specs=pl.BlockSpec((1, ROW_TILE, w), lambda i, t: (i, t, 0)),
        out_shape=jax.ShapeDtypeStruct((b, lt, w), BF16),
        compiler_params=_params("parallel", "parallel"),
        name="ssd_conv",
    )(xbc, xbc, xbc, conv_w, conv_b.reshape(1, w))


def _scan_chunk(reverse, n_ctx_chunks, n_chunks, j):
    if not reverse:
        return j
    return jnp.where(j < n_ctx_chunks, n_ctx_chunks - 1 - j, n_chunks - 1 - (j - n_ctx_chunks))


def _ssd_kernel(reverse, group, *refs):
    if reverse:
        (xbc_ref, dtc_ref, dtr_ref, acr_ref, acc_ref, dbr_ref, dbc_ref,
         yf_ref, z_ref, dsk_ref, nw_ref, out_ref, st_ref) = refs
    else:
        xbc_ref, dtc_ref, dtr_ref, acr_ref, acc_ref, dbr_ref, dbc_ref, out_ref, st_ref = refs
    q = CHUNK
    d = 1 if reverse else 0

    @pl.when(pl.program_id(1) == 0)
    def _():
        st_ref[...] = jnp.zeros_like(st_ref)

    row = lax.broadcasted_iota(I32, (q, q), 0)
    col = lax.broadcasted_iota(I32, (q, q), 1)
    if not reverse:
        lmat = jnp.where(row >= col, 1.0, 0.0).astype(BF16)
        umat = jnp.where(row <= col, 1.0, 0.0).astype(BF16)
        valid = row >= col
    else:
        lmat = jnp.where(row > col, 1.0, 0.0).astype(BF16)
        umat = jnp.where(row < col, 1.0, 0.0).astype(BF16)
        valid = col >= row
    hrow = lax.broadcasted_iota(I32, (SSD_HEADS, SSD_D_INNER), 0)
    hlane = lax.broadcasted_iota(I32, (SSD_HEADS, SSD_D_INNER), 1)
    expand = jnp.where(jnp.right_shift(hlane, 6) == hrow, 1.0, 0.0).astype(BF16)
    half = SSD_D_INNER // SSD_GROUPS
    lane_h = jnp.right_shift(lax.broadcasted_iota(I32, (q, half), 1), 6)
    lane_g = jnp.right_shift(lax.broadcasted_iota(I32, (q, LANES), 1), 6)

    for bi in range(group):
        xbc = xbc_ref[bi]
        xs = xbc[:, :SSD_D_INNER]
        bm = xbc[:, SSD_D_INNER:SSD_D_INNER + LANES]
        cm = xbc[:, SSD_D_INNER + LANES:]
        dt_c = _softplus(dtc_ref[bi][:, SSD_HEADS * d:SSD_HEADS * (d + 1)] + dbr_ref[0])
        dt_r = _softplus(dtr_ref[bi, 0] + dbc_ref[0])
        a_c = dt_c * acr_ref[0]
        a_r = dt_r * acc_ref[0]
        p_c = _dot_sel_lhs(lmat, a_c)
        p_r = _dot_sel_rhs(a_r, umat)
        if not reverse:
            tot = p_c[q - 1:q, :]
            inter_c = jnp.exp(p_c)
            w_c = jnp.exp(tot - p_c) * dt_c
        else:
            tot = p_c[q - 1:q, :] + a_c[q - 1:q, :]
            inter_c = jnp.exp(tot - p_c)
            w_c = jnp.exp(p_c) * dt_c
        inter_f = _dot(inter_c.astype(BF16), expand)
        w_f = _dot(w_c.astype(BF16), expand)
        dec_f = _dot_sel_rhs(jnp.broadcast_to(jnp.exp(tot), (SUBLANES, SSD_HEADS)), expand)[0:1]
        xs_f = xs.astype(F32)
        xw = (xs_f * w_f).astype(BF16)
        for g in range(SSD_GROUPS):
            sl = slice(g * half, (g + 1) * half)
            cg = jnp.where(lane_g == g, cm, jnp.zeros_like(cm))
            bg = jnp.where(lane_g == g, bm, jnp.zeros_like(bm))
            cb = _dot_nt(cg, bm)
            state = st_ref[bi, g]
            y = _dot(cg, state.astype(BF16)) * inter_f[:, sl]
            xs_g = xs[:, sl]
            scores, x_heads = [], []
            for hh in range(SSD_HEADS // SSD_GROUPS):
                h = g * (SSD_HEADS // SSD_GROUPS) + hh
                if not reverse:
                    seg = p_c[:, h:h + 1] - p_r[h:h + 1, :]
                else:
                    seg = p_r[h:h + 1, :] - p_c[:, h:h + 1]
                decay = jnp.exp(jnp.where(valid, seg, NEG))
                scores.append((cb * decay * dt_r[h:h + 1, :]).astype(BF16))
                x_heads.append(jnp.where(lane_h == hh, xs_g, jnp.zeros_like(xs_g)))
            y = y + _dot(jnp.concatenate(scores, axis=1), jnp.concatenate(x_heads, axis=0))
            st_ref[bi, g] = state * dec_f[:, sl] + _dot_tn(bg, xw[:, sl])
            if reverse:
                ytot = y + yf_ref[bi][:, sl].astype(F32) + dsk_ref[:, sl] * xs_f[:, sl]
                gated = ytot * _silu(z_ref[bi][:, sl].astype(F32))
                out_ref[bi, :, sl] = (_rms(gated) * nw_ref[:, sl]).astype(BF16)
            else:
                out_ref[bi, :, sl] = y.astype(BF16)


def _scan_group(b):
    return next(g for g in (4, 2, 1) if b % g == 0)


def _ssd_scan(reverse, xbc_act, dtc, dtr, a_coef, dt_bias, n_ctx_chunks, extra=None):
    b, lt, _ = xbc_act.shape
    nc = lt // CHUNK
    d = 1 if reverse else 0
    grp = _scan_group(b)
    cidx = functools.partial(_scan_chunk, reverse, n_ctx_chunks, nc)
    row = lambda w: pl.BlockSpec((grp, CHUNK, w), lambda i, j: (i, cidx(j), 0))
    small_r = pl.BlockSpec((1, 1, SSD_HEADS), lambda i, j: (d, 0, 0))
    small_c = pl.BlockSpec((1, SSD_HEADS, 1), lambda i, j: (d, 0, 0))
    in_specs = [row(SSD_XBC), row(LANES),
                pl.BlockSpec((grp, 1, SSD_HEADS, CHUNK), lambda i, j: (i, d, 0, cidx(j))),
                small_r, small_c, small_r, small_c]
    args = [xbc_act, dtc, dtr, a_coef.reshape(2, 1, SSD_HEADS), a_coef.reshape(2, SSD_HEADS, 1),
            dt_bias.reshape(2, 1, SSD_HEADS), dt_bias.reshape(2, SSD_HEADS, 1)]
    if reverse:
        yf, z, dsk, nw = extra
        vec = pl.BlockSpec((1, SSD_D_INNER), lambda i, j: (0, 0))
        in_specs += [row(SSD_D_INNER), row(SSD_D_INNER), vec, vec]
        args += [yf, z, dsk, nw]
    return pl.pallas_call(
        functools.partial(_ssd_kernel, reverse, grp),
        grid=(b // grp, nc),
        in_specs=in_specs,
        out_specs=row(SSD_D_INNER),
        out_shape=jax.ShapeDtypeStruct((b, lt, SSD_D_INNER), BF16),
        scratch_shapes=[pltpu.VMEM((grp, SSD_GROUPS, LANES, SSD_D_INNER // SSD_GROUPS), F32)],
        compiler_params=_params("parallel", "arbitrary"),
        name="ssd_scan_bwd" if reverse else "ssd_scan_fwd",
    )(*args)


def _gla_kernel(reverse, group, *refs):
    if reverse:
        gla_ref, glr_ref, wg_ref, bg_ref, of_ref, nw_ref, out_ref, st_ref = refs
    else:
        gla_ref, glr_ref, wg_ref, bg_ref, out_ref, st_ref = refs
    q = CHUNK
    w = GLA_W

    @pl.when(pl.program_id(1) == 0)
    def _():
        st_ref[...] = jnp.zeros_like(st_ref)

    row = lax.broadcasted_iota(I32, (q, q), 0)
    col = lax.broadcasted_iota(I32, (q, q), 1)
    if not reverse:
        cmat = jnp.where(row >= col, 1.0, 0.0).astype(BF16)
        valid = row >= col
    else:
        cmat = jnp.where(row > col, 1.0, 0.0).astype(BF16)
        valid = col >= row
    lane_h = jnp.right_shift(lax.broadcasted_iota(I32, (q, w), 1), 6)
    r2 = jnp.right_shift(lax.broadcasted_iota(I32, (w, w), 0), 6)
    c2 = jnp.right_shift(lax.broadcasted_iota(I32, (w, w), 1), 6)
    same_head = r2 == c2

    for bi in range(group):
        x = gla_ref[bi]
        qq = x[:, :w].astype(F32) * (64.0 ** -0.5)
        kk = x[:, w:2 * w].astype(F32)
        vv = x[:, 2 * w:3 * w]
        gpre = _dot3(glr_ref[bi], wg_ref[0]) + bg_ref[0]
        g = (jnp.minimum(gpre, 0.0) - jnp.log(1.0 + jnp.exp(-jnp.abs(gpre)))) * (1.0 / GLA_TAU)
        gc = _dot_sel_lhs(cmat, g)
        if not reverse:
            tot = gc[q - 1:q, :]
            q_dec = qq * jnp.exp(gc)
            k_inv = kk * jnp.exp(-gc)
            k_st = kk * jnp.exp(tot - gc)
        else:
            tot = gc[q - 1:q, :] + g[q - 1:q, :]
            q_dec = qq * jnp.exp(tot - gc)
            k_inv = kk * jnp.exp(gc - tot)
            k_st = kk * jnp.exp(gc)
        state = st_ref[bi]
        qd_b = q_dec.astype(BF16)
        ki_b = k_inv.astype(BF16)
        atts, v_heads = [], []
        for h in range(GLA_HEADS):
            att = _dot_nt(jnp.where(lane_h == h, qd_b, jnp.zeros_like(qd_b)), ki_b)
            atts.append(jnp.where(valid, att, 0.0).astype(BF16))
            v_heads.append(jnp.where(lane_h == h, vv, jnp.zeros_like(vv)))
        o = (_dot_nt(qd_b, state.astype(BF16))
             + _dot(jnp.concatenate(atts, axis=1), jnp.concatenate(v_heads, axis=0)))
        st_ref[bi] = jnp.where(same_head, state * jnp.exp(tot) + _dot_tn(vv, k_st.astype(BF16)), 0.0)
        if reverse:
            ot = o + of_ref[bi].astype(F32)
            ms = _dot_sel_rhs(ot * ot, jnp.where(same_head, 1.0, 0.0).astype(BF16)) * (1.0 / 64.0)
            rr = x[:, 3 * w:].astype(F32)
            out_ref[bi] = (ot * lax.rsqrt(ms + NORM_EPS) * nw_ref[...] * _silu(rr)).astype(BF16)
        else:
            out_ref[bi] = o.astype(BF16)


def _gla_scan(reverse, gla_in, glr, wg_pad, g_up_b, n_ctx_chunks, extra=None):
    b, lt, _ = gla_in.shape
    nc = lt // CHUNK
    d = 1 if reverse else 0
    grp = _scan_group(b)
    cidx = functools.partial(_scan_chunk, reverse, n_ctx_chunks, nc)
    row = lambda w: pl.BlockSpec((grp, CHUNK, w), lambda i, j: (i, cidx(j), 0))
    in_specs = [row(4 * GLA_W), row(LANES),
                pl.BlockSpec((1, LANES, GLA_W), lambda i, j: (d, 0, 0)),
                pl.BlockSpec((1, 1, GLA_W), lambda i, j: (d, 0, 0))]
    args = [gla_in, glr, wg_pad, g_up_b.reshape(2, 1, GLA_W)]
    if reverse:
        of, nw = extra
        in_specs += [row(GLA_W), pl.BlockSpec((1, GLA_W), lambda i, j: (0, 0))]
        args += [of, nw]
    return pl.pallas_call(
        functools.partial(_gla_kernel, reverse, grp),
        grid=(b // grp, nc),
        in_specs=in_specs,
        out_specs=row(GLA_W),
        out_shape=jax.ShapeDtypeStruct((b, lt, GLA_W), BF16),
        scratch_shapes=[pltpu.VMEM((grp, GLA_W, GLA_W), F32)],
        compiler_params=_params("parallel", "arbitrary"),
        name="gla_scan_bwd" if reverse else "gla_scan_fwd",
    )(*args)


def _mla_prep_kernel(m_ref, qnw_ref, wq_ref, wqs_ref, kvnw_ref, wk_ref, wv_ref, cos_ref, sin_ref,
                     q_ref, k_ref, v_ref):
    m = m_ref[0]
    cq = m[:, :2 * LANES].astype(F32)
    ckv = m[:, 2 * LANES:3 * LANES].astype(F32)
    ka = m[:, 3 * LANES:4 * LANES].astype(F32)
    kb = m[:, 4 * LANES:].astype(F32)
    cos = cos_ref[...]
    sin = sin_ref[...]
    cqn = (_rms(cq, MLA_Q_RANK) * qnw_ref[...]).astype(BF16)
    qm = _dot(cqn, wq_ref[...])
    qs = _dot(cqn, wqs_ref[...])
    ckn = (_rms(ckv) * kvnw_ref[...]).astype(BF16)
    kn = _dot(ckn, wk_ref[...])
    k_rot = ka * cos + kb * sin
    v_ref[0] = _dot(ckn, wv_ref[...]).astype(BF16)
    for h in range(MLA_HEADS):
        sl = slice(h * LANES, (h + 1) * LANES)
        q_ref[0, h] = ((qm[:, sl] * cos + qs[:, sl] * sin) * (MLA_QK ** -0.5)).astype(BF16)
        k_ref[0, h] = (kn[:, sl] + k_rot).astype(BF16)


def _mla_prep(mla_in, qnw, wq, wqs, kvnw, wk, wv, cos_t, sin_t):
    b, lt, _ = mla_in.shape
    nt = lt // ROW_TILE
    full = lambda a: pl.BlockSpec(a.shape, lambda i, t: (0,) * a.ndim)
    tab = pl.BlockSpec((ROW_TILE, LANES), lambda i, t: (t, 0))
    head = pl.BlockSpec((1, MLA_HEADS, ROW_TILE, LANES), lambda i, t: (i, 0, t, 0))
    return pl.pallas_call(
        _mla_prep_kernel,
        grid=(b, nt),
        in_specs=[pl.BlockSpec((1, ROW_TILE, MLA_IN_W), lambda i, t: (i, t, 0)),
                  full(qnw), full(wq), full(wqs), full(kvnw), full(wk), full(wv), tab, tab],
        out_specs=[head, head, pl.BlockSpec((1, ROW_TILE, MLA_HEADS * MLA_V), lambda i, t: (i, t, 0))],
        out_shape=[jax.ShapeDtypeStruct((b, MLA_HEADS, lt, LANES), BF16),
                   jax.ShapeDtypeStruct((b, MLA_HEADS, lt, LANES), BF16),
                   jax.ShapeDtypeStruct((b, lt, MLA_HEADS * MLA_V), BF16)],
        compiler_params=_params("parallel", "parallel"),
        name="mla_prep",
    )(mla_in, qnw, wq, wqs, kvnw, wk, wv, cos_t, sin_t)


def _attn_kernel(n_ctx, q_ref, k_ref, v_ref, o_ref):
    def attend(n_keys):
        width = MLA_HEADS * MLA_V
        lane_k = lax.broadcasted_iota(I32, (n_keys, width), 1)
        lane_q = lax.broadcasted_iota(I32, (ROW_TILE, width), 1)
        vv = v_ref[0, :n_keys, :]
        acc = jnp.zeros((ROW_TILE, width), F32)
        for h in range(MLA_HEADS):
            s = _dot_nt(q_ref[0, h], k_ref[0, h, :n_keys, :])
            p = jnp.exp((s - jnp.max(s, axis=-1, keepdims=True)).astype(BF16))
            ones_at = ((h + 1) % MLA_HEADS) * MLA_V
            vh = jnp.where(jnp.right_shift(lane_k, 6) == h, vv,
                           jnp.where(lane_k == ones_at, 1.0, 0.0).astype(vv.dtype))
            pv = _dot(p, vh)
            denom = pv[:, ones_at:ones_at + 1]
            acc = acc + jnp.where(jnp.right_shift(lane_q, 6) == h, pv * (1.0 / denom), 0.0)
        o_ref[0] = acc.astype(BF16)

    is_ctx = pl.program_id(1) < n_ctx // ROW_TILE
    pl.when(is_ctx)(lambda: attend(n_ctx))
    pl.when(jnp.logical_not(is_ctx))(lambda: attend(k_ref.shape[2]))


def _attention(q, k, v, n_ctx):
    b, _, lt, _ = q.shape
    nt = lt // ROW_TILE
    return pl.pallas_call(
        functools.partial(_attn_kernel, n_ctx),
        grid=(b, nt),
        in_specs=[pl.BlockSpec((1, MLA_HEADS, ROW_TILE, LANES), lambda i, t: (i, 0, t, 0)),
                  pl.BlockSpec((1, MLA_HEADS, lt, LANES), lambda i, t: (i, 0, 0, 0)),
                  pl.BlockSpec((1, lt, MLA_HEADS * MLA_V), lambda i, t: (i, 0, 0))],
        out_specs=pl.BlockSpec((1, ROW_TILE, MLA_HEADS * MLA_V), lambda i, t: (i, t, 0)),
        out_shape=jax.ShapeDtypeStruct((b, lt, MLA_HEADS * MLA_V), BF16),
        compiler_params=_params("parallel", "arbitrary"),
        name="mla_attention",
    )(q, k, v)


def _outproj_kernel(n_stream, n_ctx_tiles, *refs):
    stream, refs = refs[:n_stream], refs[n_stream:]
    (ssd_ref, gla_ref, mla_ref, mod_ref, w_ref, nw_ref, rw_ref, rb_ref,
     h_out_ref, v8_ref, idx_ref, gate_ref) = refs
    mod = mod_ref[0, 0]
    mix = (_dot(ssd_ref[0], w_ref[:SSD_D_INNER, :])
           + _dot(gla_ref[0], w_ref[SSD_D_INNER:SSD_D_INNER + GLA_W, :])
           + _dot(mla_ref[0], w_ref[SSD_D_INNER + GLA_W:, :]))
    hm = _stream_tile(stream, n_ctx_tiles) + mod[2:3, :] * mix
    h_out_ref[0] = hm
    v = (_rms(hm) * nw_ref[...]) * (1.0 + mod[4:5, :]) + mod[3:4, :]
    for j in range(SUBLANES):
        v8_ref[pl.ds(j, ROW_TILE, stride=SUBLANES), :] = v[:, j * LANES:(j + 1) * LANES]
    logits = _dot_nt(rw_ref[...], v.astype(BF16))[:N_EXPERTS, :] + rb_ref[...]
    expert = lax.broadcasted_iota(I32, logits.shape, 0).astype(F32)
    krow = lax.broadcasted_iota(I32, (SUBLANES, ROW_TILE), 0)
    idxs, vals = [], []
    for _ in range(TOP_K):
        top = jnp.max(logits, axis=0, keepdims=True)
        pick = jnp.min(jnp.where(logits == top, expert, float(N_EXPERTS)), axis=0, keepdims=True)
        idxs.append(pick)
        vals.append(top)
        logits = jnp.where(expert == pick, 2.0 * NEG, logits)
    exps = [jnp.exp(t - vals[0]) for t in vals]
    inv = 1.0 / (exps[0] + exps[1] + exps[2] + exps[3])
    idx_out = jnp.zeros(krow.shape, F32)
    gate_out = jnp.zeros(krow.shape, F32)
    for k in range(TOP_K):
        idx_out = jnp.where(krow == k, idxs[k], idx_out)
        gate_out = jnp.where(krow == k, exps[k] * inv, gate_out)
    idx_ref[0, 0] = idx_out.astype(I32)
    gate_ref[0, 0] = gate_out


def _out_projection(ssd_o, gla_o, mla_o, h, mod, w_out, norm_w, rw_pad, rb_pad, n_ctx_tiles):
    h_specs, h_args = _stream_specs(h, n_ctx_tiles)
    b, lt, _ = ssd_o.shape
    nt = lt // ROW_TILE
    row = lambda w: pl.BlockSpec((1, ROW_TILE, w), lambda i, t: (i, t, 0))
    full = lambda a: pl.BlockSpec(a.shape, lambda i, t: (0,) * a.ndim)
    mod_spec = pl.BlockSpec((1, 1, N_MOD, D_MODEL), lambda i, t: (i, jnp.where(t >= n_ctx_tiles, 1, 0), 0, 0))
    compact = pl.BlockSpec((1, 1, SUBLANES, ROW_TILE), lambda i, t: (i, t, 0, 0))
    nw = norm_w.reshape(1, D_MODEL)
    return pl.pallas_call(
        functools.partial(_outproj_kernel, len(h_args), n_ctx_tiles),
        grid=(b, nt),
        in_specs=h_specs + [row(SSD_D_INNER), row(GLA_W), row(MLA_HEADS * MLA_V), mod_spec,
                            full(w_out), full(nw), full(rw_pad), full(rb_pad)],
        out_specs=[row(D_MODEL),
                   pl.BlockSpec((ROW_TILE * SUBLANES, LANES), lambda i, t: (i * nt + t, 0)),
                   compact, compact],
        out_shape=[jax.ShapeDtypeStruct((b, lt, D_MODEL), F32),
                   jax.ShapeDtypeStruct((b * lt * SUBLANES, LANES), F32),
                   jax.ShapeDtypeStruct((b, nt, SUBLANES, ROW_TILE), I32),
                   jax.ShapeDtypeStruct((b, nt, SUBLANES, ROW_TILE), F32)],
        compiler_params=_params("parallel", "parallel"),
        name="out_projection",
    )(*h_args, ssd_o, gla_o, mla_o, mod, w_out, nw, rw_pad, rb_pad)


def _expert_w1_kernel(w_ref, g_ref, l_ref):
    blk = 2 * LANES
    row = lax.broadcasted_iota(I32, (blk, LANES), 0)
    col = lax.broadcasted_iota(I32, (blk, LANES), 1)
    even = jnp.where(row == 2 * col, 1.0, 0.0).astype(BF16)
    odd = jnp.where(row == 2 * col + 1, 1.0, 0.0).astype(BF16)
    for c in range(2 * EXPERT_FF // blk):
        w = w_ref[0, :, c * blk:(c + 1) * blk].astype(BF16)
        g_ref[0, :, c * LANES:(c + 1) * LANES] = _dot(w, even).astype(BF16)
        l_ref[0, :, c * LANES:(c + 1) * LANES] = _dot(w, odd).astype(BF16)


def _expert_w1(w1):
    n_e = w1.shape[0]
    out = pl.BlockSpec((1, D_MODEL, EXPERT_FF), lambda e: (e, 0, 0))
    return pl.pallas_call(
        _expert_w1_kernel,
        grid=(n_e,),
        in_specs=[pl.BlockSpec((1, D_MODEL, 2 * EXPERT_FF), lambda e: (e, 0, 0))],
        out_specs=[out, out],
        out_shape=[jax.ShapeDtypeStruct((n_e, D_MODEL, EXPERT_FF), BF16)] * 2,
        compiler_params=_params("parallel"),
        name="expert_w1_split",
    )(w1)


def _moe_kernel(ts, off_ref, end_ref, row_ref, gate_ref, x8_ref, w1g_ref, w1l_ref, w2_ref,
                b1g_ref, b1l_ref, b2_ref, f8_ref, xa_ref, xb_ref, ya_ref, yb_ref):
    s = pl.program_id(0)
    e = pl.program_id(1)
    batch = SUBLANES

    @pl.when(e == 0)
    def _():
        f8_ref[...] = jnp.zeros_like(f8_ref)

    base0 = off_ref[s * N_EXPERTS + e]
    end = end_ref[s * N_EXPERTS + e]
    rem = jnp.bitwise_and(end - base0, MOE_ROWS - 1)
    half_tail = jnp.logical_and(rem > 0, rem <= MOE_ROWS // 2)
    n_tiles = jnp.right_shift(end - base0, MOE_ROWS.bit_length() - 1) + jnp.where(rem > MOE_ROWS // 2, 1, 0)

    def token_tile(row):
        return pl.multiple_of(jnp.right_shift(row, TOP_K.bit_length() - 1) * SUBLANES, SUBLANES)

    def gather(base, xg_ref, r=MOE_ROWS):
        for rr in range(r):
            src = token_tile(row_ref[0, 0, base + rr])
            xg_ref[pl.ds(rr, SUBLANES, stride=MOE_PLANE), :] = x8_ref[pl.ds(src, SUBLANES), :]

    def ffn(xg_ref, yp_ref, r=MOE_ROWS):
        x = jnp.concatenate([xg_ref[j * MOE_PLANE:j * MOE_PLANE + r, :] for j in range(SUBLANES)],
                            axis=1).astype(BF16)
        glu = jnp.minimum(_dot(x, w1g_ref[0]) + b1g_ref[0], SWIGLU_LIMIT)
        lin = jnp.clip(_dot(x, w1l_ref[0]) + b1l_ref[0], -SWIGLU_LIMIT, SWIGLU_LIMIT)
        sig = 0.5 * jnp.tanh((0.5 * SWIGLU_ALPHA) * glu) + 0.5
        act = (glu * sig * (lin + 1.0)).astype(BF16)
        y = _dot(act, w2_ref[0]) + b2_ref[0]
        for j in range(SUBLANES):
            yp_ref[j * MOE_PLANE:j * MOE_PLANE + r, :] = y[:, j * LANES:(j + 1) * LANES]

    def scatter(base, yp_ref, r=MOE_ROWS):
        for r0 in range(0, r, batch):
            new = []
            for rr in range(r0, r0 + batch):
                row = row_ref[0, 0, base + rr]
                dst = token_tile(row)
                gate = jnp.where(base + rr < end, gate_ref[0, 0, row], 0.0)
                new.append((dst, f8_ref[pl.ds(dst, SUBLANES), :]
                            + gate * yp_ref[pl.ds(rr, SUBLANES, stride=MOE_PLANE), :]))
            for dst, val in reversed(new):
                f8_ref[pl.ds(dst, SUBLANES), :] = val

    def pair(i, carry):
        base = base0 + i * (2 * MOE_ROWS)
        gather(base, xa_ref)
        gather(base + MOE_ROWS, xb_ref)
        ffn(xa_ref, ya_ref)
        ffn(xb_ref, yb_ref)
        scatter(base, ya_ref)
        scatter(base + MOE_ROWS, yb_ref)
        return carry

    lax.fori_loop(0, jnp.right_shift(n_tiles, 1), pair, 0)

    @pl.when(jnp.bitwise_and(n_tiles, 1) == 1)
    def _():
        base = base0 + (n_tiles - 1) * MOE_ROWS
        gather(base, xa_ref)
        ffn(xa_ref, ya_ref)
        scatter(base, ya_ref)

    @pl.when(half_tail)
    def _():
        base = base0 + n_tiles * MOE_ROWS
        gather(base, xa_ref, MOE_ROWS // 2)
        ffn(xa_ref, ya_ref, MOE_ROWS // 2)
        scatter(base, ya_ref, MOE_ROWS // 2)


def _moe(x8, plan, w1g, w1l, w2, b1g, b1l, b2, ts):
    off, end, rows, gate = plan
    n_super = x8.shape[0] // (ts * SUBLANES)
    wspec = pl.BlockSpec((1, D_MODEL, EXPERT_FF), lambda s, e, *_: (e, 0, 0))
    w2spec = pl.BlockSpec((1, EXPERT_FF, D_MODEL), lambda s, e, *_: (e, 0, 0))
    bspec = pl.BlockSpec((1, 1, EXPERT_FF), lambda s, e, *_: (e, 0, 0))
    smem = lambda a: pl.BlockSpec((1, 1, a.shape[-1]), lambda s, e, *_: (s, 0, 0), memory_space=pltpu.SMEM)
    win = pl.BlockSpec((ts * SUBLANES, LANES), lambda s, e, *_: (s, 0), pipeline_mode=pl.Buffered(1))
    grid_spec = pltpu.PrefetchScalarGridSpec(
        num_scalar_prefetch=2,
        grid=(n_super, N_EXPERTS),
        in_specs=[smem(rows), smem(gate), win, wspec, wspec, w2spec, bspec, bspec, bspec],
        out_specs=pl.BlockSpec((ts * SUBLANES, LANES), lambda s, e, *_: (s, 0), pipeline_mode=pl.Buffered(1)),
        scratch_shapes=[pltpu.VMEM((SUBLANES * MOE_PLANE, LANES), F32)] * 4,
    )
    return pl.pallas_call(
        functools.partial(_moe_kernel, ts),
        grid_spec=grid_spec,
        out_shape=jax.ShapeDtypeStruct(x8.shape, F32),
        compiler_params=_params("arbitrary", "arbitrary"),
        name="moe_experts",
    )(off, end, rows, gate, x8, w1g, w1l, w2, b1g, b1l, b2)


def _moe_plan(idx, gate, ts):
    n_tok = idx.shape[0]
    n_super = n_tok // ts
    n_rows = ts * TOP_K
    flat_e = idx.reshape(n_super, n_rows)
    key = lax.sort(flat_e * n_rows + jnp.arange(n_rows, dtype=I32)[None, :], dimension=1)
    counts = jnp.sum(flat_e[:, :, None] == jnp.arange(N_EXPERTS, dtype=I32)[None, None, :], axis=1).astype(I32)
    end = jnp.cumsum(counts, axis=1)
    off = end - counts
    rows = jnp.pad(key % n_rows, ((0, 0), (0, MOE_ROWS)))
    return (off.reshape(-1), end.reshape(-1),
            rows.astype(I32).reshape(n_super, 1, n_rows + MOE_ROWS),
            gate.astype(F32).reshape(n_super, 1, n_rows))


def _final_kernel(h_ref, f8_ref, mod_ref, w_ref, o_ref):
    x = h_ref[0] + mod_ref[0, 0, 5:6, :] * _from_token_vreg(f8_ref, ROW_TILE)
    o_ref[0] = _rms(x) * w_ref[...]


def _final_norm(h, f8, mod, w, n_ctx_tiles):
    b, lt, _ = h.shape
    nt = lt // ROW_TILE
    nl = nt - n_ctx_tiles
    return pl.pallas_call(
        _final_kernel,
        grid=(b, nl),
        in_specs=[pl.BlockSpec((1, ROW_TILE, D_MODEL), lambda i, t: (i, t + n_ctx_tiles, 0)),
                  pl.BlockSpec((ROW_TILE * SUBLANES, LANES), lambda i, t: (i * nt + t + n_ctx_tiles, 0)),
                  pl.BlockSpec((1, 1, N_MOD, D_MODEL), lambda i, t: (i, 1, 0, 0)),
                  pl.BlockSpec((1, D_MODEL), lambda i, t: (0, 0))],
        out_specs=pl.BlockSpec((1, ROW_TILE, D_MODEL), lambda i, t: (i, t, 0)),
        out_shape=jax.ShapeDtypeStruct((b, nl * ROW_TILE, D_MODEL), F32),
        compiler_params=_params("parallel", "parallel"),
        name="final_norm",
    )(h, f8, mod, w.reshape(1, D_MODEL))


def _rope_partner():
    i = jnp.arange(MLA_ROPE)
    return jnp.where((i % 16) < 8, i + 8, i - 8)


def _pad_cols(w, width):
    return jnp.pad(w, ((0, 0), (0, width - w.shape[1])))


def _in_weight(w_in):
    sizes = (SSD_D_INNER, SSD_XBC, 2 * SSD_HEADS, GLA_W, GLA_W, GLA_W, GLA_W, 2 * GLA_GATE_RANK,
             MLA_Q_RANK, MLA_KV_RANK, MLA_ROPE)
    cols, acc = [], 0
    for s in sizes:
        cols.append(w_in[:, acc:acc + s])
        acc += s
    z, xbc, dt, gq, gk, gv, gr, glr, cq, ckv, kpe = cols
    rope_at = lambda w: jnp.pad(w, ((0, 0), (MLA_NOPE, LANES - MLA_NOPE - MLA_ROPE)))
    parts = [z, xbc, _pad_cols(dt, LANES), gq, gk, gv, gr, _pad_cols(glr, LANES), _pad_cols(cq, 2 * LANES), ckv,
             rope_at(kpe), rope_at(kpe[:, _rope_partner()])]
    return jnp.concatenate(parts, axis=1).astype(BF16)


def _mla_weights(w_uq, w_ukv):
    wq = w_uq.reshape(MLA_Q_RANK, MLA_HEADS, MLA_QK)
    nope, rope = wq[..., :MLA_NOPE], wq[..., MLA_NOPE:]
    zeros = jnp.zeros((MLA_Q_RANK, MLA_HEADS, LANES - MLA_QK), F32)
    main = jnp.concatenate([nope, rope, zeros], axis=-1)
    swap = jnp.concatenate([jnp.zeros_like(nope), rope[..., _rope_partner()], zeros], axis=-1)
    pad_rows = lambda w: jnp.pad(w.reshape(MLA_Q_RANK, MLA_HEADS * LANES), ((0, 2 * LANES - MLA_Q_RANK), (0, 0)))
    wkv = w_ukv.reshape(MLA_KV_RANK, MLA_HEADS, MLA_NOPE + MLA_V)
    wk = jnp.pad(wkv[..., :MLA_NOPE], ((0, 0), (0, 0), (0, LANES - MLA_NOPE))).reshape(MLA_KV_RANK, MLA_HEADS * LANES)
    wv = wkv[..., MLA_NOPE:].reshape(MLA_KV_RANK, MLA_HEADS * MLA_V)
    return pad_rows(main).astype(BF16), pad_rows(swap).astype(BF16), wk.astype(BF16), wv.astype(BF16)


def _rope_tables(n_ctx, n_lat):
    pos = jnp.arange(n_lat, dtype=F32)
    rowp = jnp.floor(pos / GRID_W)
    colp = pos - rowp * GRID_W
    half = MLA_ROPE // 2
    inv_freq = 1.0 / (ROPE_BASE ** (jnp.arange(0, half, 2, dtype=F32) / half))
    ang = jnp.stack([rowp[:, None] * inv_freq, colp[:, None] * inv_freq], axis=1)
    cos, sin = jnp.cos(ang), jnp.sin(ang)
    cos32 = jnp.concatenate([cos, cos], axis=2).reshape(n_lat, MLA_ROPE)
    sin32 = jnp.concatenate([-sin, sin], axis=2).reshape(n_lat, MLA_ROPE)
    cos32 = jnp.concatenate([jnp.ones((n_ctx, MLA_ROPE), F32), cos32], axis=0)
    sin32 = jnp.concatenate([jnp.zeros((n_ctx, MLA_ROPE), F32), sin32], axis=0)
    n = n_ctx + n_lat
    cos_t = jnp.concatenate([jnp.ones((n, MLA_NOPE), F32), cos32, jnp.zeros((n, LANES - MLA_QK), F32)], axis=1)
    sin_t = jnp.concatenate([jnp.zeros((n, MLA_NOPE), F32), sin32, jnp.zeros((n, LANES - MLA_QK), F32)], axis=1)
    return cos_t, sin_t


def kernel(x, c, ctx, c_ctx, w_mod, b_mod, norm1_w, w_in, ssd_conv_w, ssd_conv_b, ssd_a_log, ssd_dt_bias, ssd_d, ssd_norm_w, gla_g_up_w, gla_g_up_b, gla_norm_w, mla_q_norm_w, mla_w_uq, mla_kv_norm_w, mla_w_ukv, w_out, norm2_w, router_w, router_b, expert_w1, expert_b1, expert_w2, expert_b2, final_norm_w):
    b, n_lat, _ = x.shape
    n_ctx = ctx.shape[1]
    lt = n_ctx + n_lat
    depth = w_mod.shape[0]
    assert n_ctx % ROW_TILE == 0 and n_lat % ROW_TILE == 0 and n_lat % GRID_W == 0
    n_ctx_tiles = n_ctx // ROW_TILE
    n_ctx_chunks = n_ctx // CHUNK
    n_tok = b * lt
    ts = MOE_SUPER if n_tok % MOE_SUPER == 0 else n_tok
    assert b <= 16

    h = (ctx, x)
    cvec = jnp.concatenate([c, c_ctx[None, :], jnp.zeros((24 - b - 1, D_MODEL), F32)], axis=0)
    mod_all = _modulation(cvec, w_mod, b_mod).reshape(depth, 24, N_MOD, D_MODEL)
    mods = [jnp.stack([jnp.broadcast_to(mod_all[l, b], (b, N_MOD, D_MODEL)), mod_all[l, :b]], axis=1)
            for l in range(depth)]
    cos_t, sin_t = _rope_tables(n_ctx, n_lat)

    f8 = None
    for l in range(depth):
        outs = _in_projection(h, f8, mods[l - 1] if l else None, mods[l], norm1_w[l], _in_weight(w_in[l]),
                              n_ctx_tiles)
        z, xbc, dt, gla_in, glr, mla_in = outs[:6]
        if f8 is not None:
            h = outs[6]

        a_coef = -jnp.exp(ssd_a_log[l].astype(F32))
        xbc_act = _ssd_conv(xbc, ssd_conv_w[l], ssd_conv_b[l], n_ctx_tiles)
        dtr = jnp.transpose(dt[:, :, :2 * SSD_HEADS], (0, 2, 1)).reshape(b, 2, SSD_HEADS, lt)
        yf = _ssd_scan(False, xbc_act, dt, dtr, a_coef, ssd_dt_bias[l], n_ctx_chunks)
        dsk = jnp.repeat(ssd_d[l].astype(F32), SSD_D_INNER // SSD_HEADS).reshape(1, SSD_D_INNER)
        ssd_o = _ssd_scan(True, xbc_act, dt, dtr, a_coef, ssd_dt_bias[l], n_ctx_chunks,
                          extra=(yf, z, dsk, ssd_norm_w[l].reshape(1, SSD_D_INNER)))

        wg = jnp.zeros((2, LANES, GLA_W), F32)
        for d in range(2):
            wg = wg.at[d, d * GLA_GATE_RANK:(d + 1) * GLA_GATE_RANK].set(gla_g_up_w[l, d])
        of = _gla_scan(False, gla_in, glr, wg, gla_g_up_b[l], n_ctx_chunks)
        gnw = jnp.tile(gla_norm_w[l].astype(F32), GLA_HEADS).reshape(1, GLA_W)
        gla_o = _gla_scan(True, gla_in, glr, wg, gla_g_up_b[l], n_ctx_chunks, extra=(of, gnw))

        wq, wqs, wk, wv = _mla_weights(mla_w_uq[l], mla_w_ukv[l])
        qnw = jnp.pad(mla_q_norm_w[l], (0, 2 * LANES - MLA_Q_RANK)).reshape(1, 2 * LANES)
        qh, kh, vh = _mla_prep(mla_in, qnw, wq, wqs, mla_kv_norm_w[l].reshape(1, MLA_KV_RANK), wk, wv, cos_t, sin_t)
        mla_o = _attention(qh, kh, vh, n_ctx)

        rw = _pad_cols(router_w[l], LANES).astype(BF16).T
        rb = router_b[l].astype(F32).reshape(N_EXPERTS, 1)
        h, v8, idx, gate = _out_projection(ssd_o, gla_o, mla_o, h, mods[l], w_out[l].astype(BF16), norm2_w[l],
                                           rw, rb, n_ctx_tiles)

        per_token = lambda a: jnp.swapaxes(a[:, :, :TOP_K, :], 2, 3).reshape(b, lt, TOP_K)
        idx, gate = per_token(idx), per_token(gate)
        if l == depth - 1:
            idx = jnp.where((jnp.arange(lt) < n_ctx)[None, :, None], N_EXPERTS, idx)
        plan = _moe_plan(idx.reshape(n_tok, TOP_K), gate.reshape(n_tok, TOP_K), ts)
        w1g, w1l = _expert_w1(expert_w1[l])
        f8 = _moe(v8, plan, w1g, w1l, expert_w2[l].astype(BF16),
                  expert_b1[l][:, None, 0::2], expert_b1[l][:, None, 1::2], expert_b2[l][:, None, :], ts)

    return _final_norm(h, f8, mods[depth - 1], final_norm_w, n_ctx_tiles)
```

```python
import functools

import jax
import jax.numpy as jnp
from jax import lax
from jax.experimental import pallas as pl
from jax.experimental.pallas import tpu as pltpu

F32 = jnp.float32
BF16 = jnp.bfloat16
I32 = jnp.int32

D_MODEL = 1024
N_MOD = 6
NORM_EPS = 1e-6
SSD_HEADS, SSD_GROUPS, SSD_STATE = 8, 2, 64
SSD_D_INNER, SSD_XBC = 512, 768
GLA_HEADS, GLA_W, GLA_GATE_RANK, GLA_TAU = 4, 256, 16, 16.0
MLA_HEADS, MLA_Q_RANK, MLA_KV_RANK = 4, 192, 128
MLA_NOPE, MLA_ROPE, MLA_V, MLA_QK = 64, 32, 64, 96
GRID_W, ROPE_BASE = 64, 10000.0
N_EXPERTS, TOP_K, EXPERT_FF = 32, 4, 1024
SWIGLU_ALPHA, SWIGLU_LIMIT = 1.702, 7.0

LANES = 128
SUBLANES = 8
ROW_TILE = 256
CHUNK = 128
MOE_SUPER = 4096
MOE_ROWS = 128
MOE_PLANE = MOE_ROWS + SUBLANES
VMEM_LIMIT = 56 * 1024 * 1024
NEG = -1e30

IN_Z, IN_XBC, IN_DT, IN_GLA, IN_GLR, IN_MLA = 0, 512, 1280, 1408, 2432, 2560
IN_PAD_WIDTH = 3200
MLA_IN_W = 640


def _dot(a, b):
    return jnp.dot(a, b, preferred_element_type=F32)


def _dot_nt(a, b):
    return lax.dot_general(a, b, (((1,), (1,)), ((), ())), preferred_element_type=F32)


def _dot_tn(a, b):
    return lax.dot_general(a, b, (((0,), (0,)), ((), ())), preferred_element_type=F32)


def _split(x):
    hi = x.astype(BF16)
    lo = (x - hi.astype(F32)).astype(BF16)
    return hi, lo


def _dot_sel_rhs(x, m):
    hi, lo = _split(x)
    return _dot(hi, m) + _dot(lo, m)


def _dot_sel_lhs(m, x):
    hi, lo = _split(x)
    return _dot(m, hi) + _dot(m, lo)


def _dot3(a, b):
    ah, al = _split(a)
    bh, bl = _split(b)
    return _dot(ah, bh) + _dot(ah, bl) + _dot(al, bh)


def _softplus(x):
    return jnp.maximum(x, 0.0) + jnp.log(1.0 + jnp.exp(-jnp.abs(x)))


def _silu(x):
    return x * jax.nn.sigmoid(x)


def _rms(x, n=None):
    ms = jnp.sum(x * x, axis=-1, keepdims=True) * (1.0 / (n or x.shape[-1]))
    return x * lax.rsqrt(ms + NORM_EPS)


def _params(*sem):
    return pltpu.CompilerParams(dimension_semantics=sem, vmem_limit_bytes=VMEM_LIMIT)


def _mod_kernel(c_ref, w_ref, b_ref, o_ref):
    cv = c_ref[...]
    o_ref[0] = _dot3(_silu(cv), w_ref[0]) + b_ref[0]


def _modulation(cvec, w_mod, b_mod):
    depth, _, width = w_mod.shape
    rows = cvec.shape[0]
    tn = 1536
    return pl.pallas_call(
        _mod_kernel,
        grid=(depth, width // tn),
        in_specs=[pl.BlockSpec((rows, D_MODEL), lambda l, n: (0, 0)),
                  pl.BlockSpec((1, D_MODEL, tn), lambda l, n: (l, 0, n)),
                  pl.BlockSpec((1, 1, tn), lambda l, n: (l, 0, n))],
        out_specs=pl.BlockSpec((1, rows, tn), lambda l, n: (l, 0, n)),
        out_shape=jax.ShapeDtypeStruct((depth, rows, width), F32),
        compiler_params=_params("parallel", "parallel"),
        name="modulation",
    )(cvec, w_mod, b_mod.reshape(depth, 1, width))


def _from_token_vreg(f8_ref, rows):
    return jnp.concatenate([f8_ref[pl.ds(j, rows, stride=SUBLANES), :] for j in range(SUBLANES)], axis=1)


def _stream_specs(h, n_ctx_tiles):
    if isinstance(h, tuple):
        spec = lambda f: pl.BlockSpec((1, ROW_TILE, D_MODEL), lambda i, t: (i, f(t), 0))
        return [spec(lambda t: jnp.minimum(t, n_ctx_tiles - 1)), spec(lambda t: jnp.maximum(t - n_ctx_tiles, 0))], list(h)
    return [pl.BlockSpec((1, ROW_TILE, D_MODEL), lambda i, t: (i, t, 0))], [h]


def _stream_tile(refs, n_ctx_tiles):
    if len(refs) == 2:
        return jnp.where(pl.program_id(1) < n_ctx_tiles, refs[0][0], refs[1][0])
    return refs[0][0]


def _inproj_kernel(has_moe, n_stream, n_ctx_tiles, *refs):
    stream, refs = refs[:n_stream], refs[n_stream:]
    if has_moe:
        f8_ref, pmod_ref, mod_ref, nw_ref, w_ref = refs[:5]
        outs = refs[5:]
    else:
        mod_ref, nw_ref, w_ref = refs[:3]
        outs = refs[3:]
    x = _stream_tile(stream, n_ctx_tiles)
    if has_moe:
        x = x + pmod_ref[0, 0, 5:6, :] * _from_token_vreg(f8_ref, ROW_TILE)
        outs[6][0] = x
    mod = mod_ref[0, 0]
    u = (_rms(x) * nw_ref[...]) * (1.0 + mod[1:2, :]) + mod[0:1, :]
    ub = u.astype(BF16)
    z_ref, xbc_ref, dt_ref, gla_ref, glr_ref, mla_ref = outs[:6]
    z_ref[0] = _dot(ub, w_ref[:, IN_Z:IN_XBC]).astype(BF16)
    xbc_ref[0] = _dot(ub, w_ref[:, IN_XBC:IN_DT]).astype(BF16)
    dt_ref[0] = _dot(ub, w_ref[:, IN_DT:IN_GLA])
    gla_ref[0] = _dot(ub, w_ref[:, IN_GLA:IN_GLR]).astype(BF16)
    glr_ref[0] = _dot(ub, w_ref[:, IN_GLR:IN_MLA])
    mla_ref[0] = _dot(ub, w_ref[:, IN_MLA:IN_PAD_WIDTH]).astype(BF16)


def _in_projection(h, f8, prev_mod, mod, norm_w, w_pad, n_ctx_tiles):
    in_specs, args = _stream_specs(h, n_ctx_tiles)
    n_stream = len(args)
    b = args[0].shape[0]
    lt = sum(a.shape[1] for a in args)
    nt = lt // ROW_TILE
    has_moe = f8 is not None
    row = lambda w: pl.BlockSpec((1, ROW_TILE, w), lambda i, t: (i, t, 0))
    mod_spec = pl.BlockSpec((1, 1, N_MOD, D_MODEL), lambda i, t: (i, jnp.where(t >= n_ctx_tiles, 1, 0), 0, 0))
    if has_moe:
        in_specs += [pl.BlockSpec((ROW_TILE * SUBLANES, LANES), lambda i, t: (i * nt + t, 0)), mod_spec]
        args += [f8, prev_mod]
    in_specs += [mod_spec, pl.BlockSpec((1, D_MODEL), lambda i, t: (0, 0)),
                 pl.BlockSpec((D_MODEL, IN_PAD_WIDTH), lambda i, t: (0, 0))]
    args += [mod, norm_w.reshape(1, D_MODEL), w_pad]
    widths = [(SSD_D_INNER, BF16), (SSD_XBC, BF16), (LANES, F32), (4 * GLA_W, BF16), (LANES, F32), (MLA_IN_W, BF16)]
    if has_moe:
        widths.append((D_MODEL, F32))
    return pl.pallas_call(
        functools.partial(_inproj_kernel, has_moe, n_stream, n_ctx_tiles),
        grid=(b, nt),
        in_specs=in_specs,
        out_specs=[row(w) for w, _ in widths],
        out_shape=[jax.ShapeDtypeStruct((b, lt, w), dt) for w, dt in widths],
        compiler_params=_params("parallel", "parallel"),
        name="in_projection",
    )(*args)


def _conv_kernel(n_ctx_tiles, prev_ref, cur_ref, next_ref, w_ref, b_ref, o_ref):
    t = pl.program_id(1)
    nt = pl.num_programs(1)
    cur = cur_ref[0].astype(F32)
    has_prev = jnp.logical_and(t != 0, t != n_ctx_tiles)
    has_next = jnp.logical_and(t != n_ctx_tiles - 1, t != nt - 1)
    prev_row = jnp.where(has_prev, prev_ref[0][SUBLANES - 1:SUBLANES, :].astype(F32), 0.0)
    next_row = jnp.where(has_next, next_ref[0][0:1, :].astype(F32), 0.0)
    row = lax.broadcasted_iota(I32, cur.shape, 0)
    before = jnp.where(row == 0, prev_row, pltpu.roll(cur, 1, 0))
    after = jnp.where(row == ROW_TILE - 1, next_row, pltpu.roll(cur, ROW_TILE - 1, 0))
    y = w_ref[0:1, :] * before + w_ref[1:2, :] * cur + w_ref[2:3, :] * after + b_ref[...]
    o_ref[0] = _silu(y).astype(BF16)


def _ssd_conv(xbc, conv_w, conv_b, n_ctx_tiles):
    b, lt, w = xbc.shape
    nt = lt // ROW_TILE
    per = ROW_TILE // SUBLANES
    last = lt // SUBLANES - 1
    return pl.pallas_call(
        functools.partial(_conv_kernel, n_ctx_tiles),
        grid=(b, nt),
        in_specs=[pl.BlockSpec((1, SUBLANES, w), lambda i, t: (i, jnp.maximum(t * per - 1, 0), 0)),
                  pl.BlockSpec((1, ROW_TILE, w), lambda i, t: (i, t, 0)),
                  pl.BlockSpec((1, SUBLANES, w), lambda i, t: (i, jnp.minimum((t + 1) * per, last), 0)),
                  pl.BlockSpec((3, w), lambda i, t: (0, 0)),
                  pl.BlockSpec((1, w), lambda i, t: (0, 0))],
        out_specs=pl.BlockSpec((1, ROW_TILE, w), lambda i, t: (i, t, 0)),
        out_shape=jax.ShapeDtypeStruct((b, lt, w), BF16),
        compiler_params=_params("parallel", "parallel"),
        name="ssd_conv",
    )(xbc, xbc, xbc, conv_w, conv_b.reshape(1, w))


def _scan_chunk(reverse, n_ctx_chunks, n_chunks, j):
    if not reverse:
        return j
    return jnp.where(j < n_ctx_chunks, n_ctx_chunks - 1 - j, n_chunks - 1 - (j - n_ctx_chunks))


def _ssd_kernel(reverse, group, *refs):
    if reverse:
        (xbc_ref, dtc_ref, dtr_ref, acr_ref, acc_ref, dbr_ref, dbc_ref,
         yf_ref, z_ref, dsk_ref, nw_ref, out_ref, st_ref) = refs
    else:
        xbc_ref, dtc_ref, dtr_ref, acr_ref, acc_ref, dbr_ref, dbc_ref, out_ref, st_ref = refs
    q = CHUNK
    d = 1 if reverse else 0

    @pl.when(pl.program_id(1) == 0)
    def _():
        st_ref[...] = jnp.zeros_like(st_ref)

    row = lax.broadcasted_iota(I32, (q, q), 0)
    col = lax.broadcasted_iota(I32, (q, q), 1)
    if not reverse:
        lmat = jnp.where(row >= col, 1.0, 0.0).astype(BF16)
        umat = jnp.where(row <= col, 1.0, 0.0).astype(BF16)
        valid = row >= col
    else:
        lmat = jnp.where(row > col, 1.0, 0.0).astype(BF16)
        umat = jnp.where(row < col, 1.0, 0.0).astype(BF16)
        valid = col >= row
    hrow = lax.broadcasted_iota(I32, (SSD_HEADS, SSD_D_INNER), 0)
    hlane = lax.broadcasted_iota(I32, (SSD_HEADS, SSD_D_INNER), 1)
    expand = jnp.where(jnp.right_shift(hlane, 6) == hrow, 1.0, 0.0).astype(BF16)
    half = SSD_D_INNER // SSD_GROUPS
    lane_h = jnp.right_shift(lax.broadcasted_iota(I32, (q, half), 1), 6)
    lane_g = jnp.right_shift(lax.broadcasted_iota(I32, (q, LANES), 1), 6)

    for bi in range(group):
        xbc = xbc_ref[bi]
        xs = xbc[:, :SSD_D_INNER]
        bm = xbc[:, SSD_D_INNER:SSD_D_INNER + LANES]
        cm = xbc[:, SSD_D_INNER + LANES:]
        dt_c = _softplus(dtc_ref[bi][:, SSD_HEADS * d:SSD_HEADS * (d + 1)] + dbr_ref[0])
        dt_r = _softplus(dtr_ref[bi, 0] + dbc_ref[0])
        a_c = dt_c * acr_ref[0]
        a_r = dt_r * acc_ref[0]
        p_c = _dot_sel_lhs(lmat, a_c)
        p_r = _dot_sel_rhs(a_r, umat)
        if not reverse:
            tot = p_c[q - 1:q, :]
            inter_c = jnp.exp(p_c)
            w_c = jnp.exp(tot - p_c) * dt_c
        else:
            tot = p_c[q - 1:q, :] + a_c[q - 1:q, :]
            inter_c = jnp.exp(tot - p_c)
            w_c = jnp.exp(p_c) * dt_c
        inter_f = _dot(inter_c.astype(BF16), expand)
        w_f = _dot(w_c.astype(BF16), expand)
        dec_f = _dot_sel_rhs(jnp.broadcast_to(jnp.exp(tot), (SUBLANES, SSD_HEADS)), expand)[0:1]
        xs_f = xs.astype(F32)
        xw = (xs_f * w_f).astype(BF16)
        for g in range(SSD_GROUPS):
            sl = slice(g * half, (g + 1) * half)
            cg = jnp.where(lane_g == g, cm, jnp.zeros_like(cm))
            bg = jnp.where(lane_g == g, bm, jnp.zeros_like(bm))
            cb = _dot_nt(cg, bm)
            state = st_ref[bi, g]
            y = _dot(cg, state.astype(BF16)) * inter_f[:, sl]
            xs_g = xs[:, sl]
            scores, x_heads = [], []
            for hh in range(SSD_HEADS // SSD_GROUPS):
                h = g * (SSD_HEADS // SSD_GROUPS) + hh
                if not reverse:
                    seg = p_c[:, h:h + 1] - p_r[h:h + 1, :]
                else:
                    seg = p_r[h:h + 1, :] - p_c[:, h:h + 1]
                decay = jnp.exp(jnp.where(valid, seg, NEG))
                scores.append((cb * decay * dt_r[h:h + 1, :]).astype(BF16))
                x_heads.append(jnp.where(lane_h == hh, xs_g, jnp.zeros_like(xs_g)))
            y = y + _dot(jnp.concatenate(scores, axis=1), jnp.concatenate(x_heads, axis=0))
            st_ref[bi, g] = state * dec_f[:, sl] + _dot_tn(bg, xw[:, sl])
            if reverse:
                ytot = y + yf_ref[bi][:, sl].astype(F32) + dsk_ref[:, sl] * xs_f[:, sl]
                gated = ytot * _silu(z_ref[bi][:, sl].astype(F32))
                out_ref[bi, :, sl] = (_rms(gated) * nw_ref[:, sl]).astype(BF16)
            else:
                out_ref[bi, :, sl] = y.astype(BF16)


def _scan_group(b):
    return next(g for g in (8, 4, 2, 1) if b % g == 0)


def _ssd_scan(reverse, xbc_act, dtc, dtr, a_coef, dt_bias, n_ctx_chunks, extra=None):
    b, lt, _ = xbc_act.shape
    nc = lt // CHUNK
    d = 1 if reverse else 0
    grp = _scan_group(b)
    cidx = functools.partial(_scan_chunk, reverse, n_ctx_chunks, nc)
    row = lambda w: pl.BlockSpec((grp, CHUNK, w), lambda i, j: (i, cidx(j), 0))
    small_r = pl.BlockSpec((1, 1, SSD_HEADS), lambda i, j: (d, 0, 0))
    small_c = pl.BlockSpec((1, SSD_HEADS, 1), lambda i, j: (d, 0, 0))
    in_specs = [row(SSD_XBC), row(LANES),
                pl.BlockSpec((grp, 1, SSD_HEADS, CHUNK), lambda i, j: (i, d, 0, cidx(j))),
                small_r, small_c, small_r, small_c]
    args = [xbc_act, dtc, dtr, a_coef.reshape(2, 1, SSD_HEADS), a_coef.reshape(2, SSD_HEADS, 1),
            dt_bias.reshape(2, 1, SSD_HEADS), dt_bias.reshape(2, SSD_HEADS, 1)]
    if reverse:
        yf, z, dsk, nw = extra
        vec = pl.BlockSpec((1, SSD_D_INNER), lambda i, j: (0, 0))
        in_specs += [row(SSD_D_INNER), row(SSD_D_INNER), vec, vec]
        args += [yf, z, dsk, nw]
    return pl.pallas_call(
        functools.partial(_ssd_kernel, reverse, grp),
        grid=(b // grp, nc),
        in_specs=in_specs,
        out_specs=row(SSD_D_INNER),
        out_shape=jax.ShapeDtypeStruct((b, lt, SSD_D_INNER), BF16),
        scratch_shapes=[pltpu.VMEM((grp, SSD_GROUPS, LANES, SSD_D_INNER // SSD_GROUPS), F32)],
        compiler_params=_params("parallel", "arbitrary"),
        name="ssd_scan_bwd" if reverse else "ssd_scan_fwd",
    )(*args)


def _gla_kernel(reverse, group, *refs):
    if reverse:
        gla_ref, glr_ref, wg_ref, bg_ref, of_ref, nw_ref, out_ref, st_ref = refs
    else:
        gla_ref, glr_ref, wg_ref, bg_ref, out_ref, st_ref = refs
    q = CHUNK
    w = GLA_W

    @pl.when(pl.program_id(1) == 0)
    def _():
        st_ref[...] = jnp.zeros_like(st_ref)

    row = lax.broadcasted_iota(I32, (q, q), 0)
    col = lax.broadcasted_iota(I32, (q, q), 1)
    if not reverse:
        cmat = jnp.where(row >= col, 1.0, 0.0).astype(BF16)
        valid = row >= col
    else:
        cmat = jnp.where(row > col, 1.0, 0.0).astype(BF16)
        valid = col >= row
    lane_h = jnp.right_shift(lax.broadcasted_iota(I32, (q, w), 1), 6)
    r2 = jnp.right_shift(lax.broadcasted_iota(I32, (w, w), 0), 6)
    c2 = jnp.right_shift(lax.broadcasted_iota(I32, (w, w), 1), 6)
    same_head = r2 == c2

    for bi in range(group):
        x = gla_ref[bi]
        qq = x[:, :w].astype(F32) * (64.0 ** -0.5)
        kk = x[:, w:2 * w].astype(F32)
        vv = x[:, 2 * w:3 * w]
        gpre = _dot3(glr_ref[bi], wg_ref[0]) + bg_ref[0]
        g = (jnp.minimum(gpre, 0.0) - jnp.log(1.0 + jnp.exp(-jnp.abs(gpre)))) * (1.0 / GLA_TAU)
        gc = _dot_sel_lhs(cmat, g)
        if not reverse:
            tot = gc[q - 1:q, :]
            q_dec = qq * jnp.exp(gc)
            k_inv = kk * jnp.exp(-gc)
            k_st = kk * jnp.exp(tot - gc)
        else:
            tot = gc[q - 1:q, :] + g[q - 1:q, :]
            q_dec = qq * jnp.exp(tot - gc)
            k_inv = kk * jnp.exp(gc - tot)
            k_st = kk * jnp.exp(gc)
        state = st_ref[bi]
        qd_b = q_dec.astype(BF16)
        ki_b = k_inv.astype(BF16)
        atts, v_heads = [], []
        for h in range(GLA_HEADS):
            att = _dot_nt(jnp.where(lane_h == h, qd_b, jnp.zeros_like(qd_b)), ki_b)
            atts.append(jnp.where(valid, att, 0.0).astype(BF16))
            v_heads.append(jnp.where(lane_h == h, vv, jnp.zeros_like(vv)))
        o = (_dot_nt(qd_b, state.astype(BF16))
             + _dot(jnp.concatenate(atts, axis=1), jnp.concatenate(v_heads, axis=0)))
        st_ref[bi] = jnp.where(same_head, state * jnp.exp(tot) + _dot_tn(vv, k_st.astype(BF16)), 0.0)
        if reverse:
            ot = o + of_ref[bi].astype(F32)
            ms = _dot_sel_rhs(ot * ot, jnp.where(same_head, 1.0, 0.0).astype(BF16)) * (1.0 / 64.0)
            rr = x[:, 3 * w:].astype(F32)
            out_ref[bi] = (ot * lax.rsqrt(ms + NORM_EPS) * nw_ref[...] * _silu(rr)).astype(BF16)
        else:
            out_ref[bi] = o.astype(BF16)


def _gla_scan(reverse, gla_in, glr, wg_pad, g_up_b, n_ctx_chunks, extra=None):
    b, lt, _ = gla_in.shape
    nc = lt // CHUNK
    d = 1 if reverse else 0
    grp = _scan_group(b)
    cidx = functools.partial(_scan_chunk, reverse, n_ctx_chunks, nc)
    row = lambda w: pl.BlockSpec((grp, CHUNK, w), lambda i, j: (i, cidx(j), 0))
    in_specs = [row(4 * GLA_W), row(LANES),
                pl.BlockSpec((1, LANES, GLA_W), lambda i, j: (d, 0, 0)),
                pl.BlockSpec((1, 1, GLA_W), lambda i, j: (d, 0, 0))]
    args = [gla_in, glr, wg_pad, g_up_b.reshape(2, 1, GLA_W)]
    if reverse:
        of, nw = extra
        in_specs += [row(GLA_W), pl.BlockSpec((1, GLA_W), lambda i, j: (0, 0))]
        args += [of, nw]
    return pl.pallas_call(
        functools.partial(_gla_kernel, reverse, grp),
        grid=(b // grp, nc),
        in_specs=in_specs,
        out_specs=row(GLA_W),
        out_shape=jax.ShapeDtypeStruct((b, lt, GLA_W), BF16),
        scratch_shapes=[pltpu.VMEM((grp, GLA_W, GLA_W), F32)],
        compiler_params=_params("parallel", "arbitrary"),
        name="gla_scan_bwd" if reverse else "gla_scan_fwd",
    )(*args)


def _mla_prep_kernel(m_ref, qnw_ref, wq_ref, wqs_ref, kvnw_ref, wk_ref, wv_ref, cos_ref, sin_ref,
                     q_ref, k_ref, v_ref):
    m = m_ref[0]
    cq = m[:, :2 * LANES].astype(F32)
    ckv = m[:, 2 * LANES:3 * LANES].astype(F32)
    ka = m[:, 3 * LANES:4 * LANES].astype(F32)
    kb = m[:, 4 * LANES:].astype(F32)
    cos = cos_ref[...]
    sin = sin_ref[...]
    cqn = (_rms(cq, MLA_Q_RANK) * qnw_ref[...]).astype(BF16)
    qm = _dot(cqn, wq_ref[...])
    qs = _dot(cqn, wqs_ref[...])
    ckn = (_rms(ckv) * kvnw_ref[...]).astype(BF16)
    kn = _dot(ckn, wk_ref[...])
    k_rot = ka * cos + kb * sin
    v_ref[0] = _dot(ckn, wv_ref[...]).astype(BF16)
    for h in range(MLA_HEADS):
        sl = slice(h * LANES, (h + 1) * LANES)
        q_ref[0, h] = ((qm[:, sl] * cos + qs[:, sl] * sin) * (MLA_QK ** -0.5)).astype(BF16)
        k_ref[0, h] = (kn[:, sl] + k_rot).astype(BF16)


def _mla_prep(mla_in, qnw, wq, wqs, kvnw, wk, wv, cos_t, sin_t):
    b, lt, _ = mla_in.shape
    nt = lt // ROW_TILE
    full = lambda a: pl.BlockSpec(a.shape, lambda i, t: (0,) * a.ndim)
    tab = pl.BlockSpec((ROW_TILE, LANES), lambda i, t: (t, 0))
    head = pl.BlockSpec((1, MLA_HEADS, ROW_TILE, LANES), lambda i, t: (i, 0, t, 0))
    return pl.pallas_call(
        _mla_prep_kernel,
        grid=(b, nt),
        in_specs=[pl.BlockSpec((1, ROW_TILE, MLA_IN_W), lambda i, t: (i, t, 0)),
                  full(qnw), full(wq), full(wqs), full(kvnw), full(wk), full(wv), tab, tab],
        out_specs=[head, head, pl.BlockSpec((1, ROW_TILE, MLA_HEADS * MLA_V), lambda i, t: (i, t, 0))],
        out_shape=[jax.ShapeDtypeStruct((b, MLA_HEADS, lt, LANES), BF16),
                   jax.ShapeDtypeStruct((b, MLA_HEADS, lt, LANES), BF16),
                   jax.ShapeDtypeStruct((b, lt, MLA_HEADS * MLA_V), BF16)],
        compiler_params=_params("parallel", "parallel"),
        name="mla_prep",
    )(mla_in, qnw, wq, wqs, kvnw, wk, wv, cos_t, sin_t)


def _attn_kernel(n_ctx, q_ref, k_ref, v_ref, o_ref):
    def attend(n_keys):
        width = MLA_HEADS * MLA_V
        lane_k = lax.broadcasted_iota(I32, (n_keys, width), 1)
        lane_q = lax.broadcasted_iota(I32, (ROW_TILE, width), 1)
        vv = v_ref[0, :n_keys, :]
        acc = jnp.zeros((ROW_TILE, width), F32)
        for h in range(MLA_HEADS):
            s = _dot_nt(q_ref[0, h], k_ref[0, h, :n_keys, :])
            p = jnp.exp((s - jnp.max(s, axis=-1, keepdims=True)).astype(BF16))
            ones_at = ((h + 1) % MLA_HEADS) * MLA_V
            vh = jnp.where(jnp.right_shift(lane_k, 6) == h, vv,
                           jnp.where(lane_k == ones_at, 1.0, 0.0).astype(vv.dtype))
            pv = _dot(p, vh)
            denom = pv[:, ones_at:ones_at + 1]
            acc = acc + jnp.where(jnp.right_shift(lane_q, 6) == h, pv * (1.0 / denom), 0.0)
        o_ref[0] = acc.astype(BF16)

    is_ctx = pl.program_id(1) < n_ctx // ROW_TILE
    pl.when(is_ctx)(lambda: attend(n_ctx))
    pl.when(jnp.logical_not(is_ctx))(lambda: attend(k_ref.shape[2]))


def _attention(q, k, v, n_ctx):
    b, _, lt, _ = q.shape
    nt = lt // ROW_TILE
    return pl.pallas_call(
        functools.partial(_attn_kernel, n_ctx),
        grid=(b, nt),
        in_specs=[pl.BlockSpec((1, MLA_HEADS, ROW_TILE, LANES), lambda i, t: (i, 0, t, 0)),
                  pl.BlockSpec((1, MLA_HEADS, lt, LANES), lambda i, t: (i, 0, 0, 0)),
                  pl.BlockSpec((1, lt, MLA_HEADS * MLA_V), lambda i, t: (i, 0, 0))],
        out_specs=pl.BlockSpec((1, ROW_TILE, MLA_HEADS * MLA_V), lambda i, t: (i, t, 0)),
        out_shape=jax.ShapeDtypeStruct((b, lt, MLA_HEADS * MLA_V), BF16),
        compiler_params=_params("parallel", "arbitrary"),
        name="mla_attention",
    )(q, k, v)


def _outproj_kernel(n_stream, n_ctx_tiles, *refs):
    stream, refs = refs[:n_stream], refs[n_stream:]
    (ssd_ref, gla_ref, mla_ref, mod_ref, w_ref, nw_ref, rw_ref, rb_ref,
     h_out_ref, v8_ref, idx_ref, gate_ref) = refs
    mod = mod_ref[0, 0]
    mix = (_dot(ssd_ref[0], w_ref[:SSD_D_INNER, :])
           + _dot(gla_ref[0], w_ref[SSD_D_INNER:SSD_D_INNER + GLA_W, :])
           + _dot(mla_ref[0], w_ref[SSD_D_INNER + GLA_W:, :]))
    hm = _stream_tile(stream, n_ctx_tiles) + mod[2:3, :] * mix
    h_out_ref[0] = hm
    v = (_rms(hm) * nw_ref[...]) * (1.0 + mod[4:5, :]) + mod[3:4, :]
    for j in range(SUBLANES):
        v8_ref[pl.ds(j, ROW_TILE, stride=SUBLANES), :] = v[:, j * LANES:(j + 1) * LANES]
    logits = _dot_nt(rw_ref[...], v.astype(BF16))[:N_EXPERTS, :] + rb_ref[...]
    expert = lax.broadcasted_iota(I32, logits.shape, 0).astype(F32)
    krow = lax.broadcasted_iota(I32, (SUBLANES, ROW_TILE), 0)
    idxs, vals = [], []
    for _ in range(TOP_K):
        top = jnp.max(logits, axis=0, keepdims=True)
        pick = jnp.min(jnp.where(logits == top, expert, float(N_EXPERTS)), axis=0, keepdims=True)
        idxs.append(pick)
        vals.append(top)
        logits = jnp.where(expert == pick, 2.0 * NEG, logits)
    exps = [jnp.exp(t - vals[0]) for t in vals]
    inv = 1.0 / (exps[0] + exps[1] + exps[2] + exps[3])
    idx_out = jnp.zeros(krow.shape, F32)
    gate_out = jnp.zeros(krow.shape, F32)
    for k in range(TOP_K):
        idx_out = jnp.where(krow == k, idxs[k], idx_out)
        gate_out = jnp.where(krow == k, exps[k] * inv, gate_out)
    idx_ref[0, 0] = idx_out.astype(I32)
    gate_ref[0, 0] = gate_out


def _out_projection(ssd_o, gla_o, mla_o, h, mod, w_out, norm_w, rw_pad, rb_pad, n_ctx_tiles):
    h_specs, h_args = _stream_specs(h, n_ctx_tiles)
    b, lt, _ = ssd_o.shape
    nt = lt // ROW_TILE
    row = lambda w: pl.BlockSpec((1, ROW_TILE, w), lambda i, t: (i, t, 0))
    full = lambda a: pl.BlockSpec(a.shape, lambda i, t: (0,) * a.ndim)
    mod_spec = pl.BlockSpec((1, 1, N_MOD, D_MODEL), lambda i, t: (i, jnp.where(t >= n_ctx_tiles, 1, 0), 0, 0))
    compact = pl.BlockSpec((1, 1, SUBLANES, ROW_TILE), lambda i, t: (i, t, 0, 0))
    nw = norm_w.reshape(1, D_MODEL)
    return pl.pallas_call(
        functools.partial(_outproj_kernel, len(h_args), n_ctx_tiles),
        grid=(b, nt),
        in_specs=h_specs + [row(SSD_D_INNER), row(GLA_W), row(MLA_HEADS * MLA_V), mod_spec,
                            full(w_out), full(nw), full(rw_pad), full(rb_pad)],
        out_specs=[row(D_MODEL),
                   pl.BlockSpec((ROW_TILE * SUBLANES, LANES), lambda i, t: (i * nt + t, 0)),
                   compact, compact],
        out_shape=[jax.ShapeDtypeStruct((b, lt, D_MODEL), F32),
                   jax.ShapeDtypeStruct((b * lt * SUBLANES, LANES), F32),
                   jax.ShapeDtypeStruct((b, nt, SUBLANES, ROW_TILE), I32),
                   jax.ShapeDtypeStruct((b, nt, SUBLANES, ROW_TILE), F32)],
        compiler_params=_params("parallel", "parallel"),
        name="out_projection",
    )(*h_args, ssd_o, gla_o, mla_o, mod, w_out, nw, rw_pad, rb_pad)


def _expert_w1_kernel(w_ref, g_ref, l_ref):
    blk = 2 * LANES
    row = lax.broadcasted_iota(I32, (blk, LANES), 0)
    col = lax.broadcasted_iota(I32, (blk, LANES), 1)
    even = jnp.where(row == 2 * col, 1.0, 0.0).astype(BF16)
    odd = jnp.where(row == 2 * col + 1, 1.0, 0.0).astype(BF16)
    for c in range(2 * EXPERT_FF // blk):
        w = w_ref[0, :, c * blk:(c + 1) * blk].astype(BF16)
        g_ref[0, :, c * LANES:(c + 1) * LANES] = _dot(w, even).astype(BF16)
        l_ref[0, :, c * LANES:(c + 1) * LANES] = _dot(w, odd).astype(BF16)


def _expert_w1(w1):
    n_e = w1.shape[0]
    out = pl.BlockSpec((1, D_MODEL, EXPERT_FF), lambda e: (e, 0, 0))
    return pl.pallas_call(
        _expert_w1_kernel,
        grid=(n_e,),
        in_specs=[pl.BlockSpec((1, D_MODEL, 2 * EXPERT_FF), lambda e: (e, 0, 0))],
        out_specs=[out, out],
        out_shape=[jax.ShapeDtypeStruct((n_e, D_MODEL, EXPERT_FF), BF16)] * 2,
        compiler_params=_params("parallel"),
        name="expert_w1_split",
    )(w1)


def _moe_kernel(ts, off_ref, end_ref, row_ref, gate_ref, x8_ref, w1g_ref, w1l_ref, w2_ref,
                b1g_ref, b1l_ref, b2_ref, f8_ref, xa_ref, xb_ref, ya_ref, yb_ref):
    s = pl.program_id(0)
    e = pl.program_id(1)
    batch = SUBLANES

    @pl.when(e == 0)
    def _():
        f8_ref[...] = jnp.zeros_like(f8_ref)

    base0 = off_ref[s * N_EXPERTS + e]
    end = end_ref[s * N_EXPERTS + e]
    rem = jnp.bitwise_and(end - base0, MOE_ROWS - 1)
    half_tail = jnp.logical_and(rem > 0, rem <= MOE_ROWS // 2)
    n_tiles = jnp.right_shift(end - base0, MOE_ROWS.bit_length() - 1) + jnp.where(rem > MOE_ROWS // 2, 1, 0)

    def token_tile(row):
        return pl.multiple_of(jnp.right_shift(row, TOP_K.bit_length() - 1) * SUBLANES, SUBLANES)

    def gather(base, xg_ref, r=MOE_ROWS):
        for rr in range(r):
            src = token_tile(row_ref[0, 0, base + rr])
            xg_ref[pl.ds(rr, SUBLANES, stride=MOE_PLANE), :] = x8_ref[pl.ds(src, SUBLANES), :]

    def ffn(xg_ref, yp_ref, r=MOE_ROWS):
        x = jnp.concatenate([xg_ref[j * MOE_PLANE:j * MOE_PLANE + r, :] for j in range(SUBLANES)],
                            axis=1).astype(BF16)
        glu = jnp.minimum(_dot(x, w1g_ref[0]) + b1g_ref[0], SWIGLU_LIMIT)
        lin = jnp.clip(_dot(x, w1l_ref[0]) + b1l_ref[0], -SWIGLU_LIMIT, SWIGLU_LIMIT)
        sig = 0.5 * jnp.tanh((0.5 * SWIGLU_ALPHA) * glu) + 0.5
        act = (glu * sig * (lin + 1.0)).astype(BF16)
        y = _dot(act, w2_ref[0]) + b2_ref[0]
        for j in range(SUBLANES):
            yp_ref[j * MOE_PLANE:j * MOE_PLANE + r, :] = y[:, j * LANES:(j + 1) * LANES]

    def scatter(base, yp_ref, r=MOE_ROWS):
        for r0 in range(0, r, batch):
            new = []
            for rr in range(r0, r0 + batch):
                row = row_ref[0, 0, base + rr]
                dst = token_tile(row)
                gate = jnp.where(base + rr < end, gate_ref[0, 0, row], 0.0)
                new.append((dst, f8_ref[pl.ds(dst, SUBLANES), :]
                            + gate * yp_ref[pl.ds(rr, SUBLANES, stride=MOE_PLANE), :]))
            for dst, val in reversed(new):
                f8_ref[pl.ds(dst, SUBLANES), :] = val

    def pair(i, carry):
        base = base0 + i * (2 * MOE_ROWS)
        gather(base, xa_ref)
        gather(base + MOE_ROWS, xb_ref)
        ffn(xa_ref, ya_ref)
        ffn(xb_ref, yb_ref)
        scatter(base, ya_ref)
        scatter(base + MOE_ROWS, yb_ref)
        return carry

    lax.fori_loop(0, jnp.right_shift(n_tiles, 1), pair, 0)

    @pl.when(jnp.bitwise_and(n_tiles, 1) == 1)
    def _():
        base = base0 + (n_tiles - 1) * MOE_ROWS
        gather(base, xa_ref)
        ffn(xa_ref, ya_ref)
        scatter(base, ya_ref)

    @pl.when(half_tail)
    def _():
        base = base0 + n_tiles * MOE_ROWS
        gather(base, xa_ref, MOE_ROWS // 2)
        ffn(xa_ref, ya_ref, MOE_ROWS // 2)
        scatter(base, ya_ref, MOE_ROWS // 2)


def _moe(x8, plan, w1g, w1l, w2, b1g, b1l, b2, ts):
    off, end, rows, gate = plan
    n_super = x8.shape[0] // (ts * SUBLANES)
    wspec = pl.BlockSpec((1, D_MODEL, EXPERT_FF), lambda s, e, *_: (e, 0, 0))
    w2spec = pl.BlockSpec((1, EXPERT_FF, D_MODEL), lambda s, e, *_: (e, 0, 0))
    bspec = pl.BlockSpec((1, 1, EXPERT_FF), lambda s, e, *_: (e, 0, 0))
    smem = lambda a: pl.BlockSpec((1, 1, a.shape[-1]), lambda s, e, *_: (s, 0, 0), memory_space=pltpu.SMEM)
    win = pl.BlockSpec((ts * SUBLANES, LANES), lambda s, e, *_: (s, 0), pipeline_mode=pl.Buffered(1))
    grid_spec = pltpu.PrefetchScalarGridSpec(
        num_scalar_prefetch=2,
        grid=(n_super, N_EXPERTS),
        in_specs=[smem(rows), smem(gate), win, wspec, wspec, w2spec, bspec, bspec, bspec],
        out_specs=pl.BlockSpec((ts * SUBLANES, LANES), lambda s, e, *_: (s, 0), pipeline_mode=pl.Buffered(1)),
        scratch_shapes=[pltpu.VMEM((SUBLANES * MOE_PLANE, LANES), F32)] * 4,
    )
    return pl.pallas_call(
        functools.partial(_moe_kernel, ts),
        grid_spec=grid_spec,
        out_shape=jax.ShapeDtypeStruct(x8.shape, F32),
        compiler_params=_params("arbitrary", "arbitrary"),
        name="moe_experts",
    )(off, end, rows, gate, x8, w1g, w1l, w2, b1g, b1l, b2)


def _moe_plan(idx, gate, ts):
    n_tok = idx.shape[0]
    n_super = n_tok // ts
    n_rows = ts * TOP_K
    flat_e = idx.reshape(n_super, n_rows)
    key = lax.sort(flat_e * n_rows + jnp.arange(n_rows, dtype=I32)[None, :], dimension=1)
    counts = jnp.sum(flat_e[:, :, None] == jnp.arange(N_EXPERTS, dtype=I32)[None, None, :], axis=1).astype(I32)
    end = jnp.cumsum(counts, axis=1)
    off = end - counts
    rows = jnp.pad(key % n_rows, ((0, 0), (0, MOE_ROWS)))
    return (off.reshape(-1), end.reshape(-1),
            rows.astype(I32).reshape(n_super, 1, n_rows + MOE_ROWS),
            gate.astype(F32).reshape(n_super, 1, n_rows))


def _final_kernel(h_ref, f8_ref, mod_ref, w_ref, o_ref):
    x = h_ref[0] + mod_ref[0, 0, 5:6, :] * _from_token_vreg(f8_ref, ROW_TILE)
    o_ref[0] = _rms(x) * w_ref[...]


def _final_norm(h, f8, mod, w, n_ctx_tiles):
    b, lt, _ = h.shape
    nt = lt // ROW_TILE
    nl = nt - n_ctx_tiles
    return pl.pallas_call(
        _final_kernel,
        grid=(b, nl),
        in_specs=[pl.BlockSpec((1, ROW_TILE, D_MODEL), lambda i, t: (i, t + n_ctx_tiles, 0)),
                  pl.BlockSpec((ROW_TILE * SUBLANES, LANES), lambda i, t: (i * nt + t + n_ctx_tiles, 0)),
                  pl.BlockSpec((1, 1, N_MOD, D_MODEL), lambda i, t: (i, 1, 0, 0)),
                  pl.BlockSpec((1, D_MODEL), lambda i, t: (0, 0))],
        out_specs=pl.BlockSpec((1, ROW_TILE, D_MODEL), lambda i, t: (i, t, 0)),
        out_shape=jax.ShapeDtypeStruct((b, nl * ROW_TILE, D_MODEL), F32),
        compiler_params=_params("parallel", "parallel"),
        name="final_norm",
    )(h, f8, mod, w.reshape(1, D_MODEL))


def _rope_partner():
    i = jnp.arange(MLA_ROPE)
    return jnp.where((i % 16) < 8, i + 8, i - 8)


def _pad_cols(w, width):
    return jnp.pad(w, ((0, 0), (0, width - w.shape[1])))


def _in_weight(w_in):
    sizes = (SSD_D_INNER, SSD_XBC, 2 * SSD_HEADS, GLA_W, GLA_W, GLA_W, GLA_W, 2 * GLA_GATE_RANK,
             MLA_Q_RANK, MLA_KV_RANK, MLA_ROPE)
    cols, acc = [], 0
    for s in sizes:
        cols.append(w_in[:, acc:acc + s])
        acc += s
    z, xbc, dt, gq, gk, gv, gr, glr, cq, ckv, kpe = cols
    rope_at = lambda w: jnp.pad(w, ((0, 0), (MLA_NOPE, LANES - MLA_NOPE - MLA_ROPE)))
    parts = [z, xbc, _pad_cols(dt, LANES), gq, gk, gv, gr, _pad_cols(glr, LANES), _pad_cols(cq, 2 * LANES), ckv,
             rope_at(kpe), rope_at(kpe[:, _rope_partner()])]
    return jnp.concatenate(parts, axis=1).astype(BF16)


def _mla_weights(w_uq, w_ukv):
    wq = w_uq.reshape(MLA_Q_RANK, MLA_HEADS, MLA_QK)
    nope, rope = wq[..., :MLA_NOPE], wq[..., MLA_NOPE:]
    zeros = jnp.zeros((MLA_Q_RANK, MLA_HEADS, LANES - MLA_QK), F32)
    main = jnp.concatenate([nope, rope, zeros], axis=-1)
    swap = jnp.concatenate([jnp.zeros_like(nope), rope[..., _rope_partner()], zeros], axis=-1)
    pad_rows = lambda w: jnp.pad(w.reshape(MLA_Q_RANK, MLA_HEADS * LANES), ((0, 2 * LANES - MLA_Q_RANK), (0, 0)))
    wkv = w_ukv.reshape(MLA_KV_RANK, MLA_HEADS, MLA_NOPE + MLA_V)
    wk = jnp.pad(wkv[..., :MLA_NOPE], ((0, 0), (0, 0), (0, LANES - MLA_NOPE))).reshape(MLA_KV_RANK, MLA_HEADS * LANES)
    wv = wkv[..., MLA_NOPE:].reshape(MLA_KV_RANK, MLA_HEADS * MLA_V)
    return pad_rows(main).astype(BF16), pad_rows(swap).astype(BF16), wk.astype(BF16), wv.astype(BF16)


def _rope_tables(n_ctx, n_lat):
    pos = jnp.arange(n_lat, dtype=F32)
    rowp = jnp.floor(pos / GRID_W)
    colp = pos - rowp * GRID_W
    half = MLA_ROPE // 2
    inv_freq = 1.0 / (ROPE_BASE ** (jnp.arange(0, half, 2, dtype=F32) / half))
    ang = jnp.stack([rowp[:, None] * inv_freq, colp[:, None] * inv_freq], axis=1)
    cos, sin = jnp.cos(ang), jnp.sin(ang)
    cos32 = jnp.concatenate([cos, cos], axis=2).reshape(n_lat, MLA_ROPE)
    sin32 = jnp.concatenate([-sin, sin], axis=2).reshape(n_lat, MLA_ROPE)
    cos32 = jnp.concatenate([jnp.ones((n_ctx, MLA_ROPE), F32), cos32], axis=0)
    sin32 = jnp.concatenate([jnp.zeros((n_ctx, MLA_ROPE), F32), sin32], axis=0)
    n = n_ctx + n_lat
    cos_t = jnp.concatenate([jnp.ones((n, MLA_NOPE), F32), cos32, jnp.zeros((n, LANES - MLA_QK), F32)], axis=1)
    sin_t = jnp.concatenate([jnp.zeros((n, MLA_NOPE), F32), sin32, jnp.zeros((n, LANES - MLA_QK), F32)], axis=1)
    return cos_t, sin_t


def kernel(x, c, ctx, c_ctx, w_mod, b_mod, norm1_w, w_in, ssd_conv_w, ssd_conv_b, ssd_a_log, ssd_dt_bias, ssd_d, ssd_norm_w, gla_g_up_w, gla_g_up_b, gla_norm_w, mla_q_norm_w, mla_w_uq, mla_kv_norm_w, mla_w_ukv, w_out, norm2_w, router_w, router_b, expert_w1, expert_b1, expert_w2, expert_b2, final_norm_w):
    b, n_lat, _ = x.shape
    n_ctx = ctx.shape[1]
    lt = n_ctx + n_lat
    depth = w_mod.shape[0]
    assert n_ctx % ROW_TILE == 0 and n_lat % ROW_TILE == 0 and n_lat % GRID_W == 0
    n_ctx_tiles = n_ctx // ROW_TILE
    n_ctx_chunks = n_ctx // CHUNK
    n_tok = b * lt
    ts = MOE_SUPER if n_tok % MOE_SUPER == 0 else n_tok
    assert b <= 16

    h = (ctx, x)
    cvec = jnp.concatenate([c, c_ctx[None, :], jnp.zeros((24 - b - 1, D_MODEL), F32)], axis=0)
    mod_all = _modulation(cvec, w_mod, b_mod).reshape(depth, 24, N_MOD, D_MODEL)
    mods = [jnp.stack([jnp.broadcast_to(mod_all[l, b], (b, N_MOD, D_MODEL)), mod_all[l, :b]], axis=1)
            for l in range(depth)]
    cos_t, sin_t = _rope_tables(n_ctx, n_lat)

    f8 = None
    for l in range(depth):
        outs = _in_projection(h, f8, mods[l - 1] if l else None, mods[l], norm1_w[l], _in_weight(w_in[l]),
                              n_ctx_tiles)
        z, xbc, dt, gla_in, glr, mla_in = outs[:6]
        if f8 is not None:
            h = outs[6]

        a_coef = -jnp.exp(ssd_a_log[l].astype(F32))
        xbc_act = _ssd_conv(xbc, ssd_conv_w[l], ssd_conv_b[l], n_ctx_tiles)
        dtr = jnp.transpose(dt[:, :, :2 * SSD_HEADS], (0, 2, 1)).reshape(b, 2, SSD_HEADS, lt)
        yf = _ssd_scan(False, xbc_act, dt, dtr, a_coef, ssd_dt_bias[l], n_ctx_chunks)
        dsk = jnp.repeat(ssd_d[l].astype(F32), SSD_D_INNER // SSD_HEADS).reshape(1, SSD_D_INNER)
        ssd_o = _ssd_scan(True, xbc_act, dt, dtr, a_coef, ssd_dt_bias[l], n_ctx_chunks,
                          extra=(yf, z, dsk, ssd_norm_w[l].reshape(1, SSD_D_INNER)))

        wg = jnp.zeros((2, LANES, GLA_W), F32)
        for d in range(2):
            wg = wg.at[d, d * GLA_GATE_RANK:(d + 1) * GLA_GATE_RANK].set(gla_g_up_w[l, d])
        of = _gla_scan(False, gla_in, glr, wg, gla_g_up_b[l], n_ctx_chunks)
        gnw = jnp.tile(gla_norm_w[l].astype(F32), GLA_HEADS).reshape(1, GLA_W)
        gla_o = _gla_scan(True, gla_in, glr, wg, gla_g_up_b[l], n_ctx_chunks, extra=(of, gnw))

        wq, wqs, wk, wv = _mla_weights(mla_w_uq[l], mla_w_ukv[l])
        qnw = jnp.pad(mla_q_norm_w[l], (0, 2 * LANES - MLA_Q_RANK)).reshape(1, 2 * LANES)
        qh, kh, vh = _mla_prep(mla_in, qnw, wq, wqs, mla_kv_norm_w[l].reshape(1, MLA_KV_RANK), wk, wv, cos_t, sin_t)
        mla_o = _attention(qh, kh, vh, n_ctx)

        rw = _pad_cols(router_w[l], LANES).astype(BF16).T
        rb = router_b[l].astype(F32).reshape(N_EXPERTS, 1)
        h, v8, idx, gate = _out_projection(ssd_o, gla_o, mla_o, h, mods[l], w_out[l].astype(BF16), norm2_w[l],
                                           rw, rb, n_ctx_tiles)

        per_token = lambda a: jnp.swapaxes(a[:, :, :TOP_K, :], 2, 3).reshape(b, lt, TOP_K)
        idx, gate = per_token(idx), per_token(gate)
        if l == depth - 1:
            idx = jnp.where((jnp.arange(lt) < n_ctx)[None, :, None], N_EXPERTS, idx)
        plan = _moe_plan(idx.reshape(n_tok, TOP_K), gate.reshape(n_tok, TOP_K), ts)
        w1g, w1l = _expert_w1(expert_w1[l])
        f8 = _moe(v8, plan, w1g, w1l, expert_w2[l].astype(BF16),
                  expert_b1[l][:, None, 0::2], expert_b1[l][:, None, 1::2], expert_b2[l][:, None, :], ts)

    return _final_norm(h, f8, mods[depth - 1], final_norm_w, n_ctx_tiles)
```
